```python
import jax, jax.numpy as jnp
from jax import lax
import numpy as np

D_MODEL = 1024
BATCH = 32
SEQ = 256
DEPTH = 2
DEC_BATCH = 4
DEC_SEQ = 4096
PAST_LEN = 256

GRID_W = 64
N_BRANCH = 4
BR_WIDTH = 512
EPS = 1e-6
Q_BLOCK = 128
CONV_K = 4

RW_HEADS = 8
RW_HEAD = 64
RW_DECAY_LORA = 64
RW_A_LORA = 64
RW_GN_EPS = 64e-5
RW_PRE = 3 * BR_WIDTH + RW_DECAY_LORA + RW_A_LORA

MLA_HEADS = 8
MLA_NOPE = 64
MLA_ROPE = 32
MLA_QK = MLA_NOPE + MLA_ROPE
MLA_V = 64
MLA_Q_RANK = 384
MLA_KV_RANK = 256
ROPE_THETA = 10000.0

SSD_HEADS = 8
SSD_HEAD_DIM = 64
SSD_GROUPS = 2
SSD_HPG = SSD_HEADS // SSD_GROUPS
SSD_STATE = 64
SSD_CHUNK = 128
SSD_CONV_DIM = BR_WIDTH + 2 * SSD_GROUPS * SSD_STATE

LRU_WIDTH = 512
LRU_BLOCKS = 8
LRU_BLOCK = LRU_WIDTH // LRU_BLOCKS
LRU_C = 8.0

IN_WIDTHS = (RW_PRE, BR_WIDTH, MLA_Q_RANK, MLA_KV_RANK + MLA_ROPE, BR_WIDTH, BR_WIDTH, SSD_CONV_DIM, 2 * SSD_HEADS, LRU_WIDTH, LRU_WIDTH, N_BRANCH * D_MODEL)
N_IN = RW_PRE + MLA_Q_RANK + MLA_KV_RANK + MLA_ROPE + 3 * BR_WIDTH + SSD_CONV_DIM + 2 * SSD_HEADS + 2 * LRU_WIDTH + N_BRANCH * D_MODEL

F32 = jnp.float32

kernel_name = 'hybrid_rwkv_mla_ssd_lru_diffusion_step'


def split_cols(z, widths):
    idx = []
    acc = 0
    for w in widths[:-1]:
        acc += w
        idx.append(acc)
    return jnp.split(z, idx, axis=-1)


def rmsnorm(x, g):
    x32 = x.astype(F32)
    y = x32 * lax.rsqrt(jnp.mean(x32 * x32, axis=-1, keepdims=True) + EPS)
    return (y * g).astype(x.dtype)


def centred_dwconv(x, w, bias):
    y = lax.conv_general_dilated(x, w[:, None, :].astype(x.dtype), window_strides=(1,),
                                 padding=[(CONV_K // 2, CONV_K - 1 - CONV_K // 2)],
                                 dimension_numbers=('NWC', 'WIO', 'NWC'),
                                 feature_group_count=x.shape[-1])
    return y + bias


def centred_shift(u):
    up = jnp.pad(u, ((0, 0), (1, 1), (0, 0)))
    return 0.5 * (up[:, :-2] + up[:, 2:])


def segsum(x):
    t = x.shape[-1]
    xe = jnp.broadcast_to(x[..., None], x.shape + (t,))
    xe = jnp.where(jnp.tril(jnp.ones((t, t), bool), -1), xe, 0.0)
    xs = jnp.cumsum(xe, axis=-2)
    return jnp.where(jnp.tril(jnp.ones((t, t), bool), 0), xs, -jnp.inf)


def axial_rope(x):
    n = x.shape[1]
    rows = n // GRID_W
    row = jnp.broadcast_to(jnp.arange(rows)[:, None], (rows, GRID_W)).reshape(n).astype(F32)
    col = jnp.broadcast_to(jnp.arange(GRID_W)[None, :], (rows, GRID_W)).reshape(n).astype(F32)
    half = MLA_ROPE // 2
    nf = half // 2
    inv = ROPE_THETA ** (-jnp.arange(nf, dtype=F32) / nf)
    out = []
    for pos, xa in ((row, x[..., :half]), (col, x[..., half:])):
        ang = pos[:, None] * inv[None, :]
        cos = jnp.cos(ang)[:, None, :]
        sin = jnp.sin(ang)[:, None, :]
        x1, x2 = xa[..., :nf], xa[..., nf:]
        out += [x1 * cos - x2 * sin, x1 * sin + x2 * cos]
    return jnp.concatenate(out, axis=-1).astype(x.dtype)


def rotate_latent(t):
    return jnp.concatenate([t[..., :MLA_NOPE], axial_rope(t[..., MLA_NOPE:])], axis=-1)


def block_attention(q, k, v):
    b, nq, h, dk = q.shape
    qb = jnp.moveaxis(q.reshape(b, nq // Q_BLOCK, Q_BLOCK, h, dk), 1, 0)
    scale = dk ** -0.5

    def one(qblk):
        s = jnp.einsum('bqhd,bkhd->bhqk', qblk, k).astype(F32) * scale
        p = jax.nn.softmax(s, axis=-1).astype(v.dtype)
        return jnp.einsum('bhqk,bkhd->bqhd', p, v)

    o = lax.map(one, qb)
    return jnp.moveaxis(o, 0, 1).reshape(b, nq, h, v.shape[-1])


def rwkv7_scan(r, decay, k, v, kk, a, s0, reverse):
    def step(s, inp):
        r_t, w_t, k_t, v_t, kk_t, a_t = inp
        s_kk = jnp.einsum('bhij,bhj->bhi', s, kk_t)
        s = (s * w_t[:, :, None, :] - s_kk[..., None] * (kk_t * a_t)[:, :, None, :]
             + v_t[..., None] * k_t[:, :, None, :])
        return s, jnp.einsum('bhij,bhj->bhi', s, r_t)
    xs = tuple(jnp.moveaxis(t, 1, 0) for t in (r, decay, k, v, kk, a))
    s_fin, out = lax.scan(step, s0.astype(F32), xs, reverse=reverse)
    return jnp.moveaxis(out, 0, 1), s_fin


def rwkv_branch(u, P, l, s0):
    b, n, _ = u.shape
    dtype = u.dtype
    u = (u + P['rw_mu'][l] * (centred_shift(u) - u)).astype(F32)
    r, k, v, w_lo, a_lo = split_cols(u, (BR_WIDTH, BR_WIDTH, BR_WIDTH, RW_DECAY_LORA, RW_A_LORA))
    heads = lambda t: t.reshape(b, n, RW_HEADS, RW_HEAD)
    r, k, v = heads(r), heads(k), heads(v)
    kk = k * P['rw_kk'][l].reshape(RW_HEADS, RW_HEAD)
    kk = kk * lax.rsqrt(jnp.sum(kk * kk, axis=-1, keepdims=True) + 1e-12)
    k_a = P['rw_ka'][l].reshape(RW_HEADS, RW_HEAD)
    r_k = P['rw_rk'][l]
    w_lo = jnp.tanh(w_lo)
    wkv = 0.0
    bonus = 0.0
    finals = []
    for d in range(2):
        w_raw = P['rw_w0'][l, d] + jnp.matmul(w_lo, P['rw_w2'][l, d])
        decay = heads(jnp.exp(-jnp.exp(-jax.nn.softplus(-w_raw) - 0.5)))
        a = heads(jax.nn.sigmoid(P['rw_a0'][l, d] + jnp.matmul(a_lo, P['rw_a2'][l, d])))
        k_d = k * (1.0 + (a - 1.0) * k_a)
        o_d, s_d = rwkv7_scan(r, decay, k_d, v, kk, a, s0[:, d], reverse=(d == 1))
        wkv = wkv + o_d
        bonus = bonus + jnp.sum(r * k_d * r_k, axis=-1, keepdims=True) * v
        finals.append(s_d)
    mu = jnp.mean(wkv, axis=-1, keepdims=True)
    var = jnp.mean(jnp.square(wkv - mu), axis=-1, keepdims=True)
    o = ((wkv - mu) * lax.rsqrt(var + RW_GN_EPS) * P['rw_ln_g'][l].reshape(RW_HEADS, RW_HEAD)
         + P['rw_ln_b'][l].reshape(RW_HEADS, RW_HEAD) + bonus)
    return o.reshape(b, n, BR_WIDTH).astype(dtype), jnp.stack(finals, axis=1)


def mla_keys_values(c_kv, k_rope, kv_up, kn_g):
    b, n, _ = c_kv.shape
    kv = jnp.matmul(c_kv, kv_up).reshape(b, n, MLA_HEADS, MLA_NOPE + MLA_V)
    k_nope, v = kv[..., :MLA_NOPE], kv[..., MLA_NOPE:]
    k = jnp.concatenate([k_nope, jnp.broadcast_to(k_rope[:, :, None, :], (b, n, MLA_HEADS, MLA_ROPE)).astype(k_nope.dtype)], axis=-1)
    return rmsnorm(k, kn_g), v


def mla_branch(q_lat, kv_lat, P, l, ctx_kv):
    b, n, _ = q_lat.shape
    q = jnp.matmul(rmsnorm(q_lat, P['mla_qa_g'][l]), P['mla_q_up'][l]).reshape(b, n, MLA_HEADS, MLA_QK)
    q = rmsnorm(q, P['mla_qn_g'][l])
    c_kv, k_rope = split_cols(kv_lat, (MLA_KV_RANK, MLA_ROPE))
    c_kv = rmsnorm(c_kv, P['mla_kva_g'][l])
    k, v = mla_keys_values(c_kv, k_rope, P['mla_kv_up'][l], P['mla_kn_g'][l])
    if ctx_kv is None:
        o = block_attention(q, k, v)
    else:
        q, k = rotate_latent(q), rotate_latent(k)
        k_ctx, v_ctx = mla_keys_values(ctx_kv[0], ctx_kv[1], P['mla_kv_up'][l], P['mla_kn_g'][l])
        k_all = jnp.concatenate([k_ctx.astype(k.dtype), k], axis=1)
        v_all = jnp.concatenate([v_ctx.astype(v.dtype), v], axis=1)
        o = block_attention(q, k_all, v_all)
    return o.reshape(b, n, MLA_HEADS * MLA_V), c_kv, k_rope


def ssd_chunked(x, dtA, bm, cm, h0):
    b, n, g, r, p = x.shape
    c, ln = n // SSD_CHUNK, SSD_CHUNK
    x = x.reshape(b, c, ln, g, r, p)
    bm = bm.reshape(b, c, ln, g, SSD_STATE)
    cm = cm.reshape(b, c, ln, g, SSD_STATE)
    a = jnp.moveaxis(dtA.reshape(b, c, ln, g, r), 2, -1)
    a_cs = jnp.cumsum(a, axis=-1)
    lmat = jnp.exp(segsum(a))
    y_diag = jnp.einsum('bclgn,bcsgn,bcgrls,bcsgrp->bclgrp', cm, bm, lmat, x)
    decay_states = jnp.exp(a_cs[..., -1:] - a_cs)
    states = jnp.einsum('bclgn,bcgrl,bclgrp->bcgrpn', bm, decay_states, x)
    states = jnp.concatenate([h0[:, None], states], axis=1)
    chunk_a = jnp.pad(jnp.moveaxis(a_cs[..., -1], 1, -1), ((0, 0), (0, 0), (0, 0), (1, 0)))
    decay_chunk = jnp.exp(segsum(chunk_a))
    new_states = jnp.einsum('bgrzc,bcgrpn->bzgrpn', decay_chunk, states)
    states, final = new_states[:, :-1], new_states[:, -1]
    y_off = jnp.einsum('bclgn,bcgrpn,bcgrl->bclgrp', cm, states, jnp.exp(a_cs))
    return (y_diag + y_off).reshape(b, n, g, r, p), final


def ssd_branch(zg, xbc, dt_raw, P, l, h0):
    b, n, _ = xbc.shape
    dtype = xbc.dtype
    xbc = jax.nn.silu(centred_dwconv(xbc, P['ssd_conv_w'][l], P['ssd_conv_b'][l])).astype(F32)
    xs, bm, cm = split_cols(xbc, (BR_WIDTH, SSD_GROUPS * SSD_STATE, SSD_GROUPS * SSD_STATE))
    xs = xs.reshape(b, n, SSD_GROUPS, SSD_HPG, SSD_HEAD_DIM)
    bm = bm.reshape(b, n, SSD_GROUPS, SSD_STATE)
    cm = cm.reshape(b, n, SSD_GROUPS, SSD_STATE)
    dt_raw = dt_raw.astype(F32).reshape(b, n, 2, SSD_HEADS)
    y = xs * P['ssd_d'][l].reshape(SSD_GROUPS, SSD_HPG)[..., None]
    finals = []
    for d in range(2):
        dt = jax.nn.softplus(dt_raw[:, :, d] + P['ssd_dt_bias'][l, d]).reshape(b, n, SSD_GROUPS, SSD_HPG)
        dtA = dt * (-jnp.exp(P['ssd_a_log'][l, d])).reshape(SSD_GROUPS, SSD_HPG)
        h0_d = h0[:, d].astype(F32).reshape(b, SSD_GROUPS, SSD_HPG, SSD_HEAD_DIM, SSD_STATE)
        args = (xs * dt[..., None], dtA, bm, cm)
        if d == 1:
            args = tuple(jnp.flip(t, axis=1) for t in args)
        y_d, h_d = ssd_chunked(*args, h0_d)
        if d == 1:
            y_d = jnp.flip(y_d, axis=1)
        y = y + y_d
        finals.append(h_d.reshape(b, SSD_HEADS, SSD_HEAD_DIM, SSD_STATE))
    y = y.reshape(b, n, BR_WIDTH) * jax.nn.silu(zg.astype(F32))
    return rmsnorm(y, P['ssd_norm_g'][l]).astype(dtype), jnp.stack(finals, axis=1)


def linear_scan(a, u, h0, reverse):
    def combine(e1, e2):
        a1, b1 = e1
        a2, b2 = e2
        return a1 * a2, a2 * b1 + b2
    a_cum, b_cum = lax.associative_scan(combine, (a, u), axis=1, reverse=reverse)
    h = a_cum * h0[:, None, :] + b_cum
    final = h[:, 0] if reverse else h[:, -1]
    return h, final


def lru_branch(xl, P, l, h0):
    b, n, _ = xl.shape
    dtype = xl.dtype
    xc = centred_dwconv(xl, P['lru_conv_w'][l], P['lru_conv_b'][l]).astype(F32)
    xb = xc.reshape(b, n, LRU_BLOCKS, LRU_BLOCK)
    y = 0.0
    finals = []
    for d in range(2):
        g_r = jax.nn.sigmoid(jnp.einsum('bnki,kij->bnkj', xb, P['lru_wa'][l, d]).reshape(b, n, LRU_WIDTH) + P['lru_ba'][l, d])
        g_i = jax.nn.sigmoid(jnp.einsum('bnki,kij->bnkj', xb, P['lru_wx'][l, d]).reshape(b, n, LRU_WIDTH) + P['lru_bx'][l, d])
        log_a = -LRU_C * g_r * jax.nn.softplus(-P['lru_lambda'][l, d])
        a = jnp.exp(log_a)
        u = jnp.sqrt(-jnp.expm1(2.0 * log_a)) * (g_i * xc)
        h, h_fin = linear_scan(a, u, h0[:, d].astype(F32), reverse=(d == 1))
        y = y + h
        finals.append(h_fin)
    return y.astype(dtype), jnp.stack(finals, axis=1)


def trunk_layer(x, cond, P, l, ctx):
    b, n, _ = x.shape
    mod = jnp.matmul(jax.nn.silu(cond), P['ada_w'][l]) + P['ada_b'][l]
    shift, scale, gate = jnp.split(mod[:, None, :], 3, axis=-1)
    h = rmsnorm(x, P['norm_g'][l]) * (1 + scale) + shift
    z = jnp.matmul(h, P['w_in'][l])
    (rw_pre, rw_gate, mla_q, mla_kv, mla_gate, ssd_gate, ssd_xbc, ssd_dt,
     lru_x, lru_gate, merge_logits) = split_cols(z, IN_WIDTHS)
    if ctx is None:
        ctx_kv = None
        rw0 = jnp.zeros((b, 2, RW_HEADS, RW_HEAD, RW_HEAD), F32)
        ssd0 = jnp.zeros((b, 2, SSD_HEADS, SSD_HEAD_DIM, SSD_STATE), F32)
        lru0 = jnp.zeros((b, 2, LRU_WIDTH), F32)
    else:
        ckv0, krope0, rw0, ssd0, lru0 = ctx
        ctx_kv = (ckv0, krope0)
    o_rw, rw_fin = rwkv_branch(rw_pre, P, l, rw0)
    o_mla, c_kv, k_rope = mla_branch(mla_q, mla_kv, P, l, ctx_kv)
    o_ssd, ssd_fin = ssd_branch(ssd_gate, ssd_xbc, ssd_dt, P, l, ssd0)
    o_lru, lru_fin = lru_branch(lru_x, P, l, lru0)
    branches = jnp.stack([o_rw * jax.nn.silu(rw_gate), o_mla * jax.nn.silu(mla_gate),
                          o_ssd, o_lru * jax.nn.silu(lru_gate)], axis=2)
    proj = jnp.einsum('bnmc,mcd->bnmd', branches, P['w_branch'][l])
    gates = jax.nn.sigmoid(merge_logits.reshape(b, n, N_BRANCH, D_MODEL))
    merged = jnp.sum(gates * proj, axis=2)
    y = x + gate * jnp.matmul(merged, P['w_out'][l])
    return y, (c_kv, k_rope, rw_fin, ssd_fin, lru_fin)


def setup_inputs(seed: int = 0) -> dict:
    key = jax.random.key(seed)
    ks = iter(jax.random.split(key, 64))
    L = DEPTH

    def nrm(shape, scale=1.0):
        return scale * jax.random.normal(next(ks), shape, F32)

    def gain(shape):
        return 1.0 + 0.05 * jax.random.normal(next(ks), shape, F32)

    def unif(shape, lo, hi):
        return jax.random.uniform(next(ks), shape, F32, lo, hi)

    dt0 = jnp.exp(unif((L, 2, SSD_HEADS), np.log(1e-3).item(), np.log(1e-1).item()))
    ssd_dt_bias = dt0 + jnp.log(-jnp.expm1(-dt0))
    a_lru = unif((L, 2, LRU_WIDTH), 0.9, 0.999) ** (1.0 / LRU_C)
    lru_lambda = jnp.log(a_lru) - jnp.log1p(-a_lru)
    return {
        'x_prompt': nrm((BATCH, SEQ, D_MODEL)),
        'x_sample': nrm((DEC_BATCH, DEC_SEQ, D_MODEL)),
        'cache_mla_ckv': nrm((DEC_BATCH, DEPTH, PAST_LEN, MLA_KV_RANK)),
        'cache_mla_krope': nrm((DEC_BATCH, DEPTH, PAST_LEN, MLA_ROPE)),
        'state_rwkv': nrm((DEC_BATCH, DEPTH, 2, RW_HEADS, RW_HEAD, RW_HEAD), 0.1),
        'state_ssd': nrm((DEC_BATCH, DEPTH, 2, SSD_HEADS, SSD_HEAD_DIM, SSD_STATE), 0.1),
        'state_lru': nrm((DEC_BATCH, DEPTH, 2, LRU_WIDTH), 0.5),
        'c': nrm((DEC_BATCH, D_MODEL)),
        'c_ctx': nrm((D_MODEL,)),
        'ada_w': nrm((L, D_MODEL, 3 * D_MODEL), D_MODEL ** -0.5),
        'ada_b': nrm((L, 3 * D_MODEL), 0.02),
        'norm_g': gain((L, D_MODEL)),
        'w_in': nrm((L, D_MODEL, N_IN), D_MODEL ** -0.5),
        'rw_mu': unif((L, RW_PRE), 0.0, 1.0),
        'rw_w0': unif((L, 2, BR_WIDTH), -5.0, 1.0),
        'rw_w2': nrm((L, 2, RW_DECAY_LORA, BR_WIDTH), 0.5 * RW_DECAY_LORA ** -0.5),
        'rw_a0': nrm((L, 2, BR_WIDTH), 0.5),
        'rw_a2': nrm((L, 2, RW_A_LORA, BR_WIDTH), 0.5 * RW_A_LORA ** -0.5),
        'rw_kk': gain((L, BR_WIDTH)),
        'rw_ka': gain((L, BR_WIDTH)),
        'rw_rk': nrm((L, RW_HEADS, RW_HEAD), 0.1),
        'rw_ln_g': gain((L, BR_WIDTH)),
        'rw_ln_b': nrm((L, BR_WIDTH), 0.02),
        'mla_qa_g': gain((L, MLA_Q_RANK)),
        'mla_q_up': nrm((L, MLA_Q_RANK, MLA_HEADS * MLA_QK), MLA_Q_RANK ** -0.5),
        'mla_kva_g': gain((L, MLA_KV_RANK)),
        'mla_kv_up': nrm((L, MLA_KV_RANK, MLA_HEADS * (MLA_NOPE + MLA_V)), MLA_KV_RANK ** -0.5),
        'mla_qn_g': gain((L, MLA_QK)),
        'mla_kn_g': gain((L, MLA_QK)),
        'ssd_conv_w': nrm((L, CONV_K, SSD_CONV_DIM), CONV_K ** -0.5),
        'ssd_conv_b': nrm((L, SSD_CONV_DIM), 0.02),
        'ssd_dt_bias': ssd_dt_bias,
        'ssd_a_log': jnp.log(unif((L, 2, SSD_HEADS), 1.0, 16.0)),
        'ssd_d': gain((L, SSD_HEADS)),
        'ssd_norm_g': gain((L, BR_WIDTH)),
        'lru_conv_w': nrm((L, CONV_K, LRU_WIDTH), CONV_K ** -0.5),
        'lru_conv_b': nrm((L, LRU_WIDTH), 0.02),
        'lru_wa': nrm((L, 2, LRU_BLOCKS, LRU_BLOCK, LRU_BLOCK), LRU_BLOCK ** -0.5),
        'lru_ba': nrm((L, 2, LRU_WIDTH), 0.02),
        'lru_wx': nrm((L, 2, LRU_BLOCKS, LRU_BLOCK, LRU_BLOCK), LRU_BLOCK ** -0.5),
        'lru_bx': nrm((L, 2, LRU_WIDTH), 0.02),
        'lru_lambda': lru_lambda,
        'w_branch': nrm((L, N_BRANCH, BR_WIDTH, D_MODEL), BR_WIDTH ** -0.5),
        'w_out': nrm((L, D_MODEL, D_MODEL), D_MODEL ** -0.5),
    }


def reference(x_prompt, x_sample, cache_mla_ckv, cache_mla_krope, state_rwkv, state_ssd, state_lru,
              c, c_ctx, ada_w, ada_b, norm_g, w_in,
              rw_mu, rw_w0, rw_w2, rw_a0, rw_a2, rw_kk, rw_ka, rw_rk, rw_ln_g, rw_ln_b,
              mla_qa_g, mla_q_up, mla_kva_g, mla_kv_up, mla_qn_g, mla_kn_g,
              ssd_conv_w, ssd_conv_b, ssd_dt_bias, ssd_a_log, ssd_d, ssd_norm_g,
              lru_conv_w, lru_conv_b, lru_wa, lru_ba, lru_wx, lru_bx, lru_lambda,
              w_branch, w_out):
    P = dict(ada_w=ada_w, ada_b=ada_b, norm_g=norm_g, w_in=w_in,
             rw_mu=rw_mu, rw_w0=rw_w0, rw_w2=rw_w2, rw_a0=rw_a0, rw_a2=rw_a2, rw_kk=rw_kk,
             rw_ka=rw_ka, rw_rk=rw_rk, rw_ln_g=rw_ln_g, rw_ln_b=rw_ln_b,
             mla_qa_g=mla_qa_g, mla_q_up=mla_q_up, mla_kva_g=mla_kva_g, mla_kv_up=mla_kv_up,
             mla_qn_g=mla_qn_g, mla_kn_g=mla_kn_g,
             ssd_conv_w=ssd_conv_w, ssd_conv_b=ssd_conv_b, ssd_dt_bias=ssd_dt_bias,
             ssd_a_log=ssd_a_log, ssd_d=ssd_d, ssd_norm_g=ssd_norm_g,
             lru_conv_w=lru_conv_w, lru_conv_b=lru_conv_b, lru_wa=lru_wa, lru_ba=lru_ba,
             lru_wx=lru_wx, lru_bx=lru_bx, lru_lambda=lru_lambda,
             w_branch=w_branch, w_out=w_out)

    y_prompt = x_prompt
    cond_ctx = c_ctx[None, :]
    ctx_states = []
    for l in range(DEPTH):
        y_prompt, st = trunk_layer(y_prompt, cond_ctx, P, l, None)
        ctx_states.append(st)
    new_mla_ckv = jnp.stack([s[0] for s in ctx_states], axis=1)
    new_mla_krope = jnp.stack([s[1] for s in ctx_states], axis=1)
    new_rwkv = jnp.stack([s[2] for s in ctx_states], axis=1)
    new_ssd = jnp.stack([s[3] for s in ctx_states], axis=1)
    new_lru = jnp.stack([s[4] for s in ctx_states], axis=1)

    y_sample = x_sample
    for l in range(DEPTH):
        ctx = (cache_mla_ckv[:, l], cache_mla_krope[:, l], state_rwkv[:, l], state_ssd[:, l], state_lru[:, l])
        y_sample, _ = trunk_layer(y_sample, c, P, l, ctx)

    return (y_prompt, y_sample, new_mla_ckv, new_mla_krope, new_rwkv, new_ssd, new_lru)
```

```python
import functools

import jax
import jax.numpy as jnp
import numpy as np
from jax import lax
from jax.experimental import pallas as pl
from jax.experimental.pallas import tpu as pltpu

F32 = jnp.float32
BF16 = jnp.bfloat16

D_MODEL = 1024
DEPTH = 2
GRID_W = 64
N_BRANCH = 4
BR_WIDTH = 512
EPS = 1e-6
Q_BLOCK = 128
CONV_K = 4

RW_HEADS = 8
RW_HEAD = 64
RW_LORA = 64
RW_GN_EPS = 64e-5
RW_PRE = 3 * BR_WIDTH + 2 * RW_LORA

MLA_HEADS = 8
MLA_NOPE = 64
MLA_ROPE = 32
MLA_QK = MLA_NOPE + MLA_ROPE
MLA_V = 64
MLA_Q_RANK = 384
MLA_KV_RANK = 256
ROPE_THETA = 10000.0

SSD_HEADS = 8
SSD_HEAD_DIM = 64
SSD_GROUPS = 2
SSD_HPG = SSD_HEADS // SSD_GROUPS
SSD_STATE = 64
SSD_CHUNK = 128
SSD_CONV_DIM = BR_WIDTH + 2 * SSD_GROUPS * SSD_STATE

LRU_WIDTH = 512
LRU_BLOCKS = 8
LRU_BLOCK = LRU_WIDTH // LRU_BLOCKS
LRU_C = 8.0

IN_WIDTHS = (RW_PRE, BR_WIDTH, MLA_Q_RANK, MLA_KV_RANK + MLA_ROPE, BR_WIDTH, BR_WIDTH, SSD_CONV_DIM,
             2 * SSD_HEADS, LRU_WIDTH, LRU_WIDTH, N_BRANCH * D_MODEL)

LANE = 128
VMEM_LIMIT = 48 * 1024 * 1024


def _cparams(*sem):
    return pltpu.CompilerParams(dimension_semantics=sem, vmem_limit_bytes=VMEM_LIMIT)


def _prenorm_body(x_ref, g_ref, sc_ref, sh_ref, h_ref):
    x = x_ref[...]
    y = x * lax.rsqrt(jnp.mean(x * x, axis=-1, keepdims=True) + EPS)
    h = (y * g_ref[...]) * (1.0 + sc_ref[0]) + sh_ref[0]
    h_ref[...] = h.astype(h_ref.dtype)


def prenorm(x2d, g, scale, shift, n_seq_tokens, tm):
    t, d = x2d.shape
    bm = scale.shape[0]
    per = n_seq_tokens // tm
    mod_idx = (lambda i: (i // per, 0, 0)) if bm > 1 else (lambda i: (0, 0, 0))
    return pl.pallas_call(
        _prenorm_body,
        grid=(t // tm,),
        in_specs=[pl.BlockSpec((tm, d), lambda i: (i, 0)),
                  pl.BlockSpec((1, d), lambda i: (0, 0)),
                  pl.BlockSpec((1, 1, d), mod_idx),
                  pl.BlockSpec((1, 1, d), mod_idx)],
        out_specs=pl.BlockSpec((tm, d), lambda i: (i, 0)),
        out_shape=jax.ShapeDtypeStruct((t, d), BF16),
        compiler_params=_cparams("parallel"),
        name="prenorm",
    )(x2d, g, scale, shift)


def _mm_body(a_ref, w_ref, o_ref):
    o_ref[...] = jnp.dot(a_ref[...], w_ref[...], preferred_element_type=F32).astype(o_ref.dtype)


def matmul(a, w, tm, tn, out_dtype):
    t, k = a.shape
    n = w.shape[1]
    return pl.pallas_call(
        _mm_body,
        grid=(n // tn, t // tm),
        in_specs=[pl.BlockSpec((tm, k), lambda j, i: (i, 0)),
                  pl.BlockSpec((k, tn), lambda j, i: (0, j))],
        out_specs=pl.BlockSpec((tm, tn), lambda j, i: (i, j)),
        out_shape=jax.ShapeDtypeStruct((t, n), out_dtype),
        compiler_params=_cparams("parallel", "parallel"),
        name="matmul",
    )(a, w)


def _merge_body(x_ref, b0_ref, b1_ref, b2_ref, b3_ref, lg_ref, wb_ref, wo_ref, gate_ref, y_ref):
    merged = None
    for m, b_ref in enumerate((b0_ref, b1_ref, b2_ref, b3_ref)):
        proj = jnp.dot(b_ref[...].astype(BF16), wb_ref[m], preferred_element_type=F32)
        g = jax.nn.sigmoid(lg_ref[:, m * D_MODEL:(m + 1) * D_MODEL].astype(F32))
        merged = g * proj if merged is None else merged + g * proj
    out = jnp.dot(merged.astype(BF16), wo_ref[...], preferred_element_type=F32)
    y_ref[...] = x_ref[...] + gate_ref[0] * out


def merge(x2d, branches, logits, wb, wo, gate, n_seq_tokens, tm):
    t, d = x2d.shape
    bm = gate.shape[0]
    per = n_seq_tokens // tm
    mod_idx = (lambda i: (i // per, 0, 0)) if bm > 1 else (lambda i: (0, 0, 0))
    row = lambda i: (i, 0)
    return pl.pallas_call(
        _merge_body,
        grid=(t // tm,),
        in_specs=[pl.BlockSpec((tm, d), row)]
        + [pl.BlockSpec((tm, BR_WIDTH), row)] * 4
        + [pl.BlockSpec((tm, N_BRANCH * d), row),
           pl.BlockSpec((N_BRANCH, BR_WIDTH, d), lambda i: (0, 0, 0)),
           pl.BlockSpec((d, d), lambda i: (0, 0)),
           pl.BlockSpec((1, 1, d), mod_idx)],
        out_specs=pl.BlockSpec((tm, d), row),
        out_shape=jax.ShapeDtypeStruct((t, d), F32),
        compiler_params=_cparams("parallel"),
        name="merge",
    )(x2d, *branches, logits, wb, wo, gate)


def split_cols(z, widths):
    idx = []
    acc = 0
    for w in widths[:-1]:
        acc += w
        idx.append(acc)
    return jnp.split(z, idx, axis=-1)


def rmsnorm(x, g):
    y = x * lax.rsqrt(jnp.mean(x * x, axis=-1, keepdims=True) + EPS)
    return y * g


def centred_dwconv(x, w, bias):
    y = lax.conv_general_dilated(x, w[:, None, :], window_strides=(1,),
                                 padding=[(CONV_K // 2, CONV_K - 1 - CONV_K // 2)],
                                 dimension_numbers=('NWC', 'WIO', 'NWC'),
                                 feature_group_count=x.shape[-1])
    return y + bias


def centred_shift(u):
    up = jnp.pad(u, ((0, 0), (1, 1), (0, 0)))
    return 0.5 * (up[:, :-2] + up[:, 2:])


def segsum(x):
    t = x.shape[-1]
    xe = jnp.broadcast_to(x[..., None], x.shape + (t,))
    xe = jnp.where(jnp.tril(jnp.ones((t, t), bool), -1), xe, 0.0)
    xs = jnp.cumsum(xe, axis=-2)
    return jnp.where(jnp.tril(jnp.ones((t, t), bool), 0), xs, -jnp.inf)


def axial_rope(x):
    n = x.shape[1]
    rows = n // GRID_W
    row = jnp.broadcast_to(jnp.arange(rows)[:, None], (rows, GRID_W)).reshape(n).astype(F32)
    col = jnp.broadcast_to(jnp.arange(GRID_W)[None, :], (rows, GRID_W)).reshape(n).astype(F32)
    half = MLA_ROPE // 2
    nf = half // 2
    inv = ROPE_THETA ** (-jnp.arange(nf, dtype=F32) / nf)
    out = []
    for pos, xa in ((row, x[..., :half]), (col, x[..., half:])):
        ang = pos[:, None] * inv[None, :]
        cos = jnp.cos(ang)[:, None, :]
        sin = jnp.sin(ang)[:, None, :]
        x1, x2 = xa[..., :nf], xa[..., nf:]
        out += [x1 * cos - x2 * sin, x1 * sin + x2 * cos]
    return jnp.concatenate(out, axis=-1)


def rotate_latent(t):
    return jnp.concatenate([t[..., :MLA_NOPE], axial_rope(t[..., MLA_NOPE:])], axis=-1)


def block_attention(q, k, v):
    b, nq, h, dk = q.shape
    qb = jnp.moveaxis(q.reshape(b, nq // Q_BLOCK, Q_BLOCK, h, dk), 1, 0)
    scale = dk ** -0.5

    def one(qblk):
        s = jnp.einsum('bqhd,bkhd->bhqk', qblk, k) * scale
        p = jax.nn.softmax(s, axis=-1)
        return jnp.einsum('bhqk,bkhd->bqhd', p, v)

    o = lax.map(one, qb)
    return jnp.moveaxis(o, 0, 1).reshape(b, nq, h, v.shape[-1])


def rwkv7_scan(r, decay, k, v, kk, a, s0, reverse):
    def step(s, inp):
        r_t, w_t, k_t, v_t, kk_t, a_t = inp
        s_kk = jnp.einsum('bhij,bhj->bhi', s, kk_t)
        s = (s * w_t[:, :, None, :] - s_kk[..., None] * (kk_t * a_t)[:, :, None, :]
             + v_t[..., None] * k_t[:, :, None, :])
        return s, jnp.einsum('bhij,bhj->bhi', s, r_t)
    xs = tuple(jnp.moveaxis(t, 1, 0) for t in (r, decay, k, v, kk, a))
    s_fin, out = lax.scan(step, s0, xs, reverse=reverse)
    return jnp.moveaxis(out, 0, 1), s_fin


def rwkv_branch(u, P, l, s0):
    b, n, _ = u.shape
    u = u + P['rw_mu'][l] * (centred_shift(u) - u)
    r, k, v, w_lo, a_lo = split_cols(u, (BR_WIDTH, BR_WIDTH, BR_WIDTH, RW_LORA, RW_LORA))
    heads = lambda t: t.reshape(b, n, RW_HEADS, RW_HEAD)
    r, k, v = heads(r), heads(k), heads(v)
    kk = k * P['rw_kk'][l].reshape(RW_HEADS, RW_HEAD)
    kk = kk * lax.rsqrt(jnp.sum(kk * kk, axis=-1, keepdims=True) + 1e-12)
    k_a = P['rw_ka'][l].reshape(RW_HEADS, RW_HEAD)
    r_k = P['rw_rk'][l]
    w_lo = jnp.tanh(w_lo)
    wkv = 0.0
    bonus = 0.0
    finals = []
    for d in range(2):
        w_raw = P['rw_w0'][l, d] + jnp.matmul(w_lo, P['rw_w2'][l, d])
        decay = heads(jnp.exp(-jnp.exp(-jax.nn.softplus(-w_raw) - 0.5)))
        a = heads(jax.nn.sigmoid(P['rw_a0'][l, d] + jnp.matmul(a_lo, P['rw_a2'][l, d])))
        k_d = k * (1.0 + (a - 1.0) * k_a)
        o_d, s_d = rwkv7_scan(r, decay, k_d, v, kk, a, s0[:, d], reverse=(d == 1))
        wkv = wkv + o_d
        bonus = bonus + jnp.sum(r * k_d * r_k, axis=-1, keepdims=True) * v
        finals.append(s_d)
    mu = jnp.mean(wkv, axis=-1, keepdims=True)
    var = jnp.mean(jnp.square(wkv - mu), axis=-1, keepdims=True)
    o = ((wkv - mu) * lax.rsqrt(var + RW_GN_EPS) * P['rw_ln_g'][l].reshape(RW_HEADS, RW_HEAD)
         + P['rw_ln_b'][l].reshape(RW_HEADS, RW_HEAD) + bonus)
    return o.reshape(b, n, BR_WIDTH), jnp.stack(finals, axis=1)


def mla_keys_values(c_kv, k_rope, kv_up, kn_g):
    b, n, _ = c_kv.shape
    kv = jnp.matmul(c_kv, kv_up).reshape(b, n, MLA_HEADS, MLA_NOPE + MLA_V)
    k_nope, v = kv[..., :MLA_NOPE], kv[..., MLA_NOPE:]
    k = jnp.concatenate([k_nope, jnp.broadcast_to(k_rope[:, :, None, :], (b, n, MLA_HEADS, MLA_ROPE))], axis=-1)
    return rmsnorm(k, kn_g), v


def mla_branch(q_lat, kv_lat, P, l, ctx_kv):
    b, n, _ = q_lat.shape
    q = jnp.matmul(rmsnorm(q_lat, P['mla_qa_g'][l]), P['mla_q_up'][l]).reshape(b, n, MLA_HEADS, MLA_QK)
    q = rmsnorm(q, P['mla_qn_g'][l])
    c_kv, k_rope = split_cols(kv_lat, (MLA_KV_RANK, MLA_ROPE))
    c_kv = rmsnorm(c_kv, P['mla_kva_g'][l])
    k, v = mla_keys_values(c_kv, k_rope, P['mla_kv_up'][l], P['mla_kn_g'][l])
    if ctx_kv is None:
        o = block_attention(q, k, v)
    else:
        q, k = rotate_latent(q), rotate_latent(k)
        k_ctx, v_ctx = mla_keys_values(ctx_kv[0], ctx_kv[1], P['mla_kv_up'][l], P['mla_kn_g'][l])
        k_all = jnp.concatenate([k_ctx, k], axis=1)
        v_all = jnp.concatenate([v_ctx, v], axis=1)
        o = block_attention(q, k_all, v_all)
    return o.reshape(b, n, MLA_HEADS * MLA_V), c_kv, k_rope


def ssd_chunked(x, dtA, bm, cm, h0):
    b, n, g, r, p = x.shape
    c, ln = n // SSD_CHUNK, SSD_CHUNK
    x = x.reshape(b, c, ln, g, r, p)
    bm = bm.reshape(b, c, ln, g, SSD_STATE)
    cm = cm.reshape(b, c, ln, g, SSD_STATE)
    a = jnp.moveaxis(dtA.reshape(b, c, ln, g, r), 2, -1)
    a_cs = jnp.cumsum(a, axis=-1)
    lmat = jnp.exp(segsum(a))
    y_diag = jnp.einsum('bclgn,bcsgn,bcgrls,bcsgrp->bclgrp', cm, bm, lmat, x)
    decay_states = jnp.exp(a_cs[..., -1:] - a_cs)
    states = jnp.einsum('bclgn,bcgrl,bclgrp->bcgrpn', bm, decay_states, x)
    states = jnp.concatenate([h0[:, None], states], axis=1)
    chunk_a = jnp.pad(jnp.moveaxis(a_cs[..., -1], 1, -1), ((0, 0), (0, 0), (0, 0), (1, 0)))
    decay_chunk = jnp.exp(segsum(chunk_a))
    new_states = jnp.einsum('bgrzc,bcgrpn->bzgrpn', decay_chunk, states)
    states, final = new_states[:, :-1], new_states[:, -1]
    y_off = jnp.einsum('bclgn,bcgrpn,bcgrl->bclgrp', cm, states, jnp.exp(a_cs))
    return (y_diag + y_off).reshape(b, n, g, r, p), final


def ssd_branch(zg, xbc, dt_raw, P, l, h0):
    b, n, _ = xbc.shape
    xbc = jax.nn.silu(centred_dwconv(xbc, P['ssd_conv_w'][l], P['ssd_conv_b'][l]))
    xs, bm, cm = split_cols(xbc, (BR_WIDTH, SSD_GROUPS * SSD_STATE, SSD_GROUPS * SSD_STATE))
    xs = xs.reshape(b, n, SSD_GROUPS, SSD_HPG, SSD_HEAD_DIM)
    bm = bm.reshape(b, n, SSD_GROUPS, SSD_STATE)
    cm = cm.reshape(b, n, SSD_GROUPS, SSD_STATE)
    dt_raw = dt_raw.reshape(b, n, 2, SSD_HEADS)
    y = xs * P['ssd_d'][l].reshape(SSD_GROUPS, SSD_HPG)[..., None]
    finals = []
    for d in range(2):
        dt = jax.nn.softplus(dt_raw[:, :, d] + P['ssd_dt_bias'][l, d]).reshape(b, n, SSD_GROUPS, SSD_HPG)
        dtA = dt * (-jnp.exp(P['ssd_a_log'][l, d])).reshape(SSD_GROUPS, SSD_HPG)
        h0_d = h0[:, d].reshape(b, SSD_GROUPS, SSD_HPG, SSD_HEAD_DIM, SSD_STATE)
        args = (xs * dt[..., None], dtA, bm, cm)
        if d == 1:
            args = tuple(jnp.flip(t, axis=1) for t in args)
        y_d, h_d = ssd_chunked(*args, h0_d)
        if d == 1:
            y_d = jnp.flip(y_d, axis=1)
        y = y + y_d
        finals.append(h_d.reshape(b, SSD_HEADS, SSD_HEAD_DIM, SSD_STATE))
    y = y.reshape(b, n, BR_WIDTH) * jax.nn.silu(zg)
    return rmsnorm(y, P['ssd_norm_g'][l]), jnp.stack(finals, axis=1)


def linear_scan(a, u, h0, reverse):
    def combine(e1, e2):
        a1, b1 = e1
        a2, b2 = e2
        return a1 * a2, a2 * b1 + b2
    a_cum, b_cum = lax.associative_scan(combine, (a, u), axis=1, reverse=reverse)
    h = a_cum * h0[:, None, :] + b_cum
    final = h[:, 0] if reverse else h[:, -1]
    return h, final


def lru_branch(xl, P, l, h0):
    b, n, _ = xl.shape
    xc = centred_dwconv(xl, P['lru_conv_w'][l], P['lru_conv_b'][l])
    xb = xc.reshape(b, n, LRU_BLOCKS, LRU_BLOCK)
    y = 0.0
    finals = []
    for d in range(2):
        g_r = jax.nn.sigmoid(jnp.einsum('bnki,kij->bnkj', xb, P['lru_wa'][l, d]).reshape(b, n, LRU_WIDTH) + P['lru_ba'][l, d])
        g_i = jax.nn.sigmoid(jnp.einsum('bnki,kij->bnkj', xb, P['lru_wx'][l, d]).reshape(b, n, LRU_WIDTH) + P['lru_bx'][l, d])
        log_a = -LRU_C * g_r * jax.nn.softplus(-P['lru_lambda'][l, d])
        a = jnp.exp(log_a)
        u = jnp.sqrt(-jnp.expm1(2.0 * log_a)) * (g_i * xc)
        h, h_fin = linear_scan(a, u, h0[:, d], reverse=(d == 1))
        y = y + h
        finals.append(h_fin)
    return y, jnp.stack(finals, axis=1)


def trunk_layer(x, cond, P, l, ctx):
    b, n, d = x.shape
    t = b * n
    tm = min(n, 512)
    mod = jnp.matmul(jax.nn.silu(cond), P['ada_w'][l]) + P['ada_b'][l]
    shift, scale, gate = jnp.split(mod[:, None, :], 3, axis=-1)
    x2d = x.reshape(t, d)
    h = prenorm(x2d, P['norm_g'][l][None, :], scale, shift, n, tm)
    z = matmul(h, P['w_in_bf'][l], tm, P['w_in_tn'], F32)[:, :sum(IN_WIDTHS)].reshape(b, n, -1)
    (rw_pre, rw_gate, mla_q, mla_kv, mla_gate, ssd_gate, ssd_xbc, ssd_dt,
     lru_x, lru_gate, merge_logits) = split_cols(z, IN_WIDTHS)
    if ctx is None:
        ctx_kv = None
        rw0 = jnp.zeros((b, 2, RW_HEADS, RW_HEAD, RW_HEAD), F32)
        ssd0 = jnp.zeros((b, 2, SSD_HEADS, SSD_HEAD_DIM, SSD_STATE), F32)
        lru0 = jnp.zeros((b, 2, LRU_WIDTH), F32)
    else:
        ckv0, krope0, rw0, ssd0, lru0 = ctx
        ctx_kv = (ckv0, krope0)
    o_rw, rw_fin = rwkv_branch(rw_pre, P, l, rw0)
    o_mla, c_kv, k_rope = mla_branch(mla_q, mla_kv, P, l, ctx_kv)
    o_ssd, ssd_fin = ssd_branch(ssd_gate, ssd_xbc, ssd_dt, P, l, ssd0)
    o_lru, lru_fin = lru_branch(lru_x, P, l, lru0)
    branches = [(o_rw * jax.nn.silu(rw_gate)).reshape(t, -1), (o_mla * jax.nn.silu(mla_gate)).reshape(t, -1),
                o_ssd.reshape(t, -1), (o_lru * jax.nn.silu(lru_gate)).reshape(t, -1)]
    y = merge(x2d, branches, merge_logits.reshape(t, -1), P['w_branch_bf'][l], P['w_out_bf'][l], gate, n, tm)
    return y.reshape(b, n, d), (c_kv, k_rope, rw_fin, ssd_fin, lru_fin)


def kernel(x_prompt, x_sample, cache_mla_ckv, cache_mla_krope, state_rwkv, state_ssd, state_lru, c, c_ctx, ada_w, ada_b, norm_g, w_in, rw_mu, rw_w0, rw_w2, rw_a0, rw_a2, rw_kk, rw_ka, rw_rk, rw_ln_g, rw_ln_b, mla_qa_g, mla_q_up, mla_kva_g, mla_kv_up, mla_qn_g, mla_kn_g, ssd_conv_w, ssd_conv_b, ssd_dt_bias, ssd_a_log, ssd_d, ssd_norm_g, lru_conv_w, lru_conv_b, lru_wa, lru_ba, lru_wx, lru_bx, lru_lambda, w_branch, w_out):
    n_in = sum(IN_WIDTHS)
    tn = 1280
    n_pad = -(-n_in // tn) * tn
    P = dict(ada_w=ada_w, ada_b=ada_b, norm_g=norm_g,
             w_in_bf=jnp.pad(w_in, ((0, 0), (0, 0), (0, n_pad - n_in))).astype(BF16), w_in_tn=tn,
             rw_mu=rw_mu, rw_w0=rw_w0, rw_w2=rw_w2, rw_a0=rw_a0, rw_a2=rw_a2, rw_kk=rw_kk,
             rw_ka=rw_ka, rw_rk=rw_rk, rw_ln_g=rw_ln_g, rw_ln_b=rw_ln_b,
             mla_qa_g=mla_qa_g, mla_q_up=mla_q_up, mla_kva_g=mla_kva_g, mla_kv_up=mla_kv_up,
             mla_qn_g=mla_qn_g, mla_kn_g=mla_kn_g,
             ssd_conv_w=ssd_conv_w, ssd_conv_b=ssd_conv_b, ssd_dt_bias=ssd_dt_bias,
             ssd_a_log=ssd_a_log, ssd_d=ssd_d, ssd_norm_g=ssd_norm_g,
             lru_conv_w=lru_conv_w, lru_conv_b=lru_conv_b, lru_wa=lru_wa, lru_ba=lru_ba,
             lru_wx=lru_wx, lru_bx=lru_bx, lru_lambda=lru_lambda,
             w_branch_bf=w_branch.astype(BF16), w_out_bf=w_out.astype(BF16))

    y_prompt = x_prompt
    cond_ctx = c_ctx[None, :]
    ctx_states = []
    for l in range(DEPTH):
        y_prompt, st = trunk_layer(y_prompt, cond_ctx, P, l, None)
        ctx_states.append(st)
    new_mla_ckv = jnp.stack([s[0] for s in ctx_states], axis=1)
    new_mla_krope = jnp.stack([s[1] for s in ctx_states], axis=1)
    new_rwkv = jnp.stack([s[2] for s in ctx_states], axis=1)
    new_ssd = jnp.stack([s[3] for s in ctx_states], axis=1)
    new_lru = jnp.stack([s[4] for s in ctx_states], axis=1)

    y_sample = x_sample
    for l in range(DEPTH):
        ctx = (cache_mla_ckv[:, l], cache_mla_krope[:, l], state_rwkv[:, l], state_ssd[:, l], state_lru[:, l])
        y_sample, _ = trunk_layer(y_sample, c, P, l, ctx)

    return (y_prompt, y_sample, new_mla_ckv, new_mla_krope, new_rwkv, new_ssd, new_lru)
```

```python
import math

import jax
import jax.numpy as jnp
import numpy as np
from jax import lax
from jax.experimental import pallas as pl
from jax.experimental.pallas import tpu as pltpu

F32 = jnp.float32
BF16 = jnp.bfloat16
HIGHEST = lax.Precision.HIGHEST

D_MODEL = 1024
DEPTH = 2
GRID_W = 64
N_BRANCH = 4
BR_WIDTH = 512
EPS = 1e-6
CONV_K = 4

RW_HEADS = 8
RW_HEAD = 64
RW_LORA = 64
RW_GN_EPS = 64e-5
RW_PRE = 3 * BR_WIDTH + 2 * RW_LORA

MLA_HEADS = 8
MLA_NOPE = 64
MLA_ROPE = 32
MLA_QK = MLA_NOPE + MLA_ROPE
MLA_V = 64
MLA_Q_RANK = 384
MLA_KV_RANK = 256
ROPE_THETA = 10000.0

SSD_HEADS = 8
SSD_HEAD_DIM = 64
SSD_GROUPS = 2
SSD_STATE = 64
SSD_CONV_DIM = BR_WIDTH + 2 * SSD_GROUPS * SSD_STATE

LRU_WIDTH = 512
LRU_BLOCKS = 8
LRU_C = 8.0

IN_WIDTHS = (RW_PRE, BR_WIDTH, MLA_Q_RANK, MLA_KV_RANK + MLA_ROPE, BR_WIDTH, BR_WIDTH, SSD_CONV_DIM,
             2 * SSD_HEADS, LRU_WIDTH, LRU_WIDTH, N_BRANCH * D_MODEL)
IN_OFFS = tuple(int(v) for v in np.cumsum((0,) + IN_WIDTHS))

LANE = 128
SUBLANE = 8
VMEM_LIMIT = 48 * 1024 * 1024

RW_CHUNK = 64
PAIRS = BR_WIDTH // LANE
SSD_Q = 128
MLA_W = MLA_HEADS * LANE
MLA_TK = 512
NEG_INF = -1e30

Z_RW_W = BR_WIDTH + RW_PRE
Z_MLA_W = MLA_W + BR_WIDTH + 512 + MLA_KV_RANK
Z_SSD_W = BR_WIDTH + SSD_CONV_DIM + LANE
Z_LRU_W = 2 * LRU_WIDTH
MLA_GATE_LANE_BLOCK = MLA_W // LANE
MLA_Q_BLOCK = (MLA_W + BR_WIDTH) // 512
MLA_CKV_BLOCK = (MLA_W + BR_WIDTH + 512) // MLA_KV_RANK


def _cparams(*sem):
    return pltpu.CompilerParams(dimension_semantics=sem, vmem_limit_bytes=VMEM_LIMIT)


def _const_spec(shape):
    return pl.BlockSpec(shape, lambda *_: (0,) * len(shape))


def _mm(a, b):
    return jnp.dot(a.astype(BF16), b.astype(BF16), preferred_element_type=F32)


def _mm_nt(a, b):
    return lax.dot_general(a.astype(BF16), b.astype(BF16), (((1,), (1,)), ((), ())), preferred_element_type=F32)


def _mm_f32(a, b):
    return jnp.dot(a, b, preferred_element_type=F32, precision=HIGHEST)


def _iota2(shape, dim):
    return lax.broadcasted_iota(jnp.int32, shape, dim)


def _silu(x):
    return x * jax.nn.sigmoid(x)


def _softplus(x):
    return jnp.maximum(x, 0.0) + jnp.log(1.0 + jnp.exp(-jnp.abs(x)))


def _head_sum(x, width):
    bd = ((_iota2((LANE, LANE), 0) // width) == (_iota2((LANE, LANE), 1) // width)).astype(BF16)
    hi = x.astype(BF16)
    lo = (x - hi.astype(F32)).astype(BF16)
    outs = []
    for j in range(x.shape[-1] // LANE):
        sl = slice(j * LANE, (j + 1) * LANE)
        outs.append(jnp.dot(hi[:, sl], bd, preferred_element_type=F32) + jnp.dot(lo[:, sl], bd, preferred_element_type=F32))
    return jnp.concatenate(outs, axis=1)


def _from_prev(u, s, prev8):
    rolled = pltpu.roll(u, s, 0)
    head = jnp.where(_iota2(prev8.shape, 0) < s, pltpu.roll(prev8, s, 0), rolled[:SUBLANE])
    return jnp.concatenate([head, rolled[SUBLANE:]], axis=0)


def _from_next(u, next8):
    n = u.shape[0]
    rolled = pltpu.roll(u, n - 1, 0)
    tail = jnp.where(_iota2(next8.shape, 0) == SUBLANE - 1, pltpu.roll(next8, SUBLANE - 1, 0), rolled[n - SUBLANE:])
    return jnp.concatenate([rolled[:n - SUBLANE], tail], axis=0)


def _centred_conv(x, prev8, next8, w_ref, b_ref):
    return (w_ref[0:1, :] * _from_prev(x, 2, prev8) + w_ref[1:2, :] * _from_prev(x, 1, prev8)
            + w_ref[2:3, :] * x + w_ref[3:4, :] * _from_next(x, next8) + b_ref[...])


def _halo_specs(tq, n, c):
    per8, nb8 = tq // SUBLANE, n // SUBLANE
    return [pl.BlockSpec((1, tq, c), lambda bi, i: (bi, i, 0)),
            pl.BlockSpec((1, SUBLANE, c), lambda bi, i: (bi, jnp.maximum(i * per8 - 1, 0), 0)),
            pl.BlockSpec((1, SUBLANE, c), lambda bi, i: (bi, jnp.minimum((i + 1) * per8, nb8 - 1), 0))]


def _halo_rows(zp_ref, zn_ref, cols):
    i = pl.program_id(1)
    prev8 = jnp.where(i > 0, zp_ref[0, :, cols], 0.0)
    next8 = jnp.where(i < pl.num_programs(1) - 1, zn_ref[0, :, cols], 0.0)
    return prev8, next8


def _ada_body(c_ref, w_ref, b_ref, o_ref):
    o_ref[...] = _mm(_silu(c_ref[...]), w_ref[...]) + b_ref[...]


def ada_mod(cond8, w, bias):
    d = cond8.shape[1]
    return pl.pallas_call(
        _ada_body,
        grid=(w.shape[1] // d,),
        in_specs=[_const_spec((SUBLANE, d)), pl.BlockSpec((d, d), lambda j: (0, j)), pl.BlockSpec((1, d), lambda j: (0, j))],
        out_specs=pl.BlockSpec((SUBLANE, d), lambda j: (0, j)),
        out_shape=jax.ShapeDtypeStruct((SUBLANE, w.shape[1]), F32),
        compiler_params=_cparams("parallel"),
        name="ada_mod",
    )(cond8, w, bias.reshape(1, -1))


def _prenorm_body(x_ref, g_ref, sc_ref, sh_ref, h_ref):
    x = x_ref[...]
    y = x * lax.rsqrt(jnp.mean(x * x, axis=-1, keepdims=True) + EPS)
    h_ref[...] = ((y * g_ref[...]) * (1.0 + sc_ref[0]) + sh_ref[0]).astype(h_ref.dtype)


def _mod_index(bm, per):
    return (lambda i: (i // per, 0, 0)) if bm > 1 else (lambda i: (0, 0, 0))


def prenorm(x2d, g, scale, shift, n, tm):
    t, d = x2d.shape
    mod_idx = _mod_index(scale.shape[0], n // tm)
    return pl.pallas_call(
        _prenorm_body,
        grid=(t // tm,),
        in_specs=[pl.BlockSpec((tm, d), lambda i: (i, 0)), _const_spec((1, d)),
                  pl.BlockSpec((1, 1, d), mod_idx), pl.BlockSpec((1, 1, d), mod_idx)],
        out_specs=pl.BlockSpec((tm, d), lambda i: (i, 0)),
        out_shape=jax.ShapeDtypeStruct((t, d), BF16),
        compiler_params=_cparams("parallel"),
        name="prenorm",
    )(x2d, g.reshape(1, -1), scale, shift)


def _mm_body(a_ref, w_ref, o_ref):
    o_ref[...] = jnp.dot(a_ref[...], w_ref[...], preferred_element_type=F32).astype(o_ref.dtype)


def matmul(a, w, tm, tn, out_dtype):
    t, k = a.shape
    n = w.shape[1]
    return pl.pallas_call(
        _mm_body,
        grid=(n // tn, t // tm),
        in_specs=[pl.BlockSpec((tm, k), lambda j, i: (i, 0)), pl.BlockSpec((k, tn), lambda j, i: (0, j))],
        out_specs=pl.BlockSpec((tm, tn), lambda j, i: (i, j)),
        out_shape=jax.ShapeDtypeStruct((t, n), out_dtype),
        compiler_params=_cparams("parallel", "parallel"),
        name="in_proj",
    )(a, w)


def _merge_body(x_ref, b0_ref, b1_ref, b2_ref, b3_ref, lg_ref, wb_ref, wo_ref, gate_ref, y_ref):
    merged = None
    for m, b_ref in enumerate((b0_ref, b1_ref, b2_ref, b3_ref)):
        proj = jnp.dot(b_ref[...], wb_ref[m], preferred_element_type=F32)
        g = jax.nn.sigmoid(lg_ref[:, m * D_MODEL:(m + 1) * D_MODEL].astype(F32))
        merged = g * proj if merged is None else merged + g * proj
    out = jnp.dot(merged.astype(BF16), wo_ref[...], preferred_element_type=F32)
    y_ref[...] = x_ref[...] + gate_ref[0] * out


def merge(x2d, branches, logits, wb, wo, gate, n, tm):
    t, d = x2d.shape
    mod_idx = _mod_index(gate.shape[0], n // tm)
    row = lambda i: (i, 0)
    return pl.pallas_call(
        _merge_body,
        grid=(t // tm,),
        in_specs=[pl.BlockSpec((tm, d), row)] + [pl.BlockSpec((tm, BR_WIDTH), row)] * N_BRANCH
        + [pl.BlockSpec((tm, N_BRANCH * d), row), _const_spec((N_BRANCH, BR_WIDTH, d)), _const_spec((d, d)),
           pl.BlockSpec((1, 1, d), mod_idx)],
        out_specs=pl.BlockSpec((tm, d), row),
        out_shape=jax.ShapeDtypeStruct((t, d), F32),
        compiler_params=_cparams("parallel"),
        name="merge",
    )(x2d, *branches, logits, wb, wo, gate)


def _rw_prep_body(z_ref, zp_ref, zn_ref, mu_ref, kkg_ref, ka_ref, rk_ref, w0_ref, w2_ref, a0_ref, a2_ref,
                  r_ref, kk_ref, v_ref, lw0_ref, k0_ref, b0_ref, lw1_ref, k1_ref, b1_ref, bonus_ref):
    cols = slice(BR_WIDTH, Z_RW_W)
    prev8, next8 = _halo_rows(zp_ref, zn_ref, cols)
    u = z_ref[0, :, cols]
    u = u + mu_ref[...] * (0.5 * (_from_prev(u, 1, prev8) + _from_next(u, next8)) - u)
    r = u[:, 0:BR_WIDTH]
    k = u[:, BR_WIDTH:2 * BR_WIDTH]
    v = u[:, 2 * BR_WIDTH:3 * BR_WIDTH]
    lo = u[:, 3 * BR_WIDTH:]
    lo = jnp.where(_iota2(lo.shape, 1) < RW_LORA, jnp.tanh(lo), lo)
    kk = k * kkg_ref[...]
    kk = kk * lax.rsqrt(_head_sum(kk * kk, RW_HEAD) + 1e-12)
    r_ref[0] = r
    kk_ref[0] = kk
    v_ref[0] = v
    coef = jnp.zeros_like(r)
    for d, (lw_ref, k_ref, b_ref) in enumerate(((lw0_ref, k0_ref, b0_ref), (lw1_ref, k1_ref, b1_ref))):
        w_raw = w0_ref[d] + _mm_f32(lo, w2_ref[d])
        lw_ref[0] = -math.exp(-0.5) * jax.nn.sigmoid(w_raw)
        a = jax.nn.sigmoid(a0_ref[d] + _mm_f32(lo, a2_ref[d]))
        k_d = k * (1.0 + (a - 1.0) * ka_ref[...])
        k_ref[0] = k_d
        b_ref[0] = kk * a
        coef = coef + _head_sum(r * k_d * rk_ref[...], RW_HEAD)
    bonus_ref[0] = coef * v


def rw_prep(z_rw, P, l, tq):
    b, n, c = z_rw.shape
    w2 = jnp.concatenate([P['rw_w2'][l], jnp.zeros_like(P['rw_w2'][l])], axis=1)
    a2 = jnp.concatenate([jnp.zeros_like(P['rw_a2'][l]), P['rw_a2'][l]], axis=1)
    row = lambda x: x.reshape(1, -1)
    tile = pl.BlockSpec((1, tq, BR_WIDTH), lambda bi, i: (bi, i, 0))
    out = jax.ShapeDtypeStruct((b, n, BR_WIDTH), F32)
    return pl.pallas_call(
        _rw_prep_body,
        grid=(b, n // tq),
        in_specs=_halo_specs(tq, n, c) + [_const_spec((1, RW_PRE))] + [_const_spec((1, BR_WIDTH))] * 3
        + [_const_spec((2, 1, BR_WIDTH)), _const_spec((2, LANE, BR_WIDTH))] * 2,
        out_specs=[tile] * 10,
        out_shape=[out] * 10,
        compiler_params=_cparams("parallel", "parallel"),
        name="rw_prep",
    )(z_rw, z_rw, z_rw, row(P['rw_mu'][l]), row(P['rw_kk'][l]), row(P['rw_ka'][l]), row(P['rw_rk'][l]),
      P['rw_w0'][l][:, None, :], w2, P['rw_a0'][l][:, None, :], a2)


def _rw_masks(rev):
    n = 2 * RW_CHUNK
    ri, ci = _iota2((n, n), 0), _iota2((n, n), 1)
    same = (ri // RW_CHUNK) == (ci // RW_CHUNK)
    strict = same & ((ci > ri) if rev else (ci < ri))
    incl = same & ((ci >= ri) if rev else (ci <= ri))
    blk8 = strict & ((ri // 8) == (ci // 8))
    levels = [strict & ((ri // (2 * s)) == (ci // (2 * s))) & ((ri // s) != (ci // s)) for s in (8, 16, 32)]
    eye = (ri == ci).astype(F32)
    t_i, s_i = _iota2((RW_CHUNK, RW_CHUNK), 0), _iota2((RW_CHUNK, RW_CHUNK), 1)
    tri = ((s_i >= t_i) if rev else (s_i <= t_i)).astype(F32)
    m0 = (_iota2((RW_CHUNK, LANE), 1) < RW_HEAD).astype(F32)
    return dict(strict=strict, incl=incl, blk8=blk8, levels=levels, eye=eye, tri=tri, m0=m0, m1=1.0 - m0)


def _rw_chunk(rev, mk, lw, r, kk, v, k, b, s):
    cum = _mm_f32(mk['tri'], lw)
    total = cum[0:1] if rev else cum[RW_CHUNK - 1:RW_CHUNK]
    e_in, e_ex, e_neg, e_tot = jnp.exp(cum), jnp.exp(cum - lw), jnp.exp(-cum), jnp.exp(total - cum)
    stack = lambda x: jnp.concatenate([x * mk['m0'], x * mk['m1']], axis=0)
    rs, ks, vs = stack(r * e_in), stack(kk * e_ex), stack(v)
    rhs = jnp.concatenate([stack(b * e_neg), stack(k * e_neg)], axis=0)
    g1 = _mm_nt(ks, rhs)
    g2 = _mm_nt(rs, rhs)
    n = 2 * RW_CHUNK
    a_b = jnp.where(mk['strict'], g1[:, :n], 0.0)
    a_k = jnp.where(mk['strict'], g1[:, n:], 0.0)
    m_b = jnp.where(mk['incl'], g2[:, :n], 0.0)
    m_k = jnp.where(mk['incl'], g2[:, n:], 0.0)
    x = -jnp.where(mk['blk8'], a_b, 0.0)
    x2 = _mm(x, x)
    x4 = _mm(x2, x2)
    t_inv = mk['eye'] + x
    t_inv = t_inv + _mm(t_inv, x2)
    t_inv = t_inv + _mm(t_inv, x4)
    for lvl in mk['levels']:
        t_inv = t_inv - _mm(_mm(t_inv, jnp.where(lvl, a_b, 0.0)), t_inv)
    tx = _mm(t_inv, jnp.concatenate([_mm(a_k, vs), ks], axis=1))
    u2, w1 = -tx[:, :LANE], -tx[:, LANE:]
    q_eff = rs + _mm(m_b, w1)
    o_intra = _mm(jnp.concatenate([m_b, m_k], axis=1), jnp.concatenate([u2, vs], axis=0))
    u = _mm_nt(w1, s) + u2
    o_s = _mm_nt(q_eff, s) + o_intra
    o = o_s[:RW_CHUNK] + o_s[RW_CHUNK:]
    uv_t = jnp.concatenate([u, vs], axis=0).T
    s_new = s * jnp.exp(total) + _mm(uv_t, jnp.concatenate([stack(b * e_tot), stack(k * e_tot)], axis=0))
    return o, s_new


def _rw_scan_body(rf_ref, kkf_ref, vf_ref, lwf_ref, kf_ref, bf_ref, rb_ref, kkb_ref, vb_ref, lwb_ref, kb_ref, bb_ref,
                  s0_ref, of_ref, ob_ref, sfin_ref, s_scr):
    c = pl.program_id(1)

    @pl.when(c == 0)
    def _():
        s_scr[...] = s0_ref[0]

    for d, refs, o_ref in ((0, (lwf_ref, rf_ref, kkf_ref, vf_ref, kf_ref, bf_ref), of_ref),
                           (1, (lwb_ref, rb_ref, kkb_ref, vb_ref, kb_ref, bb_ref), ob_ref)):
        mk = _rw_masks(rev=(d == 1))
        for p in range(PAIRS):
            sl = slice(p * LANE, (p + 1) * LANE)
            o, s_new = _rw_chunk(d == 1, mk, *(ref[0, :, sl] for ref in refs), s_scr[d, p])
            o_ref[0, :, sl] = o
            s_scr[d, p] = s_new

    @pl.when(c == pl.num_programs(1) - 1)
    def _():
        sfin_ref[0] = s_scr[...]


def rw_scan(prep, s0_pairs):
    r, kk, v, lw0, k0, b0, lw1, k1, b1, _ = prep
    b, n, _ = r.shape
    nc = n // RW_CHUNK
    fwd = pl.BlockSpec((1, RW_CHUNK, BR_WIDTH), lambda bi, c: (bi, c, 0))
    bwd = pl.BlockSpec((1, RW_CHUNK, BR_WIDTH), lambda bi, c: (bi, nc - 1 - c, 0))
    st = pl.BlockSpec((1, 2, PAIRS, LANE, LANE), lambda bi, c: (bi, 0, 0, 0, 0))
    return pl.pallas_call(
        _rw_scan_body,
        grid=(b, nc),
        in_specs=[fwd] * 6 + [bwd] * 6 + [st],
        out_specs=[fwd, bwd, st],
        out_shape=[jax.ShapeDtypeStruct((b, n, BR_WIDTH), F32)] * 2
        + [jax.ShapeDtypeStruct((b, 2, PAIRS, LANE, LANE), F32)],
        scratch_shapes=[pltpu.VMEM((2, PAIRS, LANE, LANE), F32)],
        compiler_params=_cparams("parallel", "arbitrary"),
        name="rw_scan",
    )(r, kk, v, lw0, k0, b0, r, kk, v, lw1, k1, b1, s0_pairs)


def rw_state_to_pairs(s0):
    b = s0.shape[0]
    s = s0.reshape(b, 2, PAIRS, 2, RW_HEAD, RW_HEAD)
    z = jnp.zeros_like(s[:, :, :, 0])
    top = jnp.concatenate([s[:, :, :, 0], z], axis=-1)
    bot = jnp.concatenate([z, s[:, :, :, 1]], axis=-1)
    return jnp.concatenate([top, bot], axis=-2)


def rw_pairs_to_state(sp):
    b = sp.shape[0]
    h0 = sp[:, :, :, :RW_HEAD, :RW_HEAD]
    h1 = sp[:, :, :, RW_HEAD:, RW_HEAD:]
    return jnp.stack([h0, h1], axis=3).reshape(b, 2, RW_HEADS, RW_HEAD, RW_HEAD)


def _rw_post_body(of_ref, ob_ref, bonus_ref, gate_ref, g_ref, bias_ref, o_ref):
    wkv = of_ref[...] + ob_ref[...]
    mu = _head_sum(wkv, RW_HEAD) * (1.0 / RW_HEAD)
    xc = wkv - mu
    var = _head_sum(xc * xc, RW_HEAD) * (1.0 / RW_HEAD)
    o = xc * lax.rsqrt(var + RW_GN_EPS) * g_ref[...] + bias_ref[...] + bonus_ref[...]
    o_ref[...] = (o * _silu(gate_ref[...])).astype(o_ref.dtype)


def rw_post(o_f, o_b, bonus, z_rw2d, ln_g, ln_b, tm):
    t = o_f.shape[0]
    tile = pl.BlockSpec((tm, BR_WIDTH), lambda i: (i, 0))
    return pl.pallas_call(
        _rw_post_body,
        grid=(t // tm,),
        in_specs=[tile] * 4 + [_const_spec((1, BR_WIDTH))] * 2,
        out_specs=tile,
        out_shape=jax.ShapeDtypeStruct((t, BR_WIDTH), BF16),
        compiler_params=_cparams("parallel"),
        name="rw_post",
    )(o_f, o_b, bonus, z_rw2d, ln_g.reshape(1, -1), ln_b.reshape(1, -1))


def _ssd_prep_body(z_ref, zp_ref, zn_ref, w_ref, b_ref, dtb_ref, nega_ref, xbc_ref, dtp_ref):
    cols = slice(BR_WIDTH, BR_WIDTH + SSD_CONV_DIM)
    prev8, next8 = _halo_rows(zp_ref, zn_ref, cols)
    xbc_ref[0] = _silu(_centred_conv(z_ref[0, :, cols], prev8, next8, w_ref, b_ref))
    dt = _softplus(z_ref[0, :, BR_WIDTH + SSD_CONV_DIM:] + dtb_ref[...])
    dta = pltpu.roll(dt, 2 * SSD_HEADS, 1) * nega_ref[...]
    lane = _iota2(dt.shape, 1)
    dtp_ref[0] = jnp.where(lane < 2 * SSD_HEADS, dt, jnp.where(lane < 4 * SSD_HEADS, dta, 0.0))


def ssd_prep(z_ssd, conv_w, conv_b, dt_bias, a_log, tq):
    b, n, c = z_ssd.shape
    dtb = jnp.pad(dt_bias.reshape(1, -1), ((0, 0), (0, LANE - 2 * SSD_HEADS)))
    nega = jnp.pad(-jnp.exp(a_log).reshape(1, -1), ((0, 0), (2 * SSD_HEADS, LANE - 4 * SSD_HEADS)))
    return pl.pallas_call(
        _ssd_prep_body,
        grid=(b, n // tq),
        in_specs=_halo_specs(tq, n, c) + [_const_spec((CONV_K, SSD_CONV_DIM)), _const_spec((1, SSD_CONV_DIM)),
                                           _const_spec((1, LANE)), _const_spec((1, LANE))],
        out_specs=[pl.BlockSpec((1, tq, SSD_CONV_DIM), lambda bi, i: (bi, i, 0)),
                   pl.BlockSpec((1, tq, LANE), lambda bi, i: (bi, i, 0))],
        out_shape=[jax.ShapeDtypeStruct((b, n, SSD_CONV_DIM), F32), jax.ShapeDtypeStruct((b, n, LANE), F32)],
        compiler_params=_cparams("parallel", "parallel"),
        name="ssd_prep",
    )(z_ssd, z_ssd, z_ssd, conv_w, conv_b.reshape(1, -1), dtb, nega)


def _ssd_chunk(rev, d, xbc, dtp, st_ref):
    q = SSD_Q
    li, si = _iota2((q, q), 0), _iota2((q, q), 1)
    incl = (si >= li) if rev else (si <= li)
    tri = incl.astype(F32)
    er, ec = _iota2((LANE, BR_WIDTH), 0), _iota2((LANE, BR_WIDTH), 1) // SSD_HEAD_DIM
    e_dt = (er == d * SSD_HEADS + ec).astype(F32)
    e_a = (er == (2 + d) * SSD_HEADS + ec).astype(F32)
    dt_full = _mm_f32(dtp, e_dt)
    acs = _mm_f32(tri, _mm_f32(dtp, e_a))
    a_tot = acs[0:1] if rev else acs[q - 1:q]
    a_rows = dtp.T[(2 + d) * SSD_HEADS:(3 + d) * SSD_HEADS, :]
    acs_rows = lax.dot_general(a_rows, tri, (((1,), (1,)), ((), ())), preferred_element_type=F32, precision=HIGHEST)
    x, bm, cm = xbc[:, :BR_WIDTH], xbc[:, BR_WIDTH:BR_WIDTH + LANE], xbc[:, BR_WIDTH + LANE:]
    xd = x * dt_full
    e_acs = jnp.exp(acs)
    xdec = xd * jnp.exp(a_tot - acs)
    e_tot = jnp.exp(a_tot)
    m0 = (_iota2((q, LANE), 1) < SSD_HEAD_DIM).astype(F32)
    m1 = 1.0 - m0
    ys = []
    for g in range(SSD_GROUPS):
        mg = m0 if g == 0 else m1
        cb = _mm_nt(cm * mg, bm)
        bm_t = (bm * mg).T
        for p in range(2 * g, 2 * g + 2):
            sl = slice(p * LANE, (p + 1) * LANE)
            ms = []
            for hh in range(2):
                h = 2 * p + hh
                col = acs[:, h * SSD_HEAD_DIM:h * SSD_HEAD_DIM + 1]
                ms.append(jnp.where(incl, cb * jnp.exp(jnp.minimum(col - acs_rows[h:h + 1, :], 0.0)), 0.0))
            xs = xd[:, sl]
            y = _mm(jnp.concatenate(ms, axis=1), jnp.concatenate([xs * m0, xs * m1], axis=0))
            y = y + _mm(cm, st_ref[p]) * e_acs[:, sl]
            st_ref[p] = st_ref[p] * e_tot[:, sl] + _mm(bm_t, xdec[:, sl])
            ys.append(y)
    return jnp.concatenate(ys, axis=1)


def _ssd_scan_body(xf_ref, dtf_ref, xb_ref, dtb_ref, s0_ref, yf_ref, yb_ref, sfin_ref, st_scr):
    c = pl.program_id(1)

    @pl.when(c == 0)
    def _():
        for d in range(2):
            for p in range(PAIRS):
                st_scr[d, p] = s0_ref[0, d, p].T

    yf_ref[0] = _ssd_chunk(False, 0, xf_ref[0], dtf_ref[0], st_scr.at[0])
    yb_ref[0] = _ssd_chunk(True, 1, xb_ref[0], dtb_ref[0], st_scr.at[1])

    @pl.when(c == pl.num_programs(1) - 1)
    def _():
        for d in range(2):
            for p in range(PAIRS):
                sfin_ref[0, d, p] = st_scr[d, p].T


def ssd_scan(xbc, dtp, s0_pairs):
    b, n, _ = xbc.shape
    nc = n // SSD_Q
    fw = lambda w: pl.BlockSpec((1, SSD_Q, w), lambda bi, c: (bi, c, 0))
    bw = lambda w: pl.BlockSpec((1, SSD_Q, w), lambda bi, c: (bi, nc - 1 - c, 0))
    st = pl.BlockSpec((1, 2, PAIRS, LANE, LANE), lambda bi, c: (bi, 0, 0, 0, 0))
    return pl.pallas_call(
        _ssd_scan_body,
        grid=(b, nc),
        in_specs=[fw(SSD_CONV_DIM), fw(LANE), bw(SSD_CONV_DIM), bw(LANE), st],
        out_specs=[fw(BR_WIDTH), bw(BR_WIDTH), st],
        out_shape=[jax.ShapeDtypeStruct((b, n, BR_WIDTH), F32)] * 2
        + [jax.ShapeDtypeStruct((b, 2, PAIRS, LANE, LANE), F32)],
        scratch_shapes=[pltpu.VMEM((2, PAIRS, LANE, LANE), F32)],
        compiler_params=_cparams("parallel", "arbitrary"),
        name="ssd_scan",
    )(xbc, dtp, xbc, dtp, s0_pairs)


def ssd_state_to_pairs(h0):
    b = h0.shape[0]
    s = h0.reshape(b, 2, PAIRS, LANE, SSD_STATE)
    lo = jnp.pad(s, ((0, 0),) * 4 + ((0, SSD_STATE),))
    hi = jnp.pad(s, ((0, 0),) * 4 + ((SSD_STATE, 0),))
    return jnp.where(_pair_in_group0(), lo, hi)


def _pair_in_group0():
    return (jnp.arange(PAIRS) < PAIRS // SSD_GROUPS)[None, None, :, None, None]


def ssd_pairs_to_state(sp):
    b = sp.shape[0]
    s = jnp.where(_pair_in_group0(), sp[..., :SSD_STATE], sp[..., SSD_STATE:])
    return s.reshape(b, 2, SSD_HEADS, SSD_HEAD_DIM, SSD_STATE)


def _ssd_post_body(yf_ref, yb_ref, x_ref, gate_ref, d_ref, g_ref, o_ref):
    y = (x_ref[...] * d_ref[...] + yf_ref[...] + yb_ref[...]) * _silu(gate_ref[...])
    y = y * lax.rsqrt(jnp.mean(y * y, axis=-1, keepdims=True) + EPS)
    o_ref[...] = (y * g_ref[...]).astype(o_ref.dtype)


def ssd_post(y_f, y_b, xbc2d, z_ssd2d, d_full, norm_g, tm):
    t = y_f.shape[0]
    tile = pl.BlockSpec((tm, BR_WIDTH), lambda i: (i, 0))
    return pl.pallas_call(
        _ssd_post_body,
        grid=(t // tm,),
        in_specs=[tile] * 4 + [_const_spec((1, BR_WIDTH))] * 2,
        out_specs=tile,
        out_shape=jax.ShapeDtypeStruct((t, BR_WIDTH), BF16),
        compiler_params=_cparams("parallel"),
        name="ssd_post",
    )(y_f, y_b, xbc2d, z_ssd2d, d_full.reshape(1, -1), norm_g.reshape(1, -1))


def _lru_prep_body(z_ref, zp_ref, zn_ref, w_ref, b_ref, wa_ref, ba_ref, wx_ref, bx_ref, sp_ref,
                   a0_ref, u0_ref, a1_ref, u1_ref):
    cols = slice(0, LRU_WIDTH)
    prev8, next8 = _halo_rows(zp_ref, zn_ref, cols)
    xc = _centred_conv(z_ref[0, :, cols], prev8, next8, w_ref, b_ref)
    for d, (a_ref, u_ref) in enumerate(((a0_ref, u0_ref), (a1_ref, u1_ref))):
        g_r = jax.nn.sigmoid(_mm(xc, wa_ref[d]) + ba_ref[d])
        g_i = jax.nn.sigmoid(_mm(xc, wx_ref[d]) + bx_ref[d])
        log_a = -LRU_C * g_r * sp_ref[d]
        t = jnp.tanh(log_a)
        a_ref[0] = jnp.exp(log_a)
        u_ref[0] = jnp.sqrt(-2.0 * t / (1.0 - t)) * (g_i * xc)


def _block_diag(w):
    d, k, bi, bo = w.shape
    return jnp.einsum('dkij,kl->dkilj', w, jnp.eye(k, dtype=w.dtype)).reshape(d, k * bi, k * bo)


def lru_prep(z_lru, conv_w, conv_b, wa, ba, wx, bx, lam, tq):
    b, n, c = z_lru.shape
    tile = pl.BlockSpec((1, tq, LRU_WIDTH), lambda bi, i: (bi, i, 0))
    out = jax.ShapeDtypeStruct((b, n, LRU_WIDTH), F32)
    sp = _softplus(-lam)[:, None, :]
    return pl.pallas_call(
        _lru_prep_body,
        grid=(b, n // tq),
        in_specs=_halo_specs(tq, n, c) + [_const_spec((CONV_K, LRU_WIDTH)), _const_spec((1, LRU_WIDTH)),
                                           _const_spec((2, LRU_WIDTH, LRU_WIDTH)), _const_spec((2, 1, LRU_WIDTH)),
                                           _const_spec((2, LRU_WIDTH, LRU_WIDTH)), _const_spec((2, 1, LRU_WIDTH)),
                                           _const_spec((2, 1, LRU_WIDTH))],
        out_specs=[tile] * 4,
        out_shape=[out] * 4,
        compiler_params=_cparams("parallel", "parallel"),
        name="lru_prep",
    )(z_lru, z_lru, z_lru, conv_w, conv_b.reshape(1, -1), _block_diag(wa).astype(BF16), ba[:, None, :],
      _block_diag(wx).astype(BF16), bx[:, None, :], sp)


def _lru_tile_scan(rev, a, u, carry):
    n = a.shape[0]
    row = _iota2(a.shape, 0)
    s = 1
    while s < n:
        shift = n - s if rev else s
        keep = (row < n - s) if rev else (row >= s)
        a_sh = jnp.where(keep, pltpu.roll(a, shift, 0), 1.0)
        u_sh = jnp.where(keep, pltpu.roll(u, shift, 0), 0.0)
        u = u + a * u_sh
        a = a * a_sh
        s *= 2
    h = a * carry + u
    return h, (h[0:1] if rev else h[n - 1:n])


def _lru_scan_body(af_ref, uf_ref, ab_ref, ub_ref, h0_ref, yf_ref, yb_ref, hfin_ref, h_scr):
    c = pl.program_id(1)

    @pl.when(c == 0)
    def _():
        h_scr[...] = h0_ref[0]

    hf, cf = _lru_tile_scan(False, af_ref[0], uf_ref[0], h_scr[0:1])
    yf_ref[0] = hf
    h_scr[0:1] = cf
    hb, cb = _lru_tile_scan(True, ab_ref[0], ub_ref[0], h_scr[1:2])
    yb_ref[0] = hb
    h_scr[1:2] = cb

    @pl.when(c == pl.num_programs(1) - 1)
    def _():
        hfin_ref[0] = h_scr[...]


def lru_scan(a0, u0, a1, u1, h0, tq):
    b, n, _ = a0.shape
    nc = n // tq
    fw = pl.BlockSpec((1, tq, LRU_WIDTH), lambda bi, c: (bi, c, 0))
    bw = pl.BlockSpec((1, tq, LRU_WIDTH), lambda bi, c: (bi, nc - 1 - c, 0))
    st = pl.BlockSpec((1, 2, LRU_WIDTH), lambda bi, c: (bi, 0, 0))
    return pl.pallas_call(
        _lru_scan_body,
        grid=(b, nc),
        in_specs=[fw, fw, bw, bw, st],
        out_specs=[fw, bw, st],
        out_shape=[jax.ShapeDtypeStruct((b, n, LRU_WIDTH), F32)] * 2 + [jax.ShapeDtypeStruct((b, 2, LRU_WIDTH), F32)],
        scratch_shapes=[pltpu.VMEM((2, LRU_WIDTH), F32)],
        compiler_params=_cparams("parallel", "arbitrary"),
        name="lru_scan",
    )(a0, u0, a1, u1, h0)


def _lru_post_body(yf_ref, yb_ref, gate_ref, o_ref):
    o_ref[...] = ((yf_ref[...] + yb_ref[...]) * _silu(gate_ref[...])).astype(o_ref.dtype)


def lru_post(y_f, y_b, z_lru2d, tm):
    t = y_f.shape[0]
    tile = pl.BlockSpec((tm, LRU_WIDTH), lambda i: (i, 0))
    return pl.pallas_call(
        _lru_post_body,
        grid=(t // tm,),
        in_specs=[tile, tile, pl.BlockSpec((tm, LRU_WIDTH), lambda i: (i, 1))],
        out_specs=tile,
        out_shape=jax.ShapeDtypeStruct((t, LRU_WIDTH), BF16),
        compiler_params=_cparams("parallel"),
        name="lru_post",
    )(y_f, y_b, z_lru2d)


def rope_tables(n):
    t = np.arange(n)
    pos = jnp.asarray(np.stack([t // GRID_W, t % GRID_W], axis=0).astype(np.float32))
    nf = MLA_ROPE // 4
    inv = jnp.asarray(ROPE_THETA, F32) ** (-jnp.arange(nf, dtype=F32) / nf)
    rel = np.arange(LANE) - MLA_NOPE
    in_rope = (rel >= 0) & (rel < MLA_ROPE)
    half = np.clip(rel // (2 * nf), 0, 1)
    within = rel % (2 * nf)
    is_x2 = within >= nf
    ang = pos[half, :].T * inv[within % nf][None, :]
    cos = jnp.where(in_rope[None, :], jnp.cos(ang), 1.0)
    sin = jnp.sin(ang)
    sin_a = jnp.where((in_rope & ~is_x2)[None, :], -sin, 0.0)
    sin_b = jnp.where((in_rope & is_x2)[None, :], sin, 0.0)
    return cos, sin_a, sin_b


def _head_norm_rope(x, g, rope):
    nf = MLA_ROPE // 4
    outs = []
    for h in range(MLA_HEADS):
        xh = x[:, h * LANE:(h + 1) * LANE]
        ms = jnp.sum(xh * xh, axis=-1, keepdims=True) * (1.0 / MLA_QK)
        xh = xh * lax.rsqrt(ms + EPS) * g
        if rope is not None:
            cos, sin_a, sin_b = rope
            xh = xh * cos + pltpu.roll(xh, LANE - nf, 1) * sin_a + pltpu.roll(xh, nf, 1) * sin_b
        outs.append(xh)
    return jnp.concatenate(outs, axis=1)


def _mla_q_body(use_rope, q_ref, qag_ref, wq_ref, qng_ref, *rest):
    *rope_refs, o_ref = rest
    x = q_ref[:, :MLA_Q_RANK]
    x = x * lax.rsqrt(jnp.mean(x * x, axis=-1, keepdims=True) + EPS) * qag_ref[...]
    q = _mm(x, wq_ref[...])
    rope = tuple(r[...] for r in rope_refs) if use_rope else None
    q = _head_norm_rope(q, qng_ref[...], rope)
    o_ref[...] = (q * (MLA_QK ** -0.5)).astype(o_ref.dtype)


def mla_q(z_mla, qa_g, wq_pad, qn_g_pad, rope, n, tm):
    t = z_mla.shape[0]
    per = n // tm
    in_specs = [pl.BlockSpec((tm, 512), lambda i: (i, MLA_Q_BLOCK)), _const_spec((1, MLA_Q_RANK)),
                _const_spec((MLA_Q_RANK, MLA_W)), _const_spec((1, LANE))]
    args = [z_mla, qa_g.reshape(1, -1), wq_pad, qn_g_pad]
    if rope is not None:
        in_specs += [pl.BlockSpec((tm, LANE), lambda i: (i % per, 0))] * 3
        args += list(rope)
    return pl.pallas_call(
        lambda *refs: _mla_q_body(rope is not None, *refs),
        grid=(t // tm,),
        in_specs=in_specs,
        out_specs=pl.BlockSpec((tm, MLA_W), lambda i: (i, 0)),
        out_shape=jax.ShapeDtypeStruct((t, MLA_W), BF16),
        compiler_params=_cparams("parallel"),
        name="mla_q",
    )(*args)


def _mla_kv_body(norm_ckv, use_rope, c_ref, kr_ref, kvag_ref, wk_ref, wv_ref, kng_ref, *rest):
    *rope_refs, cn_ref, k_ref, v_ref = rest
    c = c_ref[...]
    if norm_ckv:
        c = c * lax.rsqrt(jnp.mean(c * c, axis=-1, keepdims=True) + EPS) * kvag_ref[...]
    cn_ref[...] = c
    k = _mm(c, wk_ref[...]) + kr_ref[...]
    rope = tuple(r[...] for r in rope_refs) if use_rope else None
    k_ref[...] = _head_norm_rope(k, kng_ref[...], rope).astype(k_ref.dtype)
    v_ref[...] = _mm(c, wv_ref[...]).astype(v_ref.dtype)


def mla_kv(c_arr, c_block, kr_arr, kr_block, kva_g, wk_pad, wv, kn_g_pad, rope, norm_ckv, n, tm):
    t = c_arr.shape[0]
    per = n // tm
    in_specs = [pl.BlockSpec((tm, MLA_KV_RANK), lambda i: (i, c_block)),
                pl.BlockSpec((tm, MLA_W), lambda i: (i, kr_block)),
                _const_spec((1, MLA_KV_RANK)), _const_spec((MLA_KV_RANK, MLA_W)), _const_spec((MLA_KV_RANK, BR_WIDTH)),
                _const_spec((1, LANE))]
    args = [c_arr, kr_arr, kva_g.reshape(1, -1), wk_pad, wv, kn_g_pad]
    if rope is not None:
        in_specs += [pl.BlockSpec((tm, LANE), lambda i: (i % per, 0))] * 3
        args += list(rope)
    return pl.pallas_call(
        lambda *refs: _mla_kv_body(norm_ckv, rope is not None, *refs),
        grid=(t // tm,),
        in_specs=in_specs,
        out_specs=[pl.BlockSpec((tm, MLA_KV_RANK), lambda i: (i, 0)), pl.BlockSpec((tm, MLA_W), lambda i: (i, 0)),
                   pl.BlockSpec((tm, BR_WIDTH), lambda i: (i, 0))],
        out_shape=[jax.ShapeDtypeStruct((t, MLA_KV_RANK), F32), jax.ShapeDtypeStruct((t, MLA_W), BF16),
                   jax.ShapeDtypeStruct((t, BR_WIDTH), BF16)],
        compiler_params=_cparams("parallel"),
        name="mla_kv",
    )(*args)


def _attn_body(n_src, q_ref, gate_ref, *rest):
    kv_refs, o_ref = rest[:2 * n_src], rest[2 * n_src]
    tq = q_ref.shape[0]
    accs = []
    for hh in range(2):
        hs = slice(hh * LANE, (hh + 1) * LANE)
        q = q_ref[:, hs]

        def tile(carry, k, v):
            m, l, acc = carry
            s = lax.dot_general(q, k, (((1,), (1,)), ((), ())), preferred_element_type=F32)
            m_new = jnp.maximum(m, jnp.max(s, axis=-1, keepdims=True))
            alpha = jnp.exp(m - m_new)
            p = jnp.exp(s - m_new)
            l = alpha * l + jnp.sum(p, axis=-1, keepdims=True)
            acc = alpha * acc + jnp.dot(p.astype(BF16), v, preferred_element_type=F32)
            return m_new, l, acc

        carry = (jnp.full((tq, 1), NEG_INF, F32), jnp.zeros((tq, 1), F32), jnp.zeros((tq, LANE), F32))
        for s_i in range(n_src):
            k_ref, v_ref = kv_refs[2 * s_i], kv_refs[2 * s_i + 1]
            nk = k_ref.shape[1]
            tk = min(MLA_TK, nk)

            def step(j, carry, k_ref=k_ref, v_ref=v_ref, tk=tk):
                rows = pl.ds(pl.multiple_of(j * tk, tk), tk)
                return tile(carry, k_ref[0, rows, hs], v_ref[0, rows, :])

            carry = lax.fori_loop(0, nk // tk, step, carry) if nk > tk else step(0, carry)
        m, l, acc = carry
        accs.append(acc / l)
    o = jnp.where(_iota2((tq, LANE), 1) < MLA_V, accs[0], accs[1])
    o_ref[...] = (o * _silu(gate_ref[...])).astype(o_ref.dtype)


def mla_attention(q, z_mla, kvs, n, tq):
    t = q.shape[0]
    b = t // n
    per = n // tq
    in_specs = [pl.BlockSpec((tq, 2 * LANE), lambda bi, p, i: (bi * per + i, p)),
                pl.BlockSpec((tq, LANE), lambda bi, p, i: (bi * per + i, MLA_GATE_LANE_BLOCK + p))]
    args = [q, z_mla]
    for k, v in kvs:
        nk = k.shape[1]
        in_specs += [pl.BlockSpec((1, nk, 2 * LANE), lambda bi, p, i: (bi, 0, p)),
                     pl.BlockSpec((1, nk, LANE), lambda bi, p, i: (bi, 0, p))]
        args += [k, v]
    return pl.pallas_call(
        lambda *refs: _attn_body(len(kvs), *refs),
        grid=(b, PAIRS, per),
        in_specs=in_specs,
        out_specs=pl.BlockSpec((tq, LANE), lambda bi, p, i: (bi * per + i, p)),
        out_shape=jax.ShapeDtypeStruct((t, BR_WIDTH), BF16),
        compiler_params=_cparams("parallel", "parallel", "arbitrary"),
        name="mla_attention",
    )(*args)


def pad_heads(w, real, lead_zero=0):
    shp = w.shape[:-1]
    w = w.reshape(*shp, MLA_HEADS, real)
    w = jnp.pad(w, [(0, 0)] * len(shp) + [(0, 0), (lead_zero, LANE - real - lead_zero)])
    return w.reshape(*shp, MLA_W)


def pad_lane(g):
    return jnp.pad(g.reshape(1, -1), ((0, 0), (0, LANE - g.size)))


def place_rope_key(k_rope):
    return pad_heads(jnp.tile(k_rope, (1,) * (k_rope.ndim - 1) + (MLA_HEADS,)), MLA_ROPE, MLA_NOPE)


def pack_weights(P, l):
    w = P['w_in'][l]
    col = lambda i: w[:, IN_OFFS[i]:IN_OFFS[i + 1]]
    zeros = lambda n: jnp.zeros((D_MODEL, n), w.dtype)
    mla_kv = col(3)
    w_rw = jnp.concatenate([col(1), col(0)], axis=1)
    w_mla = jnp.concatenate([place_rope_key(mla_kv[:, MLA_KV_RANK:]), col(4), col(2), zeros(512 - MLA_Q_RANK),
                             mla_kv[:, :MLA_KV_RANK]], axis=1)
    w_ssd = jnp.concatenate([col(5), col(6), col(7), zeros(LANE - 2 * SSD_HEADS)], axis=1)
    w_lru = jnp.concatenate([col(8), col(9)], axis=1)
    kvu = P['mla_kv_up'][l].reshape(MLA_KV_RANK, MLA_HEADS, MLA_NOPE + MLA_V)
    bf = lambda x: x.astype(BF16)
    return dict(w_rw=bf(w_rw), w_mla=bf(w_mla), w_ssd=bf(w_ssd), w_lru=bf(w_lru), w_mrg=bf(col(10)),
                wq=bf(pad_heads(P['mla_q_up'][l], MLA_QK)),
                wk=bf(pad_heads(kvu[:, :, :MLA_NOPE].reshape(MLA_KV_RANK, -1), MLA_NOPE)),
                wv=bf(kvu[:, :, MLA_NOPE:].reshape(MLA_KV_RANK, -1)),
                w_branch=bf(P['w_branch'][l]), w_out=bf(P['w_out'][l]))


def trunk_layer(x, mod, P, W, l, ctx):
    b, n, d = x.shape
    t = b * n
    tm = min(n, 512)
    tq = 256
    shift, scale, gate = mod
    x2d = x.reshape(t, d)
    h = prenorm(x2d, P['norm_g'][l], scale, shift, n, tm)
    z_rw = matmul(h, W['w_rw'], tm, Z_RW_W, F32)
    z_mla = matmul(h, W['w_mla'], tm, Z_MLA_W, F32)
    z_ssd = matmul(h, W['w_ssd'], tm, Z_SSD_W, F32)
    z_lru = matmul(h, W['w_lru'], tm, Z_LRU_W, F32)
    logits = matmul(h, W['w_mrg'], tm, 2048, BF16)
    if ctx is None:
        rw0 = jnp.zeros((b, 2, PAIRS, LANE, LANE), F32)
        ssd0 = jnp.zeros((b, 2, PAIRS, LANE, LANE), F32)
        lru0 = jnp.zeros((b, 2, LRU_WIDTH), F32)
        rope = None
    else:
        ckv0, krope0, rw0, ssd0, lru0 = ctx
        rw0, ssd0 = rw_state_to_pairs(rw0), ssd_state_to_pairs(ssd0)
        rope = rope_tables(n)

    prep = rw_prep(z_rw.reshape(b, n, -1), P, l, tq)
    o_f, o_b, rw_fin = rw_scan(prep, rw0)
    flat = lambda a: a.reshape(t, -1)
    br_rw = rw_post(flat(o_f), flat(o_b), flat(prep[9]), z_rw, P['rw_ln_g'][l], P['rw_ln_b'][l], tm)

    qn_g, kn_g = pad_lane(P['mla_qn_g'][l]), pad_lane(P['mla_kn_g'][l])
    q = mla_q(z_mla, P['mla_qa_g'][l], W['wq'], qn_g, rope, n, tm)
    c_kv, k, v = mla_kv(z_mla, MLA_CKV_BLOCK, z_mla, 0, P['mla_kva_g'][l], W['wk'], W['wv'], kn_g, rope, True, n, tm)
    kvs = [(k.reshape(b, n, -1), v.reshape(b, n, -1))]
    if ctx is not None:
        past = ckv0.shape[1]
        _, k0, v0 = mla_kv(ckv0.reshape(b * past, -1), 0, place_rope_key(krope0).reshape(b * past, -1), 0,
                           P['mla_kva_g'][l], W['wk'], W['wv'], kn_g, None, False, past, past)
        kvs = [(k0.reshape(b, past, -1), v0.reshape(b, past, -1))] + kvs
    br_mla = mla_attention(q, z_mla, kvs, n, min(n, 256))
    k_rope = z_mla[:, MLA_NOPE:MLA_NOPE + MLA_ROPE]

    xbc, dtp = ssd_prep(z_ssd.reshape(b, n, -1), P['ssd_conv_w'][l], P['ssd_conv_b'][l], P['ssd_dt_bias'][l],
                        P['ssd_a_log'][l], tq)
    y_f, y_b, ssd_fin = ssd_scan(xbc, dtp, ssd0)
    br_ssd = ssd_post(flat(y_f), flat(y_b), flat(xbc), z_ssd, jnp.repeat(P['ssd_d'][l], SSD_HEAD_DIM),
                      P['ssd_norm_g'][l], tm)

    a0, u0, a1, u1 = lru_prep(z_lru.reshape(b, n, -1), P['lru_conv_w'][l], P['lru_conv_b'][l], P['lru_wa'][l],
                              P['lru_ba'][l], P['lru_wx'][l], P['lru_bx'][l], P['lru_lambda'][l], tq)
    h_f, h_b, lru_fin = lru_scan(a0, u0, a1, u1, lru0, tq)
    br_lru = lru_post(flat(h_f), flat(h_b), z_lru, tm)

    y = merge(x2d, [br_rw, br_mla, br_ssd, br_lru], logits, W['w_branch'], W['w_out'], gate, n, tm)
    states = (c_kv.reshape(b, n, -1), k_rope.reshape(b, n, -1), rw_pairs_to_state(rw_fin),
              ssd_pairs_to_state(ssd_fin), lru_fin)
    return y.reshape(b, n, d), states


def kernel(x_prompt, x_sample, cache_mla_ckv, cache_mla_krope, state_rwkv, state_ssd, state_lru, c, c_ctx, ada_w, ada_b, norm_g, w_in, rw_mu, rw_w0, rw_w2, rw_a0, rw_a2, rw_kk, rw_ka, rw_rk, rw_ln_g, rw_ln_b, mla_qa_g, mla_q_up, mla_kva_g, mla_kv_up, mla_qn_g, mla_kn_g, ssd_conv_w, ssd_conv_b, ssd_dt_bias, ssd_a_log, ssd_d, ssd_norm_g, lru_conv_w, lru_conv_b, lru_wa, lru_ba, lru_wx, lru_bx, lru_lambda, w_branch, w_out):
    P = dict(norm_g=norm_g, w_in=w_in,
             rw_mu=rw_mu, rw_w0=rw_w0, rw_w2=rw_w2, rw_a0=rw_a0, rw_a2=rw_a2, rw_kk=rw_kk,
             rw_ka=rw_ka, rw_rk=rw_rk.reshape(DEPTH, BR_WIDTH), rw_ln_g=rw_ln_g, rw_ln_b=rw_ln_b,
             mla_qa_g=mla_qa_g, mla_q_up=mla_q_up, mla_kva_g=mla_kva_g, mla_kv_up=mla_kv_up,
             mla_qn_g=mla_qn_g, mla_kn_g=mla_kn_g,
             ssd_conv_w=ssd_conv_w, ssd_conv_b=ssd_conv_b, ssd_dt_bias=ssd_dt_bias,
             ssd_a_log=ssd_a_log, ssd_d=ssd_d, ssd_norm_g=ssd_norm_g,
             lru_conv_w=lru_conv_w, lru_conv_b=lru_conv_b, lru_wa=lru_wa, lru_ba=lru_ba,
             lru_wx=lru_wx, lru_bx=lru_bx, lru_lambda=lru_lambda,
             w_branch=w_branch, w_out=w_out)
    dec_batch = c.shape[0]
    cond8 = jnp.concatenate([c_ctx[None, :], c, jnp.zeros((SUBLANE - 1 - dec_batch, D_MODEL), F32)], axis=0)
    weights, mods = [], []
    for l in range(DEPTH):
        weights.append(pack_weights(P, l))
        mod = ada_mod(cond8, ada_w[l], ada_b[l])
        mods.append(tuple(jnp.split(mod[:, None, :], 3, axis=-1)))

    y_prompt = x_prompt
    ctx_states = []
    for l in range(DEPTH):
        y_prompt, st = trunk_layer(y_prompt, tuple(m[0:1] for m in mods[l]), P, weights[l], l, None)
        ctx_states.append(st)
    new_states = tuple(jnp.stack([s[i] for s in ctx_states], axis=1) for i in range(5))

    y_sample = x_sample
    for l in range(DEPTH):
        ctx = (cache_mla_ckv[:, l], cache_mla_krope[:, l], state_rwkv[:, l], state_ssd[:, l], state_lru[:, l])
        y_sample, _ = trunk_layer(y_sample, tuple(m[1:1 + dec_batch] for m in mods[l]), P, weights[l], l, ctx)

    return (y_prompt, y_sample) + new_states
```

```python
import math

import jax
import jax.numpy as jnp
import numpy as np
from jax import lax
from jax.experimental import pallas as pl
from jax.experimental.pallas import tpu as pltpu

F32 = jnp.float32
BF16 = jnp.bfloat16
HIGHEST = lax.Precision.HIGHEST

D_MODEL = 1024
DEPTH = 2
GRID_W = 64
N_BRANCH = 4
BR_WIDTH = 512
EPS = 1e-6
CONV_K = 4

RW_HEADS = 8
RW_HEAD = 64
RW_LORA = 64
RW_GN_EPS = 64e-5
RW_PRE = 3 * BR_WIDTH + 2 * RW_LORA

MLA_HEADS = 8
MLA_NOPE = 64
MLA_ROPE = 32
MLA_QK = MLA_NOPE + MLA_ROPE
MLA_V = 64
MLA_Q_RANK = 384
MLA_KV_RANK = 256
ROPE_THETA = 10000.0

SSD_HEADS = 8
SSD_HEAD_DIM = 64
SSD_GROUPS = 2
SSD_STATE = 64
SSD_CONV_DIM = BR_WIDTH + 2 * SSD_GROUPS * SSD_STATE

LRU_WIDTH = 512
LRU_BLOCKS = 8
LRU_C = 8.0

IN_WIDTHS = (RW_PRE, BR_WIDTH, MLA_Q_RANK, MLA_KV_RANK + MLA_ROPE, BR_WIDTH, BR_WIDTH, SSD_CONV_DIM,
             2 * SSD_HEADS, LRU_WIDTH, LRU_WIDTH, N_BRANCH * D_MODEL)
IN_OFFS = tuple(int(v) for v in np.cumsum((0,) + IN_WIDTHS))

LANE = 128
SUBLANE = 8
VMEM_LIMIT = 48 * 1024 * 1024

RW_CHUNK = 64
PAIRS = BR_WIDTH // LANE
SSD_Q = 128
MLA_W = MLA_HEADS * LANE
MLA_TK = 4096
NEG_INF = -1e30

Z_RW_W = BR_WIDTH + RW_PRE
Z_MLA_W = MLA_W + BR_WIDTH + 512 + MLA_KV_RANK
Z_SSD_W = BR_WIDTH + SSD_CONV_DIM + LANE
Z_LRU_W = 2 * LRU_WIDTH
MLA_GATE_LANE_BLOCK = MLA_W // LANE
MLA_Q_BLOCK = (MLA_W + BR_WIDTH) // 512
MLA_CKV_BLOCK = (MLA_W + BR_WIDTH + 512) // MLA_KV_RANK


def _cparams(*sem):
    return pltpu.CompilerParams(dimension_semantics=sem, vmem_limit_bytes=VMEM_LIMIT)


def _const_spec(shape):
    return pl.BlockSpec(shape, lambda *_: (0,) * len(shape))


def _mm(a, b):
    return jnp.dot(a.astype(BF16), b.astype(BF16), preferred_element_type=F32)


def _mm_nt(a, b):
    return lax.dot_general(a.astype(BF16), b.astype(BF16), (((1,), (1,)), ((), ())), preferred_element_type=F32)


def _mm_f32(a, b):
    return jnp.dot(a, b, preferred_element_type=F32, precision=HIGHEST)


def _iota2(shape, dim):
    return lax.broadcasted_iota(jnp.int32, shape, dim)


def _silu(x):
    return x * jax.nn.sigmoid(x)


def _softplus(x):
    return jnp.maximum(x, 0.0) + jnp.log(1.0 + jnp.exp(-jnp.abs(x)))


def _head_sum(x, width):
    bd = ((_iota2((LANE, LANE), 0) // width) == (_iota2((LANE, LANE), 1) // width)).astype(BF16)
    hi = x.astype(BF16)
    lo = (x - hi.astype(F32)).astype(BF16)
    outs = []
    for j in range(x.shape[-1] // LANE):
        sl = slice(j * LANE, (j + 1) * LANE)
        outs.append(jnp.dot(hi[:, sl], bd, preferred_element_type=F32) + jnp.dot(lo[:, sl], bd, preferred_element_type=F32))
    return jnp.concatenate(outs, axis=1)


def _from_prev(u, s, prev8):
    rolled = pltpu.roll(u, s, 0)
    head = jnp.where(_iota2(prev8.shape, 0) < s, pltpu.roll(prev8, s, 0), rolled[:SUBLANE])
    return jnp.concatenate([head, rolled[SUBLANE:]], axis=0)


def _from_next(u, next8):
    n = u.shape[0]
    rolled = pltpu.roll(u, n - 1, 0)
    tail = jnp.where(_iota2(next8.shape, 0) == SUBLANE - 1, pltpu.roll(next8, SUBLANE - 1, 0), rolled[n - SUBLANE:])
    return jnp.concatenate([rolled[:n - SUBLANE], tail], axis=0)


def _centred_conv(x, prev8, next8, w_ref, b_ref):
    return (w_ref[0:1, :] * _from_prev(x, 2, prev8) + w_ref[1:2, :] * _from_prev(x, 1, prev8)
            + w_ref[2:3, :] * x + w_ref[3:4, :] * _from_next(x, next8) + b_ref[...])


def _halo_specs(tq, n, c):
    per8, nb8 = tq // SUBLANE, n // SUBLANE
    return [pl.BlockSpec((1, tq, c), lambda bi, i: (bi, i, 0)),
            pl.BlockSpec((1, SUBLANE, c), lambda bi, i: (bi, jnp.maximum(i * per8 - 1, 0), 0)),
            pl.BlockSpec((1, SUBLANE, c), lambda bi, i: (bi, jnp.minimum((i + 1) * per8, nb8 - 1), 0))]


def _halo_rows(zp_ref, zn_ref, cols):
    i = pl.program_id(1)
    prev8 = jnp.where(i > 0, zp_ref[0, :, cols], 0.0)
    next8 = jnp.where(i < pl.num_programs(1) - 1, zn_ref[0, :, cols], 0.0)
    return prev8, next8


def _ada_body(c_ref, w_ref, b_ref, o_ref):
    o_ref[...] = _mm(_silu(c_ref[...]), w_ref[...]) + b_ref[...]


def ada_mod(cond8, w, bias):
    d = cond8.shape[1]
    return pl.pallas_call(
        _ada_body,
        grid=(w.shape[1] // d,),
        in_specs=[_const_spec((SUBLANE, d)), pl.BlockSpec((d, d), lambda j: (0, j)), pl.BlockSpec((1, d), lambda j: (0, j))],
        out_specs=pl.BlockSpec((SUBLANE, d), lambda j: (0, j)),
        out_shape=jax.ShapeDtypeStruct((SUBLANE, w.shape[1]), F32),
        compiler_params=_cparams("parallel"),
        name="ada_mod",
    )(cond8, w, bias.reshape(1, -1))


def _prenorm_body(x_ref, g_ref, sc_ref, sh_ref, h_ref):
    x = x_ref[...]
    y = x * lax.rsqrt(jnp.mean(x * x, axis=-1, keepdims=True) + EPS)
    h_ref[...] = ((y * g_ref[...]) * (1.0 + sc_ref[0]) + sh_ref[0]).astype(h_ref.dtype)


def _mod_index(bm, per):
    return (lambda i: (i // per, 0, 0)) if bm > 1 else (lambda i: (0, 0, 0))


def prenorm(x2d, g, scale, shift, n, tm):
    t, d = x2d.shape
    mod_idx = _mod_index(scale.shape[0], n // tm)
    return pl.pallas_call(
        _prenorm_body,
        grid=(t // tm,),
        in_specs=[pl.BlockSpec((tm, d), lambda i: (i, 0)), _const_spec((1, d)),
                  pl.BlockSpec((1, 1, d), mod_idx), pl.BlockSpec((1, 1, d), mod_idx)],
        out_specs=pl.BlockSpec((tm, d), lambda i: (i, 0)),
        out_shape=jax.ShapeDtypeStruct((t, d), BF16),
        compiler_params=_cparams("parallel"),
        name="prenorm",
    )(x2d, g.reshape(1, -1), scale, shift)


def _mm_body(a_ref, w_ref, o_ref):
    o_ref[...] = jnp.dot(a_ref[...], w_ref[...], preferred_element_type=F32).astype(o_ref.dtype)


def matmul(a, w, tm, tn, out_dtype):
    t, k = a.shape
    n = w.shape[1]
    return pl.pallas_call(
        _mm_body,
        grid=(n // tn, t // tm),
        in_specs=[pl.BlockSpec((tm, k), lambda j, i: (i, 0)), pl.BlockSpec((k, tn), lambda j, i: (0, j))],
        out_specs=pl.BlockSpec((tm, tn), lambda j, i: (i, j)),
        out_shape=jax.ShapeDtypeStruct((t, n), out_dtype),
        compiler_params=_cparams("parallel", "parallel"),
        name="in_proj",
    )(a, w)


def _merge_body(x_ref, b0_ref, b1_ref, b2_ref, b3_ref, lg_ref, wb_ref, wo_ref, gate_ref, y_ref):
    merged = None
    for m, b_ref in enumerate((b0_ref, b1_ref, b2_ref, b3_ref)):
        proj = jnp.dot(b_ref[...], wb_ref[m], preferred_element_type=F32)
        g = jax.nn.sigmoid(lg_ref[:, m * D_MODEL:(m + 1) * D_MODEL].astype(F32))
        merged = g * proj if merged is None else merged + g * proj
    out = jnp.dot(merged.astype(BF16), wo_ref[...], preferred_element_type=F32)
    y_ref[...] = x_ref[...] + gate_ref[0] * out


def merge(x2d, branches, logits, wb, wo, gate, n, tm):
    t, d = x2d.shape
    mod_idx = _mod_index(gate.shape[0], n // tm)
    row = lambda i: (i, 0)
    return pl.pallas_call(
        _merge_body,
        grid=(t // tm,),
        in_specs=[pl.BlockSpec((tm, d), row)] + [pl.BlockSpec((tm, BR_WIDTH), row)] * N_BRANCH
        + [pl.BlockSpec((tm, N_BRANCH * d), row), _const_spec((N_BRANCH, BR_WIDTH, d)), _const_spec((d, d)),
           pl.BlockSpec((1, 1, d), mod_idx)],
        out_specs=pl.BlockSpec((tm, d), row),
        out_shape=jax.ShapeDtypeStruct((t, d), F32),
        compiler_params=_cparams("parallel"),
        name="merge",
    )(x2d, *branches, logits, wb, wo, gate)


def _rw_prep_body(z_ref, zp_ref, zn_ref, mu_ref, kkg_ref, ka_ref, rk_ref, w0_ref, w2_ref, a0_ref, a2_ref,
                  r_ref, kk_ref, v_ref, lw0_ref, k0_ref, b0_ref, lw1_ref, k1_ref, b1_ref, bonus_ref):
    cols = slice(BR_WIDTH, Z_RW_W)
    prev8, next8 = _halo_rows(zp_ref, zn_ref, cols)
    u = z_ref[0, :, cols]
    u = u + mu_ref[...] * (0.5 * (_from_prev(u, 1, prev8) + _from_next(u, next8)) - u)
    r = u[:, 0:BR_WIDTH]
    k = u[:, BR_WIDTH:2 * BR_WIDTH]
    v = u[:, 2 * BR_WIDTH:3 * BR_WIDTH]
    lo = u[:, 3 * BR_WIDTH:]
    lo = jnp.where(_iota2(lo.shape, 1) < RW_LORA, jnp.tanh(lo), lo)
    kk = k * kkg_ref[...]
    kk = kk * lax.rsqrt(_head_sum(kk * kk, RW_HEAD) + 1e-12)
    r_ref[0] = r
    kk_ref[0] = kk
    v_ref[0] = v
    coef = jnp.zeros_like(r)
    for d, (lw_ref, k_ref, b_ref) in enumerate(((lw0_ref, k0_ref, b0_ref), (lw1_ref, k1_ref, b1_ref))):
        w_raw = w0_ref[d] + _mm_f32(lo, w2_ref[d])
        lw_ref[0] = -math.exp(-0.5) * jax.nn.sigmoid(w_raw)
        a = jax.nn.sigmoid(a0_ref[d] + _mm_f32(lo, a2_ref[d]))
        k_d = k * (1.0 + (a - 1.0) * ka_ref[...])
        k_ref[0] = k_d
        b_ref[0] = kk * a
        coef = coef + _head_sum(r * k_d * rk_ref[...], RW_HEAD)
    bonus_ref[0] = coef * v


def rw_prep(z_rw, P, l, tq):
    b, n, c = z_rw.shape
    w2 = jnp.concatenate([P['rw_w2'][l], jnp.zeros_like(P['rw_w2'][l])], axis=1)
    a2 = jnp.concatenate([jnp.zeros_like(P['rw_a2'][l]), P['rw_a2'][l]], axis=1)
    row = lambda x: x.reshape(1, -1)
    tile = pl.BlockSpec((1, tq, BR_WIDTH), lambda bi, i: (bi, i, 0))
    out = jax.ShapeDtypeStruct((b, n, BR_WIDTH), F32)
    return pl.pallas_call(
        _rw_prep_body,
        grid=(b, n // tq),
        in_specs=_halo_specs(tq, n, c) + [_const_spec((1, RW_PRE))] + [_const_spec((1, BR_WIDTH))] * 3
        + [_const_spec((2, 1, BR_WIDTH)), _const_spec((2, LANE, BR_WIDTH))] * 2,
        out_specs=[tile] * 10,
        out_shape=[out] * 10,
        compiler_params=_cparams("parallel", "parallel"),
        name="rw_prep",
    )(z_rw, z_rw, z_rw, row(P['rw_mu'][l]), row(P['rw_kk'][l]), row(P['rw_ka'][l]), row(P['rw_rk'][l]),
      P['rw_w0'][l][:, None, :], w2, P['rw_a0'][l][:, None, :], a2)


def _rw_masks(rev):
    n = 2 * RW_CHUNK
    ri, ci = _iota2((n, n), 0), _iota2((n, n), 1)
    same = (ri // RW_CHUNK) == (ci // RW_CHUNK)
    strict = same & ((ci > ri) if rev else (ci < ri))
    incl = same & ((ci >= ri) if rev else (ci <= ri))
    blk8 = strict & ((ri // 8) == (ci // 8))
    levels = [strict & ((ri // (2 * s)) == (ci // (2 * s))) & ((ri // s) != (ci // s)) for s in (8, 16, 32)]
    eye = (ri == ci).astype(F32)
    t_i, s_i = _iota2((RW_CHUNK, RW_CHUNK), 0), _iota2((RW_CHUNK, RW_CHUNK), 1)
    tri = ((s_i >= t_i) if rev else (s_i <= t_i)).astype(F32)
    m0 = (_iota2((RW_CHUNK, LANE), 1) < RW_HEAD).astype(F32)
    return dict(strict=strict, incl=incl, blk8=blk8, levels=levels, eye=eye, tri=tri, m0=m0, m1=1.0 - m0)


def _rw_chunks(chains):
    n = 2 * RW_CHUNK
    each = lambda f, *cols: [f(*a) for a in zip(*cols)]
    revs, mks = [c[0] for c in chains], [c[1] for c in chains]
    lws, rs_, kks, vs_, ks_, bs_, ss = ([c[i] for c in chains] for i in range(2, 9))
    stack = lambda mk, x: jnp.concatenate([x * mk['m0'], x * mk['m1']], axis=0)
    cums = each(lambda mk, lw: _mm_f32(mk['tri'], lw), mks, lws)
    totals = each(lambda rev, cum: cum[0:1] if rev else cum[RW_CHUNK - 1:RW_CHUNK], revs, cums)
    e_negs = each(lambda cum: jnp.exp(-cum), cums)
    e_tots = each(lambda cum, tot: jnp.exp(tot - cum), cums, totals)
    rs = each(lambda mk, r, cum: stack(mk, r * jnp.exp(cum)), mks, rs_, cums)
    ks = each(lambda mk, kk, cum, lw: stack(mk, kk * jnp.exp(cum - lw)), mks, kks, cums, lws)
    vs = each(stack, mks, vs_)
    both = lambda mk, b, k, e: jnp.concatenate([stack(mk, b * e), stack(mk, k * e)], axis=0)
    rhs = each(both, mks, bs_, ks_, e_negs)
    hat = each(both, mks, bs_, ks_, e_tots)
    g1 = each(_mm_nt, ks, rhs)
    g2 = each(_mm_nt, rs, rhs)
    a_b = each(lambda mk, g: jnp.where(mk['strict'], g[:, :n], 0.0), mks, g1)
    a_k = each(lambda mk, g: jnp.where(mk['strict'], g[:, n:], 0.0), mks, g1)
    m_b = each(lambda mk, g: jnp.where(mk['incl'], g[:, :n], 0.0), mks, g2)
    m_k = each(lambda mk, g: jnp.where(mk['incl'], g[:, n:], 0.0), mks, g2)
    x = each(lambda mk, a: -jnp.where(mk['blk8'], a, 0.0), mks, a_b)
    x2 = each(_mm, x, x)
    akv = each(_mm, a_k, vs)
    t_inv = each(lambda mk, x: mk['eye'] + x, mks, x)
    x4 = each(_mm, x2, x2)
    t_inv = each(lambda t, x2: t + _mm(t, x2), t_inv, x2)
    t_inv = each(lambda t, x4: t + _mm(t, x4), t_inv, x4)
    for li in range(3):
        tl = each(lambda mk, t, a: _mm(t, jnp.where(mk['levels'][li], a, 0.0)), mks, t_inv, a_b)
        t_inv = each(lambda t, tl: t - _mm(tl, t), t_inv, tl)
    tx = each(lambda t, akv, ks: _mm(t, jnp.concatenate([akv, ks], axis=1)), t_inv, akv, ks)
    u2 = each(lambda tx: -tx[:, :LANE], tx)
    w1 = each(lambda tx: -tx[:, LANE:], tx)
    u = each(lambda w1, s, u2: _mm_nt(w1, s) + u2, w1, ss, u2)
    q_eff = each(lambda rs, m_b, w1: rs + _mm(m_b, w1), rs, m_b, w1)
    o_intra = each(lambda m_b, m_k, u2, vs: _mm(jnp.concatenate([m_b, m_k], axis=1), jnp.concatenate([u2, vs], axis=0)),
                   m_b, m_k, u2, vs)
    s_new = each(lambda s, tot, u, vs, hat: s * jnp.exp(tot) + _mm(jnp.concatenate([u, vs], axis=0).T, hat),
                 ss, totals, u, vs, hat)
    o_s = each(lambda q_eff, s, o_intra: _mm_nt(q_eff, s) + o_intra, q_eff, ss, o_intra)
    return each(lambda o_s: o_s[:RW_CHUNK] + o_s[RW_CHUNK:], o_s), s_new


def _rw_scan_body(rf_ref, kkf_ref, vf_ref, lwf_ref, kf_ref, bf_ref, rb_ref, kkb_ref, vb_ref, lwb_ref, kb_ref, bb_ref,
                  s0_ref, of_ref, ob_ref, sfin_ref, s_scr):
    c = pl.program_id(1)

    @pl.when(c == 0)
    def _():
        s_scr[...] = s0_ref[0]

    chains, dests = [], []
    for d, refs, o_ref in ((0, (lwf_ref, rf_ref, kkf_ref, vf_ref, kf_ref, bf_ref), of_ref),
                           (1, (lwb_ref, rb_ref, kkb_ref, vb_ref, kb_ref, bb_ref), ob_ref)):
        mk = _rw_masks(rev=(d == 1))
        for p in range(PAIRS):
            sl = slice(p * LANE, (p + 1) * LANE)
            chains.append((d == 1, mk) + tuple(ref[0, :, sl] for ref in refs) + (s_scr[d, p],))
            dests.append((o_ref, sl, d, p))
    outs, states = _rw_chunks(chains)
    for (o_ref, sl, d, p), o, s_new in zip(dests, outs, states):
        o_ref[0, :, sl] = o
        s_scr[d, p] = s_new

    @pl.when(c == pl.num_programs(1) - 1)
    def _():
        sfin_ref[0] = s_scr[...]


def rw_scan(prep, s0_pairs):
    r, kk, v, lw0, k0, b0, lw1, k1, b1, _ = prep
    b, n, _ = r.shape
    nc = n // RW_CHUNK
    fwd = pl.BlockSpec((1, RW_CHUNK, BR_WIDTH), lambda bi, c: (bi, c, 0))
    bwd = pl.BlockSpec((1, RW_CHUNK, BR_WIDTH), lambda bi, c: (bi, nc - 1 - c, 0))
    st = pl.BlockSpec((1, 2, PAIRS, LANE, LANE), lambda bi, c: (bi, 0, 0, 0, 0))
    return pl.pallas_call(
        _rw_scan_body,
        grid=(b, nc),
        in_specs=[fwd] * 6 + [bwd] * 6 + [st],
        out_specs=[fwd, bwd, st],
        out_shape=[jax.ShapeDtypeStruct((b, n, BR_WIDTH), F32)] * 2
        + [jax.ShapeDtypeStruct((b, 2, PAIRS, LANE, LANE), F32)],
        scratch_shapes=[pltpu.VMEM((2, PAIRS, LANE, LANE), F32)],
        compiler_params=_cparams("parallel", "arbitrary"),
        name="rw_scan",
    )(r, kk, v, lw0, k0, b0, r, kk, v, lw1, k1, b1, s0_pairs)


def rw_state_to_pairs(s0):
    b = s0.shape[0]
    s = s0.reshape(b, 2, PAIRS, 2, RW_HEAD, RW_HEAD)
    z = jnp.zeros_like(s[:, :, :, 0])
    top = jnp.concatenate([s[:, :, :, 0], z], axis=-1)
    bot = jnp.concatenate([z, s[:, :, :, 1]], axis=-1)
    return jnp.concatenate([top, bot], axis=-2)


def rw_pairs_to_state(sp):
    b = sp.shape[0]
    h0 = sp[:, :, :, :RW_HEAD, :RW_HEAD]
    h1 = sp[:, :, :, RW_HEAD:, RW_HEAD:]
    return jnp.stack([h0, h1], axis=3).reshape(b, 2, RW_HEADS, RW_HEAD, RW_HEAD)


def _rw_post_body(of_ref, ob_ref, bonus_ref, gate_ref, g_ref, bias_ref, o_ref):
    wkv = of_ref[...] + ob_ref[...]
    mu = _head_sum(wkv, RW_HEAD) * (1.0 / RW_HEAD)
    xc = wkv - mu
    var = _head_sum(xc * xc, RW_HEAD) * (1.0 / RW_HEAD)
    o = xc * lax.rsqrt(var + RW_GN_EPS) * g_ref[...] + bias_ref[...] + bonus_ref[...]
    o_ref[...] = (o * _silu(gate_ref[...])).astype(o_ref.dtype)


def rw_post(o_f, o_b, bonus, z_rw2d, ln_g, ln_b, tm):
    t = o_f.shape[0]
    tile = pl.BlockSpec((tm, BR_WIDTH), lambda i: (i, 0))
    return pl.pallas_call(
        _rw_post_body,
        grid=(t // tm,),
        in_specs=[tile] * 4 + [_const_spec((1, BR_WIDTH))] * 2,
        out_specs=tile,
        out_shape=jax.ShapeDtypeStruct((t, BR_WIDTH), BF16),
        compiler_params=_cparams("parallel"),
        name="rw_post",
    )(o_f, o_b, bonus, z_rw2d, ln_g.reshape(1, -1), ln_b.reshape(1, -1))


def _ssd_prep_body(z_ref, zp_ref, zn_ref, w_ref, b_ref, dtb_ref, nega_ref, xbc_ref, dtp_ref):
    cols = slice(BR_WIDTH, BR_WIDTH + SSD_CONV_DIM)
    prev8, next8 = _halo_rows(zp_ref, zn_ref, cols)
    xbc_ref[0] = _silu(_centred_conv(z_ref[0, :, cols], prev8, next8, w_ref, b_ref))
    dt = _softplus(z_ref[0, :, BR_WIDTH + SSD_CONV_DIM:] + dtb_ref[...])
    dta = pltpu.roll(dt, 2 * SSD_HEADS, 1) * nega_ref[...]
    lane = _iota2(dt.shape, 1)
    dtp_ref[0] = jnp.where(lane < 2 * SSD_HEADS, dt, jnp.where(lane < 4 * SSD_HEADS, dta, 0.0))


def ssd_prep(z_ssd, conv_w, conv_b, dt_bias, a_log, tq):
    b, n, c = z_ssd.shape
    dtb = jnp.pad(dt_bias.reshape(1, -1), ((0, 0), (0, LANE - 2 * SSD_HEADS)))
    nega = jnp.pad(-jnp.exp(a_log).reshape(1, -1), ((0, 0), (2 * SSD_HEADS, LANE - 4 * SSD_HEADS)))
    return pl.pallas_call(
        _ssd_prep_body,
        grid=(b, n // tq),
        in_specs=_halo_specs(tq, n, c) + [_const_spec((CONV_K, SSD_CONV_DIM)), _const_spec((1, SSD_CONV_DIM)),
                                           _const_spec((1, LANE)), _const_spec((1, LANE))],
        out_specs=[pl.BlockSpec((1, tq, SSD_CONV_DIM), lambda bi, i: (bi, i, 0)),
                   pl.BlockSpec((1, tq, LANE), lambda bi, i: (bi, i, 0))],
        out_shape=[jax.ShapeDtypeStruct((b, n, SSD_CONV_DIM), F32), jax.ShapeDtypeStruct((b, n, LANE), F32)],
        compiler_params=_cparams("parallel", "parallel"),
        name="ssd_prep",
    )(z_ssd, z_ssd, z_ssd, conv_w, conv_b.reshape(1, -1), dtb, nega)


def _ssd_chunk(rev, d, xbc, dtp, st_ref):
    q = SSD_Q
    li, si = _iota2((q, q), 0), _iota2((q, q), 1)
    incl = (si >= li) if rev else (si <= li)
    tri = incl.astype(F32)
    head0 = _iota2((q, LANE), 1) < SSD_HEAD_DIM
    m0 = head0.astype(F32)
    m1 = 1.0 - m0
    dt_lane, a_lane = d * SSD_HEADS, (2 + d) * SSD_HEADS

    def per_head_lanes(arr, lane0):
        cols = [arr[:, lane0 + h:lane0 + h + 1] for h in range(SSD_HEADS)]
        return jnp.concatenate([jnp.where(head0, cols[2 * p], cols[2 * p + 1]) for p in range(PAIRS)], axis=1)

    cum = _mm_f32(tri, dtp)
    acs_rows = cum.T[a_lane:a_lane + SSD_HEADS, :]
    acs = per_head_lanes(cum, a_lane)
    a_tot = acs[0:1] if rev else acs[q - 1:q]
    x, bm, cm = xbc[:, :BR_WIDTH], xbc[:, BR_WIDTH:BR_WIDTH + LANE], xbc[:, BR_WIDTH + LANE:]
    xd = x * per_head_lanes(dtp, dt_lane)
    e_acs = jnp.exp(acs)
    xdec = xd * jnp.exp(a_tot - acs)
    e_tot = jnp.exp(a_tot)
    ys = []
    for g in range(SSD_GROUPS):
        mg = m0 if g == 0 else m1
        cb = _mm_nt(cm * mg, bm)
        bm_t = (bm * mg).T
        for p in range(2 * g, 2 * g + 2):
            sl = slice(p * LANE, (p + 1) * LANE)
            ms = []
            for hh in range(2):
                h = 2 * p + hh
                col = cum[:, a_lane + h:a_lane + h + 1]
                ms.append(jnp.where(incl, cb * jnp.exp(jnp.minimum(col - acs_rows[h:h + 1, :], 0.0)), 0.0))
            xs = xd[:, sl]
            y = _mm(jnp.concatenate(ms, axis=1), jnp.concatenate([xs * m0, xs * m1], axis=0))
            y = y + _mm(cm, st_ref[p]) * e_acs[:, sl]
            st_ref[p] = st_ref[p] * e_tot[:, sl] + _mm(bm_t, xdec[:, sl])
            ys.append(y)
    return jnp.concatenate(ys, axis=1)


def _ssd_scan_body(xf_ref, dtf_ref, xb_ref, dtb_ref, s0_ref, yf_ref, yb_ref, sfin_ref, st_scr):
    c = pl.program_id(1)

    @pl.when(c == 0)
    def _():
        for d in range(2):
            for p in range(PAIRS):
                st_scr[d, p] = s0_ref[0, d, p].T

    yf_ref[0] = _ssd_chunk(False, 0, xf_ref[0], dtf_ref[0], st_scr.at[0])
    yb_ref[0] = _ssd_chunk(True, 1, xb_ref[0], dtb_ref[0], st_scr.at[1])

    @pl.when(c == pl.num_programs(1) - 1)
    def _():
        for d in range(2):
            for p in range(PAIRS):
                sfin_ref[0, d, p] = st_scr[d, p].T


def ssd_scan(xbc, dtp, s0_pairs):
    b, n, _ = xbc.shape
    nc = n // SSD_Q
    fw = lambda w: pl.BlockSpec((1, SSD_Q, w), lambda bi, c: (bi, c, 0))
    bw = lambda w: pl.BlockSpec((1, SSD_Q, w), lambda bi, c: (bi, nc - 1 - c, 0))
    st = pl.BlockSpec((1, 2, PAIRS, LANE, LANE), lambda bi, c: (bi, 0, 0, 0, 0))
    return pl.pallas_call(
        _ssd_scan_body,
        grid=(b, nc),
        in_specs=[fw(SSD_CONV_DIM), fw(LANE), bw(SSD_CONV_DIM), bw(LANE), st],
        out_specs=[fw(BR_WIDTH), bw(BR_WIDTH), st],
        out_shape=[jax.ShapeDtypeStruct((b, n, BR_WIDTH), F32)] * 2
        + [jax.ShapeDtypeStruct((b, 2, PAIRS, LANE, LANE), F32)],
        scratch_shapes=[pltpu.VMEM((2, PAIRS, LANE, LANE), F32)],
        compiler_params=_cparams("parallel", "arbitrary"),
        name="ssd_scan",
    )(xbc, dtp, xbc, dtp, s0_pairs)


def ssd_state_to_pairs(h0):
    b = h0.shape[0]
    s = h0.reshape(b, 2, PAIRS, LANE, SSD_STATE)
    lo = jnp.pad(s, ((0, 0),) * 4 + ((0, SSD_STATE),))
    hi = jnp.pad(s, ((0, 0),) * 4 + ((SSD_STATE, 0),))
    return jnp.where(_pair_in_group0(), lo, hi)


def _pair_in_group0():
    return (jnp.arange(PAIRS) < PAIRS // SSD_GROUPS)[None, None, :, None, None]


def ssd_pairs_to_state(sp):
    b = sp.shape[0]
    s = jnp.where(_pair_in_group0(), sp[..., :SSD_STATE], sp[..., SSD_STATE:])
    return s.reshape(b, 2, SSD_HEADS, SSD_HEAD_DIM, SSD_STATE)


def _ssd_post_body(yf_ref, yb_ref, x_ref, gate_ref, d_ref, g_ref, o_ref):
    y = (x_ref[...] * d_ref[...] + yf_ref[...] + yb_ref[...]) * _silu(gate_ref[...])
    y = y * lax.rsqrt(jnp.mean(y * y, axis=-1, keepdims=True) + EPS)
    o_ref[...] = (y * g_ref[...]).astype(o_ref.dtype)


def ssd_post(y_f, y_b, xbc2d, z_ssd2d, d_full, norm_g, tm):
    t = y_f.shape[0]
    tile = pl.BlockSpec((tm, BR_WIDTH), lambda i: (i, 0))
    return pl.pallas_call(
        _ssd_post_body,
        grid=(t // tm,),
        in_specs=[tile] * 4 + [_const_spec((1, BR_WIDTH))] * 2,
        out_specs=tile,
        out_shape=jax.ShapeDtypeStruct((t, BR_WIDTH), BF16),
        compiler_params=_cparams("parallel"),
        name="ssd_post",
    )(y_f, y_b, xbc2d, z_ssd2d, d_full.reshape(1, -1), norm_g.reshape(1, -1))


def _lru_prep_body(z_ref, zp_ref, zn_ref, w_ref, b_ref, wa_ref, ba_ref, wx_ref, bx_ref, sp_ref,
                   a0_ref, u0_ref, a1_ref, u1_ref):
    cols = slice(0, LRU_WIDTH)
    prev8, next8 = _halo_rows(zp_ref, zn_ref, cols)
    xc = _centred_conv(z_ref[0, :, cols], prev8, next8, w_ref, b_ref)
    for d, (a_ref, u_ref) in enumerate(((a0_ref, u0_ref), (a1_ref, u1_ref))):
        g_r = jax.nn.sigmoid(_mm(xc, wa_ref[d]) + ba_ref[d])
        g_i = jax.nn.sigmoid(_mm(xc, wx_ref[d]) + bx_ref[d])
        log_a = -LRU_C * g_r * sp_ref[d]
        t = jnp.tanh(log_a)
        a_ref[0] = jnp.exp(log_a)
        u_ref[0] = jnp.sqrt(-2.0 * t / (1.0 - t)) * (g_i * xc)


def _block_diag(w):
    d, k, bi, bo = w.shape
    return jnp.einsum('dkij,kl->dkilj', w, jnp.eye(k, dtype=w.dtype)).reshape(d, k * bi, k * bo)


def lru_prep(z_lru, conv_w, conv_b, wa, ba, wx, bx, lam, tq):
    b, n, c = z_lru.shape
    tile = pl.BlockSpec((1, tq, LRU_WIDTH), lambda bi, i: (bi, i, 0))
    out = jax.ShapeDtypeStruct((b, n, LRU_WIDTH), F32)
    sp = _softplus(-lam)[:, None, :]
    return pl.pallas_call(
        _lru_prep_body,
        grid=(b, n // tq),
        in_specs=_halo_specs(tq, n, c) + [_const_spec((CONV_K, LRU_WIDTH)), _const_spec((1, LRU_WIDTH)),
                                           _const_spec((2, LRU_WIDTH, LRU_WIDTH)), _const_spec((2, 1, LRU_WIDTH)),
                                           _const_spec((2, LRU_WIDTH, LRU_WIDTH)), _const_spec((2, 1, LRU_WIDTH)),
                                           _const_spec((2, 1, LRU_WIDTH))],
        out_specs=[tile] * 4,
        out_shape=[out] * 4,
        compiler_params=_cparams("parallel", "parallel"),
        name="lru_prep",
    )(z_lru, z_lru, z_lru, conv_w, conv_b.reshape(1, -1), _block_diag(wa).astype(BF16), ba[:, None, :],
      _block_diag(wx).astype(BF16), bx[:, None, :], sp)


def _lru_tile_scan(rev, a, u, carry):
    n = a.shape[0]
    row = _iota2(a.shape, 0)
    s = 1
    while s < n:
        shift = n - s if rev else s
        keep = (row < n - s) if rev else (row >= s)
        a_sh = jnp.where(keep, pltpu.roll(a, shift, 0), 1.0)
        u_sh = jnp.where(keep, pltpu.roll(u, shift, 0), 0.0)
        u = u + a * u_sh
        a = a * a_sh
        s *= 2
    h = a * carry + u
    return h, (h[0:1] if rev else h[n - 1:n])


def _lru_scan_body(af_ref, uf_ref, ab_ref, ub_ref, h0_ref, yf_ref, yb_ref, hfin_ref, h_scr):
    c = pl.program_id(1)

    @pl.when(c == 0)
    def _():
        h_scr[...] = h0_ref[0]

    hf, cf = _lru_tile_scan(False, af_ref[0], uf_ref[0], h_scr[0:1])
    yf_ref[0] = hf
    h_scr[0:1] = cf
    hb, cb = _lru_tile_scan(True, ab_ref[0], ub_ref[0], h_scr[1:2])
    yb_ref[0] = hb
    h_scr[1:2] = cb

    @pl.when(c == pl.num_programs(1) - 1)
    def _():
        hfin_ref[0] = h_scr[...]


def lru_scan(a0, u0, a1, u1, h0, tq):
    b, n, _ = a0.shape
    nc = n // tq
    fw = pl.BlockSpec((1, tq, LRU_WIDTH), lambda bi, c: (bi, c, 0))
    bw = pl.BlockSpec((1, tq, LRU_WIDTH), lambda bi, c: (bi, nc - 1 - c, 0))
    st = pl.BlockSpec((1, 2, LRU_WIDTH), lambda bi, c: (bi, 0, 0))
    return pl.pallas_call(
        _lru_scan_body,
        grid=(b, nc),
        in_specs=[fw, fw, bw, bw, st],
        out_specs=[fw, bw, st],
        out_shape=[jax.ShapeDtypeStruct((b, n, LRU_WIDTH), F32)] * 2 + [jax.ShapeDtypeStruct((b, 2, LRU_WIDTH), F32)],
        scratch_shapes=[pltpu.VMEM((2, LRU_WIDTH), F32)],
        compiler_params=_cparams("parallel", "arbitrary"),
        name="lru_scan",
    )(a0, u0, a1, u1, h0)


def _lru_post_body(yf_ref, yb_ref, gate_ref, o_ref):
    o_ref[...] = ((yf_ref[...] + yb_ref[...]) * _silu(gate_ref[...])).astype(o_ref.dtype)


def lru_post(y_f, y_b, z_lru2d, tm):
    t = y_f.shape[0]
    tile = pl.BlockSpec((tm, LRU_WIDTH), lambda i: (i, 0))
    return pl.pallas_call(
        _lru_post_body,
        grid=(t // tm,),
        in_specs=[tile, tile, pl.BlockSpec((tm, LRU_WIDTH), lambda i: (i, 1))],
        out_specs=tile,
        out_shape=jax.ShapeDtypeStruct((t, LRU_WIDTH), BF16),
        compiler_params=_cparams("parallel"),
        name="lru_post",
    )(y_f, y_b, z_lru2d)


def rope_tables(n):
    t = np.arange(n)
    pos = jnp.asarray(np.stack([t // GRID_W, t % GRID_W], axis=0).astype(np.float32))
    nf = MLA_ROPE // 4
    inv = jnp.asarray(ROPE_THETA, F32) ** (-jnp.arange(nf, dtype=F32) / nf)
    rel = np.arange(LANE) - MLA_NOPE
    in_rope = (rel >= 0) & (rel < MLA_ROPE)
    half = np.clip(rel // (2 * nf), 0, 1)
    within = rel % (2 * nf)
    is_x2 = within >= nf
    ang = pos[half, :].T * inv[within % nf][None, :]
    cos = jnp.where(in_rope[None, :], jnp.cos(ang), 1.0)
    sin = jnp.sin(ang)
    sin_a = jnp.where((in_rope & ~is_x2)[None, :], -sin, 0.0)
    sin_b = jnp.where((in_rope & is_x2)[None, :], sin, 0.0)
    return cos, sin_a, sin_b


def _head_norm_rope(x, g, rope):
    nf = MLA_ROPE // 4
    outs = []
    for h in range(MLA_HEADS):
        xh = x[:, h * LANE:(h + 1) * LANE]
        ms = jnp.sum(xh * xh, axis=-1, keepdims=True) * (1.0 / MLA_QK)
        xh = xh * lax.rsqrt(ms + EPS) * g
        if rope is not None:
            cos, sin_a, sin_b = rope
            xh = xh * cos + pltpu.roll(xh, LANE - nf, 1) * sin_a + pltpu.roll(xh, nf, 1) * sin_b
        outs.append(xh)
    return jnp.concatenate(outs, axis=1)


def _mla_q_body(use_rope, q_ref, qag_ref, wq_ref, qng_ref, *rest):
    *rope_refs, o_ref = rest
    x = q_ref[:, :MLA_Q_RANK]
    x = x * lax.rsqrt(jnp.mean(x * x, axis=-1, keepdims=True) + EPS) * qag_ref[...]
    q = _mm(x, wq_ref[...])
    rope = tuple(r[...] for r in rope_refs) if use_rope else None
    q = _head_norm_rope(q, qng_ref[...], rope)
    o_ref[...] = (q * (MLA_QK ** -0.5)).astype(o_ref.dtype)


def mla_q(z_mla, qa_g, wq_pad, qn_g_pad, rope, n, tm):
    t = z_mla.shape[0]
    per = n // tm
    in_specs = [pl.BlockSpec((tm, 512), lambda i: (i, MLA_Q_BLOCK)), _const_spec((1, MLA_Q_RANK)),
                _const_spec((MLA_Q_RANK, MLA_W)), _const_spec((1, LANE))]
    args = [z_mla, qa_g.reshape(1, -1), wq_pad, qn_g_pad]
    if rope is not None:
        in_specs += [pl.BlockSpec((tm, LANE), lambda i: (i % per, 0))] * 3
        args += list(rope)
    return pl.pallas_call(
        lambda *refs: _mla_q_body(rope is not None, *refs),
        grid=(t // tm,),
        in_specs=in_specs,
        out_specs=pl.BlockSpec((tm, MLA_W), lambda i: (i, 0)),
        out_shape=jax.ShapeDtypeStruct((t, MLA_W), BF16),
        compiler_params=_cparams("parallel"),
        name="mla_q",
    )(*args)


def _mla_kv_body(norm_ckv, use_rope, c_ref, kr_ref, kvag_ref, wk_ref, wv_ref, kng_ref, *rest):
    *rope_refs, cn_ref, k_ref, v_ref = rest
    c = c_ref[...]
    if norm_ckv:
        c = c * lax.rsqrt(jnp.mean(c * c, axis=-1, keepdims=True) + EPS) * kvag_ref[...]
    cn_ref[...] = c
    k = _mm(c, wk_ref[...]) + kr_ref[...]
    rope = tuple(r[...] for r in rope_refs) if use_rope else None
    k_ref[...] = _head_norm_rope(k, kng_ref[...], rope).astype(k_ref.dtype)
    v_ref[...] = _mm(c, wv_ref[...]).astype(v_ref.dtype)


def mla_kv(c_arr, c_block, kr_arr, kr_block, kva_g, wk_pad, wv, kn_g_pad, rope, norm_ckv, n, tm):
    t = c_arr.shape[0]
    per = n // tm
    in_specs = [pl.BlockSpec((tm, MLA_KV_RANK), lambda i: (i, c_block)),
                pl.BlockSpec((tm, MLA_W), lambda i: (i, kr_block)),
                _const_spec((1, MLA_KV_RANK)), _const_spec((MLA_KV_RANK, MLA_W)), _const_spec((MLA_KV_RANK, BR_WIDTH)),
                _const_spec((1, LANE))]
    args = [c_arr, kr_arr, kva_g.reshape(1, -1), wk_pad, wv, kn_g_pad]
    if rope is not None:
        in_specs += [pl.BlockSpec((tm, LANE), lambda i: (i % per, 0))] * 3
        args += list(rope)
    return pl.pallas_call(
        lambda *refs: _mla_kv_body(norm_ckv, rope is not None, *refs),
        grid=(t // tm,),
        in_specs=in_specs,
        out_specs=[pl.BlockSpec((tm, MLA_KV_RANK), lambda i: (i, 0)), pl.BlockSpec((tm, MLA_W), lambda i: (i, 0)),
                   pl.BlockSpec((tm, BR_WIDTH), lambda i: (i, 0))],
        out_shape=[jax.ShapeDtypeStruct((t, MLA_KV_RANK), F32), jax.ShapeDtypeStruct((t, MLA_W), BF16),
                   jax.ShapeDtypeStruct((t, BR_WIDTH), BF16)],
        compiler_params=_cparams("parallel"),
        name="mla_kv",
    )(*args)


def _attn_body(n_src, q_ref, gate_ref, *rest):
    kv_refs, o_ref = rest[:2 * n_src], rest[2 * n_src]
    tq = q_ref.shape[0]
    heads = (slice(0, LANE), slice(LANE, 2 * LANE))
    qs = [q_ref[:, hs] for hs in heads]

    def tile(carry, ks, v):
        ss = [lax.dot_general(q, k, (((1,), (1,)), ((), ())), preferred_element_type=F32) for q, k in zip(qs, ks)]
        m_new = [jnp.maximum(c[0], jnp.max(s, axis=-1, keepdims=True)) for c, s in zip(carry, ss)]
        ps = [jnp.exp(s - m) for s, m in zip(ss, m_new)]
        alphas = [jnp.exp(c[0] - m) for c, m in zip(carry, m_new)]
        pv = [jnp.dot(p.astype(BF16), v, preferred_element_type=F32) for p in ps]
        return tuple((m, a * c[1] + jnp.sum(p, axis=-1, keepdims=True), a * c[2] + o)
                     for c, m, a, p, o in zip(carry, m_new, alphas, ps, pv))

    init = (jnp.full((tq, 1), NEG_INF, F32), jnp.zeros((tq, 1), F32), jnp.zeros((tq, LANE), F32))
    carry = (init, init)
    for s_i in range(n_src):
        k_ref, v_ref = kv_refs[2 * s_i], kv_refs[2 * s_i + 1]
        nk = k_ref.shape[1]
        tk = min(MLA_TK, nk)

        def step(j, carry, k_ref=k_ref, v_ref=v_ref, tk=tk):
            rows = pl.ds(pl.multiple_of(j * tk, tk), tk)
            return tile(carry, [k_ref[0, rows, hs] for hs in heads], v_ref[0, rows, :])

        carry = lax.fori_loop(0, nk // tk, step, carry) if nk > tk else step(0, carry)
    accs = [acc / l for _, l, acc in carry]
    o = jnp.where(_iota2((tq, LANE), 1) < MLA_V, accs[0], accs[1])
    o_ref[...] = (o * _silu(gate_ref[...])).astype(o_ref.dtype)


def mla_attention(q, z_mla, kvs, n, tq):
    t = q.shape[0]
    b = t // n
    per = n // tq
    in_specs = [pl.BlockSpec((tq, 2 * LANE), lambda bi, p, i: (bi * per + i, p)),
                pl.BlockSpec((tq, LANE), lambda bi, p, i: (bi * per + i, MLA_GATE_LANE_BLOCK + p))]
    args = [q, z_mla]
    for k, v in kvs:
        nk = k.shape[1]
        in_specs += [pl.BlockSpec((1, nk, 2 * LANE), lambda bi, p, i: (bi, 0, p)),
                     pl.BlockSpec((1, nk, LANE), lambda bi, p, i: (bi, 0, p))]
        args += [k, v]
    return pl.pallas_call(
        lambda *refs: _attn_body(len(kvs), *refs),
        grid=(b, PAIRS, per),
        in_specs=in_specs,
        out_specs=pl.BlockSpec((tq, LANE), lambda bi, p, i: (bi * per + i, p)),
        out_shape=jax.ShapeDtypeStruct((t, BR_WIDTH), BF16),
        compiler_params=_cparams("parallel", "parallel", "arbitrary"),
        name="mla_attention",
    )(*args)


def pad_heads(w, real, lead_zero=0):
    shp = w.shape[:-1]
    w = w.reshape(*shp, MLA_HEADS, real)
    w = jnp.pad(w, [(0, 0)] * len(shp) + [(0, 0), (lead_zero, LANE - real - lead_zero)])
    return w.reshape(*shp, MLA_W)


def pad_lane(g):
    return jnp.pad(g.reshape(1, -1), ((0, 0), (0, LANE - g.size)))


def place_rope_key(k_rope):
    return pad_heads(jnp.tile(k_rope, (1,) * (k_rope.ndim - 1) + (MLA_HEADS,)), MLA_ROPE, MLA_NOPE)


def pack_weights(P, l):
    w = P['w_in'][l]
    col = lambda i: w[:, IN_OFFS[i]:IN_OFFS[i + 1]]
    zeros = lambda n: jnp.zeros((D_MODEL, n), w.dtype)
    mla_kv = col(3)
    w_rw = jnp.concatenate([col(1), col(0)], axis=1)
    w_mla = jnp.concatenate([place_rope_key(mla_kv[:, MLA_KV_RANK:]), col(4), col(2), zeros(512 - MLA_Q_RANK),
                             mla_kv[:, :MLA_KV_RANK]], axis=1)
    w_ssd = jnp.concatenate([col(5), col(6), col(7), zeros(LANE - 2 * SSD_HEADS)], axis=1)
    w_lru = jnp.concatenate([col(8), col(9)], axis=1)
    kvu = P['mla_kv_up'][l].reshape(MLA_KV_RANK, MLA_HEADS, MLA_NOPE + MLA_V)
    bf = lambda x: x.astype(BF16)
    return dict(w_rw=bf(w_rw), w_mla=bf(w_mla), w_ssd=bf(w_ssd), w_lru=bf(w_lru), w_mrg=bf(col(10)),
                wq=bf(pad_heads(P['mla_q_up'][l], MLA_QK)),
                wk=bf(pad_heads(kvu[:, :, :MLA_NOPE].reshape(MLA_KV_RANK, -1), MLA_NOPE)),
                wv=bf(kvu[:, :, MLA_NOPE:].reshape(MLA_KV_RANK, -1)),
                w_branch=bf(P['w_branch'][l]), w_out=bf(P['w_out'][l]))


def trunk_layer(x, mod, P, W, l, ctx):
    b, n, d = x.shape
    t = b * n
    tm = min(n, 512)
    tq = 256
    shift, scale, gate = mod
    x2d = x.reshape(t, d)
    h = prenorm(x2d, P['norm_g'][l], scale, shift, n, tm)
    z_rw = matmul(h, W['w_rw'], tm, Z_RW_W, F32)
    z_mla = matmul(h, W['w_mla'], tm, Z_MLA_W, F32)
    z_ssd = matmul(h, W['w_ssd'], tm, Z_SSD_W, F32)
    z_lru = matmul(h, W['w_lru'], tm, Z_LRU_W, F32)
    logits = matmul(h, W['w_mrg'], tm, 2048, BF16)
    if ctx is None:
        rw0 = jnp.zeros((b, 2, PAIRS, LANE, LANE), F32)
        ssd0 = jnp.zeros((b, 2, PAIRS, LANE, LANE), F32)
        lru0 = jnp.zeros((b, 2, LRU_WIDTH), F32)
        rope = None
    else:
        ckv0, krope0, rw0, ssd0, lru0 = ctx
        rw0, ssd0 = rw_state_to_pairs(rw0), ssd_state_to_pairs(ssd0)
        rope = rope_tables(n)

    prep = rw_prep(z_rw.reshape(b, n, -1), P, l, tq)
    o_f, o_b, rw_fin = rw_scan(prep, rw0)
    flat = lambda a: a.reshape(t, -1)
    br_rw = rw_post(flat(o_f), flat(o_b), flat(prep[9]), z_rw, P['rw_ln_g'][l], P['rw_ln_b'][l], tm)

    qn_g, kn_g = pad_lane(P['mla_qn_g'][l]), pad_lane(P['mla_kn_g'][l])
    q = mla_q(z_mla, P['mla_qa_g'][l], W['wq'], qn_g, rope, n, tm)
    c_kv, k, v = mla_kv(z_mla, MLA_CKV_BLOCK, z_mla, 0, P['mla_kva_g'][l], W['wk'], W['wv'], kn_g, rope, True, n, tm)
    kvs = [(k.reshape(b, n, -1), v.reshape(b, n, -1))]
    if ctx is not None:
        past = ckv0.shape[1]
        _, k0, v0 = mla_kv(ckv0.reshape(b * past, -1), 0, place_rope_key(krope0).reshape(b * past, -1), 0,
                           P['mla_kva_g'][l], W['wk'], W['wv'], kn_g, None, False, past, past)
        kvs = [(k0.reshape(b, past, -1), v0.reshape(b, past, -1))] + kvs
    br_mla = mla_attention(q, z_mla, kvs, n, min(n, 256))
    k_rope = z_mla[:, MLA_NOPE:MLA_NOPE + MLA_ROPE]

    xbc, dtp = ssd_prep(z_ssd.reshape(b, n, -1), P['ssd_conv_w'][l], P['ssd_conv_b'][l], P['ssd_dt_bias'][l],
                        P['ssd_a_log'][l], tq)
    y_f, y_b, ssd_fin = ssd_scan(xbc, dtp, ssd0)
    br_ssd = ssd_post(flat(y_f), flat(y_b), flat(xbc), z_ssd, jnp.repeat(P['ssd_d'][l], SSD_HEAD_DIM),
                      P['ssd_norm_g'][l], tm)

    a0, u0, a1, u1 = lru_prep(z_lru.reshape(b, n, -1), P['lru_conv_w'][l], P['lru_conv_b'][l], P['lru_wa'][l],
                              P['lru_ba'][l], P['lru_wx'][l], P['lru_bx'][l], P['lru_lambda'][l], tq)
    h_f, h_b, lru_fin = lru_scan(a0, u0, a1, u1, lru0, tq)
    br_lru = lru_post(flat(h_f), flat(h_b), z_lru, tm)

    y = merge(x2d, [br_rw, br_mla, br_ssd, br_lru], logits, W['w_branch'], W['w_out'], gate, n, tm)
    states = (c_kv.reshape(b, n, -1), k_rope.reshape(b, n, -1), rw_pairs_to_state(rw_fin),
              ssd_pairs_to_state(ssd_fin), lru_fin)
    return y.reshape(b, n, d), states


def kernel(x_prompt, x_sample, cache_mla_ckv, cache_mla_krope, state_rwkv, state_ssd, state_lru, c, c_ctx, ada_w, ada_b, norm_g, w_in, rw_mu, rw_w0, rw_w2, rw_a0, rw_a2, rw_kk, rw_ka, rw_rk, rw_ln_g, rw_ln_b, mla_qa_g, mla_q_up, mla_kva_g, mla_kv_up, mla_qn_g, mla_kn_g, ssd_conv_w, ssd_conv_b, ssd_dt_bias, ssd_a_log, ssd_d, ssd_norm_g, lru_conv_w, lru_conv_b, lru_wa, lru_ba, lru_wx, lru_bx, lru_lambda, w_branch, w_out):
    P = dict(norm_g=norm_g, w_in=w_in,
             rw_mu=rw_mu, rw_w0=rw_w0, rw_w2=rw_w2, rw_a0=rw_a0, rw_a2=rw_a2, rw_kk=rw_kk,
             rw_ka=rw_ka, rw_rk=rw_rk.reshape(DEPTH, BR_WIDTH), rw_ln_g=rw_ln_g, rw_ln_b=rw_ln_b,
             mla_qa_g=mla_qa_g, mla_q_up=mla_q_up, mla_kva_g=mla_kva_g, mla_kv_up=mla_kv_up,
             mla_qn_g=mla_qn_g, mla_kn_g=mla_kn_g,
             ssd_conv_w=ssd_conv_w, ssd_conv_b=ssd_conv_b, ssd_dt_bias=ssd_dt_bias,
             ssd_a_log=ssd_a_log, ssd_d=ssd_d, ssd_norm_g=ssd_norm_g,
             lru_conv_w=lru_conv_w, lru_conv_b=lru_conv_b, lru_wa=lru_wa, lru_ba=lru_ba,
             lru_wx=lru_wx, lru_bx=lru_bx, lru_lambda=lru_lambda,
             w_branch=w_branch, w_out=w_out)
    dec_batch = c.shape[0]
    cond8 = jnp.concatenate([c_ctx[None, :], c, jnp.zeros((SUBLANE - 1 - dec_batch, D_MODEL), F32)], axis=0)
    weights, mods = [], []
    for l in range(DEPTH):
        weights.append(pack_weights(P, l))
        mod = ada_mod(cond8, ada_w[l], ada_b[l])
        mods.append(tuple(jnp.split(mod[:, None, :], 3, axis=-1)))

    y_prompt = x_prompt
    ctx_states = []
    for l in range(DEPTH):
        y_prompt, st = trunk_layer(y_prompt, tuple(m[0:1] for m in mods[l]), P, weights[l], l, None)
        ctx_states.append(st)
    new_states = tuple(jnp.stack([s[i] for s in ctx_states], axis=1) for i in range(5))

    y_sample = x_sample
    for l in range(DEPTH):
        ctx = (cache_mla_ckv[:, l], cache_mla_krope[:, l], state_rwkv[:, l], state_ssd[:, l], state_lru[:, l])
        y_sample, _ = trunk_layer(y_sample, tuple(m[1:1 + dec_batch] for m in mods[l]), P, weights[l], l, ctx)

    return (y_prompt, y_sample) + new_states
```

```python
import math

import jax
import jax.numpy as jnp
import numpy as np
from jax import lax
from jax.experimental import pallas as pl
from jax.experimental.pallas import tpu as pltpu

F32 = jnp.float32
BF16 = jnp.bfloat16
HIGHEST = lax.Precision.HIGHEST

D_MODEL = 1024
DEPTH = 2
GRID_W = 64
N_BRANCH = 4
BR_WIDTH = 512
EPS = 1e-6
CONV_K = 4

RW_HEADS = 8
RW_HEAD = 64
RW_LORA = 64
RW_GN_EPS = 64e-5
RW_PRE = 3 * BR_WIDTH + 2 * RW_LORA

MLA_HEADS = 8
MLA_NOPE = 64
MLA_ROPE = 32
MLA_QK = MLA_NOPE + MLA_ROPE
MLA_V = 64
MLA_Q_RANK = 384
MLA_KV_RANK = 256
ROPE_THETA = 10000.0

SSD_HEADS = 8
SSD_HEAD_DIM = 64
SSD_GROUPS = 2
SSD_STATE = 64
SSD_CONV_DIM = BR_WIDTH + 2 * SSD_GROUPS * SSD_STATE

LRU_WIDTH = 512
LRU_BLOCKS = 8
LRU_C = 8.0

IN_WIDTHS = (RW_PRE, BR_WIDTH, MLA_Q_RANK, MLA_KV_RANK + MLA_ROPE, BR_WIDTH, BR_WIDTH, SSD_CONV_DIM,
             2 * SSD_HEADS, LRU_WIDTH, LRU_WIDTH, N_BRANCH * D_MODEL)
IN_OFFS = tuple(int(v) for v in np.cumsum((0,) + IN_WIDTHS))

LANE = 128
SUBLANE = 8
HALO = 16
VMEM_LIMIT = 48 * 1024 * 1024

RW_CHUNK = 64
PAIRS = BR_WIDTH // LANE
SSD_Q = 128
MLA_W = MLA_HEADS * LANE
MLA_TK = 4096
NEG_INF = -1e30

Z_RW_W = BR_WIDTH + RW_PRE
Z_MLA_W = MLA_W + BR_WIDTH + 512 + MLA_KV_RANK
Z_SSD_W = BR_WIDTH + SSD_CONV_DIM + LANE
Z_LRU_W = 2 * LRU_WIDTH
MLA_GATE_LANE_BLOCK = MLA_W // LANE
MLA_Q_BLOCK = (MLA_W + BR_WIDTH) // 512
MLA_CKV_BLOCK = (MLA_W + BR_WIDTH + 512) // MLA_KV_RANK


def _cparams(*sem):
    return pltpu.CompilerParams(dimension_semantics=sem, vmem_limit_bytes=VMEM_LIMIT)


def _const_spec(shape):
    return pl.BlockSpec(shape, lambda *_: (0,) * len(shape))


def _mm(a, b):
    return jnp.dot(a.astype(BF16), b.astype(BF16), preferred_element_type=F32)


def _mm_nt(a, b):
    return lax.dot_general(a.astype(BF16), b.astype(BF16), (((1,), (1,)), ((), ())), preferred_element_type=F32)


def _mm_f32(a, b):
    return jnp.dot(a, b, preferred_element_type=F32, precision=HIGHEST)


def _iota2(shape, dim):
    return lax.broadcasted_iota(jnp.int32, shape, dim)


def _silu(x):
    return x * jax.nn.sigmoid(x)


def _softplus(x):
    return jnp.maximum(x, 0.0) + jnp.log(1.0 + jnp.exp(-jnp.abs(x)))


def _head_sum(x, width):
    bd = ((_iota2((LANE, LANE), 0) // width) == (_iota2((LANE, LANE), 1) // width)).astype(BF16)
    hi = x.astype(BF16)
    lo = (x - hi.astype(F32)).astype(BF16)
    outs = []
    for j in range(x.shape[-1] // LANE):
        sl = slice(j * LANE, (j + 1) * LANE)
        outs.append(jnp.dot(hi[:, sl], bd, preferred_element_type=F32) + jnp.dot(lo[:, sl], bd, preferred_element_type=F32))
    return jnp.concatenate(outs, axis=1)


def _from_prev(u, s, prev8):
    rolled = pltpu.roll(u, s, 0)
    head = jnp.where(_iota2(prev8.shape, 0) < s, pltpu.roll(prev8, s, 0), rolled[:SUBLANE])
    return jnp.concatenate([head, rolled[SUBLANE:]], axis=0)


def _from_next(u, next8):
    n = u.shape[0]
    rolled = pltpu.roll(u, n - 1, 0)
    tail = jnp.where(_iota2(next8.shape, 0) == SUBLANE - 1, pltpu.roll(next8, SUBLANE - 1, 0), rolled[n - SUBLANE:])
    return jnp.concatenate([rolled[:n - SUBLANE], tail], axis=0)


def _centred_conv(x, prev8, next8, w_ref, b_ref):
    return (w_ref[0:1, :] * _from_prev(x, 2, prev8) + w_ref[1:2, :] * _from_prev(x, 1, prev8)
            + w_ref[2:3, :] * x + w_ref[3:4, :] * _from_next(x, next8) + b_ref[...])


def _halo_specs(tq, n, c):
    per, nb = tq // HALO, n // HALO
    return [pl.BlockSpec((1, tq, c), lambda bi, i: (bi, i, 0)),
            pl.BlockSpec((1, HALO, c), lambda bi, i: (bi, jnp.maximum(i * per - 1, 0), 0)),
            pl.BlockSpec((1, HALO, c), lambda bi, i: (bi, jnp.minimum((i + 1) * per, nb - 1), 0))]


def _halo_rows(zp_ref, zn_ref, cols):
    i = pl.program_id(1)
    prev8 = jnp.where(i > 0, zp_ref[0, HALO - SUBLANE:, cols].astype(F32), 0.0)
    next8 = jnp.where(i < pl.num_programs(1) - 1, zn_ref[0, :SUBLANE, cols].astype(F32), 0.0)
    return prev8, next8


def _ada_body(c_ref, w_ref, b_ref, o_ref):
    o_ref[...] = _mm(_silu(c_ref[...]), w_ref[...]) + b_ref[...]


def ada_mod(cond8, w, bias):
    d = cond8.shape[1]
    return pl.pallas_call(
        _ada_body,
        grid=(w.shape[1] // d,),
        in_specs=[_const_spec((SUBLANE, d)), pl.BlockSpec((d, d), lambda j: (0, j)), pl.BlockSpec((1, d), lambda j: (0, j))],
        out_specs=pl.BlockSpec((SUBLANE, d), lambda j: (0, j)),
        out_shape=jax.ShapeDtypeStruct((SUBLANE, w.shape[1]), F32),
        compiler_params=_cparams("parallel"),
        name="ada_mod",
    )(cond8, w, bias.reshape(1, -1))


def _prenorm_body(x_ref, g_ref, sc_ref, sh_ref, h_ref):
    x = x_ref[...]
    y = x * lax.rsqrt(jnp.mean(x * x, axis=-1, keepdims=True) + EPS)
    h_ref[...] = ((y * g_ref[...]) * (1.0 + sc_ref[0]) + sh_ref[0]).astype(h_ref.dtype)


def _mod_index(bm, per):
    return (lambda i: (i // per, 0, 0)) if bm > 1 else (lambda i: (0, 0, 0))


def prenorm(x2d, g, scale, shift, n, tm):
    t, d = x2d.shape
    mod_idx = _mod_index(scale.shape[0], n // tm)
    return pl.pallas_call(
        _prenorm_body,
        grid=(t // tm,),
        in_specs=[pl.BlockSpec((tm, d), lambda i: (i, 0)), _const_spec((1, d)),
                  pl.BlockSpec((1, 1, d), mod_idx), pl.BlockSpec((1, 1, d), mod_idx)],
        out_specs=pl.BlockSpec((tm, d), lambda i: (i, 0)),
        out_shape=jax.ShapeDtypeStruct((t, d), BF16),
        compiler_params=_cparams("parallel"),
        name="prenorm",
    )(x2d, g.reshape(1, -1), scale, shift)


def _mm_body(a_ref, w_ref, o_ref):
    o_ref[...] = jnp.dot(a_ref[...], w_ref[...], preferred_element_type=F32).astype(o_ref.dtype)


def matmul(a, w, tm, tn, out_dtype):
    t, k = a.shape
    n = w.shape[1]
    return pl.pallas_call(
        _mm_body,
        grid=(n // tn, t // tm),
        in_specs=[pl.BlockSpec((tm, k), lambda j, i: (i, 0)), pl.BlockSpec((k, tn), lambda j, i: (0, j))],
        out_specs=pl.BlockSpec((tm, tn), lambda j, i: (i, j)),
        out_shape=jax.ShapeDtypeStruct((t, n), out_dtype),
        compiler_params=_cparams("parallel", "parallel"),
        name="in_proj",
    )(a, w)


def _merge_body(x_ref, b0_ref, b1_ref, b2_ref, b3_ref, lg_ref, wb_ref, wo_ref, gate_ref, y_ref):
    merged = None
    for m, b_ref in enumerate((b0_ref, b1_ref, b2_ref, b3_ref)):
        proj = jnp.dot(b_ref[...], wb_ref[m], preferred_element_type=F32)
        g = jax.nn.sigmoid(lg_ref[:, m * D_MODEL:(m + 1) * D_MODEL].astype(F32))
        merged = g * proj if merged is None else merged + g * proj
    out = jnp.dot(merged.astype(BF16), wo_ref[...], preferred_element_type=F32)
    y_ref[...] = x_ref[...] + gate_ref[0] * out


def merge(x2d, branches, logits, wb, wo, gate, n, tm):
    t, d = x2d.shape
    mod_idx = _mod_index(gate.shape[0], n // tm)
    row = lambda i: (i, 0)
    return pl.pallas_call(
        _merge_body,
        grid=(t // tm,),
        in_specs=[pl.BlockSpec((tm, d), row)] + [pl.BlockSpec((tm, BR_WIDTH), row)] * N_BRANCH
        + [pl.BlockSpec((tm, N_BRANCH * d), row), _const_spec((N_BRANCH, BR_WIDTH, d)), _const_spec((d, d)),
           pl.BlockSpec((1, 1, d), mod_idx)],
        out_specs=pl.BlockSpec((tm, d), row),
        out_shape=jax.ShapeDtypeStruct((t, d), F32),
        compiler_params=_cparams("parallel"),
        name="merge",
    )(x2d, *branches, logits, wb, wo, gate)


def _rw_prep_body(z_ref, zp_ref, zn_ref, mu_ref, kkg_ref, ka_ref, rk_ref, w0_ref, w2_ref, a0_ref, a2_ref,
                  r_ref, kk_ref, v_ref, lw0_ref, k0_ref, b0_ref, lw1_ref, k1_ref, b1_ref, bonus_ref):
    cols = slice(BR_WIDTH, Z_RW_W)
    prev8, next8 = _halo_rows(zp_ref, zn_ref, cols)
    u = z_ref[0, :, cols].astype(F32)
    u = u + mu_ref[...] * (0.5 * (_from_prev(u, 1, prev8) + _from_next(u, next8)) - u)
    r = u[:, 0:BR_WIDTH]
    k = u[:, BR_WIDTH:2 * BR_WIDTH]
    v = u[:, 2 * BR_WIDTH:3 * BR_WIDTH]
    lo = u[:, 3 * BR_WIDTH:]
    lo = jnp.where(_iota2(lo.shape, 1) < RW_LORA, jnp.tanh(lo), lo)
    kk = k * kkg_ref[...]
    kk = kk * lax.rsqrt(_head_sum(kk * kk, RW_HEAD) + 1e-12)
    r_ref[0] = r.astype(r_ref.dtype)
    kk_ref[0] = kk.astype(kk_ref.dtype)
    v_ref[0] = v.astype(v_ref.dtype)
    coef = jnp.zeros_like(r)
    lo_hi = lo.astype(BF16)
    lo_lo = (lo - lo_hi.astype(F32)).astype(BF16)
    lora = lambda w_ref, d: (jnp.dot(lo_hi, w_ref[d, 0], preferred_element_type=F32)
                             + jnp.dot(lo_hi, w_ref[d, 1], preferred_element_type=F32)
                             + jnp.dot(lo_lo, w_ref[d, 0], preferred_element_type=F32))
    for d, (lw_ref, k_ref, b_ref) in enumerate(((lw0_ref, k0_ref, b0_ref), (lw1_ref, k1_ref, b1_ref))):
        w_raw = w0_ref[d] + lora(w2_ref, d)
        lw_ref[0] = -math.exp(-0.5) * jax.nn.sigmoid(w_raw)
        a = jax.nn.sigmoid(a0_ref[d] + lora(a2_ref, d))
        k_d = k * (1.0 + (a - 1.0) * ka_ref[...])
        k_ref[0] = k_d.astype(k_ref.dtype)
        b_ref[0] = (kk * a).astype(b_ref.dtype)
        coef = coef + _head_sum(r * k_d * rk_ref[...], RW_HEAD)
    bonus_ref[0] = coef * v


def rw_prep(z_rw, P, l, tq):
    b, n, c = z_rw.shape
    w2 = jnp.concatenate([P['rw_w2'][l], jnp.zeros_like(P['rw_w2'][l])], axis=1)
    a2 = jnp.concatenate([jnp.zeros_like(P['rw_a2'][l]), P['rw_a2'][l]], axis=1)
    row = lambda x: x.reshape(1, -1)
    tile = pl.BlockSpec((1, tq, BR_WIDTH), lambda bi, i: (bi, i, 0))
    out = lambda dt: jax.ShapeDtypeStruct((b, n, BR_WIDTH), dt)
    return pl.pallas_call(
        _rw_prep_body,
        grid=(b, n // tq),
        in_specs=_halo_specs(tq, n, c) + [_const_spec((1, RW_PRE))] + [_const_spec((1, BR_WIDTH))] * 3
        + [_const_spec((2, 1, BR_WIDTH)), _const_spec((2, 2, LANE, BR_WIDTH))] * 2,
        out_specs=[tile] * 10,
        out_shape=[out(BF16)] * 3 + [out(F32), out(BF16), out(BF16)] * 2 + [out(F32)],
        compiler_params=_cparams("parallel", "parallel"),
        name="rw_prep",
    )(z_rw, z_rw, z_rw, row(P['rw_mu'][l]), row(P['rw_kk'][l]), row(P['rw_ka'][l]), row(P['rw_rk'][l]),
      P['rw_w0'][l][:, None, :], _split_bf16(w2), P['rw_a0'][l][:, None, :], _split_bf16(a2))


def _split_bf16(w):
    hi = w.astype(BF16)
    return jnp.stack([hi, (w - hi.astype(F32)).astype(BF16)], axis=1)


def _rw_masks(rev):
    n = 2 * RW_CHUNK
    ri, ci = _iota2((n, n), 0), _iota2((n, n), 1)
    same = (ri // RW_CHUNK) == (ci // RW_CHUNK)
    strict = same & ((ci > ri) if rev else (ci < ri))
    incl = same & ((ci >= ri) if rev else (ci <= ri))
    blk8 = strict & ((ri // 8) == (ci // 8))
    levels = [strict & ((ri // (2 * s)) == (ci // (2 * s))) & ((ri // s) != (ci // s)) for s in (8, 16, 32)]
    eye = (ri == ci).astype(F32)
    t_i, s_i = _iota2((RW_CHUNK, RW_CHUNK), 0), _iota2((RW_CHUNK, RW_CHUNK), 1)
    tri = ((s_i >= t_i) if rev else (s_i <= t_i)).astype(F32)
    m0 = (_iota2((RW_CHUNK, LANE), 1) < RW_HEAD).astype(F32)
    return dict(strict=strict, incl=incl, blk8=blk8, levels=levels, eye=eye, tri=tri, m0=m0, m1=1.0 - m0)


def _rw_chunks(chains):
    n = 2 * RW_CHUNK
    each = lambda f, *cols: [f(*a) for a in zip(*cols)]
    revs, mks = [c[0] for c in chains], [c[1] for c in chains]
    lws, rs_, kks, vs_, ks_, bs_, ss = ([c[i] for c in chains] for i in range(2, 9))
    stack = lambda mk, x: jnp.concatenate([x * mk['m0'], x * mk['m1']], axis=0)
    cums = each(lambda mk, lw: _mm_f32(mk['tri'], lw), mks, lws)
    totals = each(lambda rev, cum: cum[0:1] if rev else cum[RW_CHUNK - 1:RW_CHUNK], revs, cums)
    e_negs = each(lambda cum: jnp.exp(-cum), cums)
    e_tots = each(lambda cum, tot: jnp.exp(tot - cum), cums, totals)
    rs = each(lambda mk, r, cum: stack(mk, r * jnp.exp(cum)), mks, rs_, cums)
    ks = each(lambda mk, kk, cum, lw: stack(mk, kk * jnp.exp(cum - lw)), mks, kks, cums, lws)
    vs = each(stack, mks, vs_)
    both = lambda mk, b, k, e: jnp.concatenate([stack(mk, b * e), stack(mk, k * e)], axis=0)
    rhs = each(both, mks, bs_, ks_, e_negs)
    hat = each(both, mks, bs_, ks_, e_tots)
    g1 = each(_mm_nt, ks, rhs)
    g2 = each(_mm_nt, rs, rhs)
    a_b = each(lambda mk, g: jnp.where(mk['strict'], g[:, :n], 0.0), mks, g1)
    a_k = each(lambda mk, g: jnp.where(mk['strict'], g[:, n:], 0.0), mks, g1)
    m_b = each(lambda mk, g: jnp.where(mk['incl'], g[:, :n], 0.0), mks, g2)
    m_k = each(lambda mk, g: jnp.where(mk['incl'], g[:, n:], 0.0), mks, g2)
    x = each(lambda mk, a: -jnp.where(mk['blk8'], a, 0.0), mks, a_b)
    x2 = each(_mm, x, x)
    akv = each(_mm, a_k, vs)
    t_inv = each(lambda mk, x: mk['eye'] + x, mks, x)
    x4 = each(_mm, x2, x2)
    t_inv = each(lambda t, x2: t + _mm(t, x2), t_inv, x2)
    t_inv = each(lambda t, x4: t + _mm(t, x4), t_inv, x4)
    for li in range(3):
        tl = each(lambda mk, t, a: _mm(t, jnp.where(mk['levels'][li], a, 0.0)), mks, t_inv, a_b)
        t_inv = each(lambda t, tl: t - _mm(tl, t), t_inv, tl)
    tx = each(lambda t, akv, ks: _mm(t, jnp.concatenate([akv, ks], axis=1)), t_inv, akv, ks)
    u2 = each(lambda tx: -tx[:, :LANE], tx)
    w1 = each(lambda tx: -tx[:, LANE:], tx)
    u = each(lambda w1, s, u2: _mm_nt(w1, s) + u2, w1, ss, u2)
    q_eff = each(lambda rs, m_b, w1: rs + _mm(m_b, w1), rs, m_b, w1)
    o_intra = each(lambda m_b, m_k, u2, vs: _mm(jnp.concatenate([m_b, m_k], axis=1), jnp.concatenate([u2, vs], axis=0)),
                   m_b, m_k, u2, vs)
    s_new = each(lambda s, tot, u, vs, hat: s * jnp.exp(tot) + _mm(jnp.concatenate([u, vs], axis=0).T, hat),
                 ss, totals, u, vs, hat)
    o_s = each(lambda q_eff, s, o_intra: _mm_nt(q_eff, s) + o_intra, q_eff, ss, o_intra)
    return each(lambda o_s: o_s[:RW_CHUNK] + o_s[RW_CHUNK:], o_s), s_new


def _rw_scan_body(rf_ref, kkf_ref, vf_ref, lwf_ref, kf_ref, bf_ref, rb_ref, kkb_ref, vb_ref, lwb_ref, kb_ref, bb_ref,
                  s0_ref, of_ref, ob_ref, sfin_ref, s_scr):
    c = pl.program_id(1)

    @pl.when(c == 0)
    def _():
        s_scr[...] = s0_ref[0]

    chains, dests = [], []
    for d, refs, o_ref in ((0, (lwf_ref, rf_ref, kkf_ref, vf_ref, kf_ref, bf_ref), of_ref),
                           (1, (lwb_ref, rb_ref, kkb_ref, vb_ref, kb_ref, bb_ref), ob_ref)):
        mk = _rw_masks(rev=(d == 1))
        for p in range(PAIRS):
            sl = slice(p * LANE, (p + 1) * LANE)
            chains.append((d == 1, mk) + tuple(ref[0, :, sl].astype(F32) for ref in refs) + (s_scr[d, p],))
            dests.append((o_ref, sl, d, p))
    outs, states = _rw_chunks(chains)
    for (o_ref, sl, d, p), o, s_new in zip(dests, outs, states):
        o_ref[0, :, sl] = o.astype(o_ref.dtype)
        s_scr[d, p] = s_new

    @pl.when(c == pl.num_programs(1) - 1)
    def _():
        sfin_ref[0] = s_scr[...]


def rw_scan(prep, s0_pairs):
    r, kk, v, lw0, k0, b0, lw1, k1, b1, _ = prep
    b, n, _ = r.shape
    nc = n // RW_CHUNK
    fwd = pl.BlockSpec((1, RW_CHUNK, BR_WIDTH), lambda bi, c: (bi, c, 0))
    bwd = pl.BlockSpec((1, RW_CHUNK, BR_WIDTH), lambda bi, c: (bi, nc - 1 - c, 0))
    st = pl.BlockSpec((1, 2, PAIRS, LANE, LANE), lambda bi, c: (bi, 0, 0, 0, 0))
    return pl.pallas_call(
        _rw_scan_body,
        grid=(b, nc),
        in_specs=[fwd] * 6 + [bwd] * 6 + [st],
        out_specs=[fwd, bwd, st],
        out_shape=[jax.ShapeDtypeStruct((b, n, BR_WIDTH), BF16)] * 2
        + [jax.ShapeDtypeStruct((b, 2, PAIRS, LANE, LANE), F32)],
        scratch_shapes=[pltpu.VMEM((2, PAIRS, LANE, LANE), F32)],
        compiler_params=_cparams("parallel", "arbitrary"),
        name="rw_scan",
    )(r, kk, v, lw0, k0, b0, r, kk, v, lw1, k1, b1, s0_pairs)


def rw_state_to_pairs(s0):
    b = s0.shape[0]
    s = s0.reshape(b, 2, PAIRS, 2, RW_HEAD, RW_HEAD)
    z = jnp.zeros_like(s[:, :, :, 0])
    top = jnp.concatenate([s[:, :, :, 0], z], axis=-1)
    bot = jnp.concatenate([z, s[:, :, :, 1]], axis=-1)
    return jnp.concatenate([top, bot], axis=-2)


def rw_pairs_to_state(sp):
    b = sp.shape[0]
    h0 = sp[:, :, :, :RW_HEAD, :RW_HEAD]
    h1 = sp[:, :, :, RW_HEAD:, RW_HEAD:]
    return jnp.stack([h0, h1], axis=3).reshape(b, 2, RW_HEADS, RW_HEAD, RW_HEAD)


def _rw_post_body(of_ref, ob_ref, bonus_ref, gate_ref, g_ref, bias_ref, o_ref):
    wkv = of_ref[...].astype(F32) + ob_ref[...].astype(F32)
    mu = _head_sum(wkv, RW_HEAD) * (1.0 / RW_HEAD)
    xc = wkv - mu
    var = _head_sum(xc * xc, RW_HEAD) * (1.0 / RW_HEAD)
    o = xc * lax.rsqrt(var + RW_GN_EPS) * g_ref[...] + bias_ref[...] + bonus_ref[...]
    o_ref[...] = (o * _silu(gate_ref[...].astype(F32))).astype(o_ref.dtype)


def rw_post(o_f, o_b, bonus, z_rw2d, ln_g, ln_b, tm):
    t = o_f.shape[0]
    tile = pl.BlockSpec((tm, BR_WIDTH), lambda i: (i, 0))
    return pl.pallas_call(
        _rw_post_body,
        grid=(t // tm,),
        in_specs=[tile] * 4 + [_const_spec((1, BR_WIDTH))] * 2,
        out_specs=tile,
        out_shape=jax.ShapeDtypeStruct((t, BR_WIDTH), BF16),
        compiler_params=_cparams("parallel"),
        name="rw_post",
    )(o_f, o_b, bonus, z_rw2d, ln_g.reshape(1, -1), ln_b.reshape(1, -1))


def _ssd_prep_body(z_ref, zp_ref, zn_ref, w_ref, b_ref, dtb_ref, nega_ref, xbc_ref, dtp_ref):
    cols = slice(BR_WIDTH, BR_WIDTH + SSD_CONV_DIM)
    prev8, next8 = _halo_rows(zp_ref, zn_ref, cols)
    xbc_ref[0] = _silu(_centred_conv(z_ref[0, :, cols], prev8, next8, w_ref, b_ref)).astype(xbc_ref.dtype)
    dt = _softplus(z_ref[0, :, BR_WIDTH + SSD_CONV_DIM:] + dtb_ref[...])
    dta = pltpu.roll(dt, 2 * SSD_HEADS, 1) * nega_ref[...]
    lane = _iota2(dt.shape, 1)
    dtp_ref[0] = jnp.where(lane < 2 * SSD_HEADS, dt, jnp.where(lane < 4 * SSD_HEADS, dta, 0.0))


def ssd_prep(z_ssd, conv_w, conv_b, dt_bias, a_log, tq):
    b, n, c = z_ssd.shape
    dtb = jnp.pad(dt_bias.reshape(1, -1), ((0, 0), (0, LANE - 2 * SSD_HEADS)))
    nega = jnp.pad(-jnp.exp(a_log).reshape(1, -1), ((0, 0), (2 * SSD_HEADS, LANE - 4 * SSD_HEADS)))
    return pl.pallas_call(
        _ssd_prep_body,
        grid=(b, n // tq),
        in_specs=_halo_specs(tq, n, c) + [_const_spec((CONV_K, SSD_CONV_DIM)), _const_spec((1, SSD_CONV_DIM)),
                                           _const_spec((1, LANE)), _const_spec((1, LANE))],
        out_specs=[pl.BlockSpec((1, tq, SSD_CONV_DIM), lambda bi, i: (bi, i, 0)),
                   pl.BlockSpec((1, tq, LANE), lambda bi, i: (bi, i, 0))],
        out_shape=[jax.ShapeDtypeStruct((b, n, SSD_CONV_DIM), BF16), jax.ShapeDtypeStruct((b, n, LANE), F32)],
        compiler_params=_cparams("parallel", "parallel"),
        name="ssd_prep",
    )(z_ssd, z_ssd, z_ssd, conv_w, conv_b.reshape(1, -1), dtb, nega)


def _ssd_chunk(rev, d, xbc, dtp, st_ref):
    q = SSD_Q
    li, si = _iota2((q, q), 0), _iota2((q, q), 1)
    incl = (si >= li) if rev else (si <= li)
    tri = incl.astype(F32)
    head0 = _iota2((q, LANE), 1) < SSD_HEAD_DIM
    m0 = head0.astype(F32)
    m1 = 1.0 - m0
    dt_lane, a_lane = d * SSD_HEADS, (2 + d) * SSD_HEADS

    def per_head_lanes(arr, lane0):
        cols = [arr[:, lane0 + h:lane0 + h + 1] for h in range(SSD_HEADS)]
        return jnp.concatenate([jnp.where(head0, cols[2 * p], cols[2 * p + 1]) for p in range(PAIRS)], axis=1)

    cum = _mm_f32(tri, dtp)
    acs_rows = cum.T[a_lane:a_lane + SSD_HEADS, :]
    acs = per_head_lanes(cum, a_lane)
    a_tot = acs[0:1] if rev else acs[q - 1:q]
    x, bm, cm = xbc[:, :BR_WIDTH], xbc[:, BR_WIDTH:BR_WIDTH + LANE], xbc[:, BR_WIDTH + LANE:]
    xd = x * per_head_lanes(dtp, dt_lane)
    e_acs = jnp.exp(acs)
    xdec = xd * jnp.exp(a_tot - acs)
    e_tot = jnp.exp(a_tot)
    ys = []
    for g in range(SSD_GROUPS):
        mg = m0 if g == 0 else m1
        cb = _mm_nt(cm * mg, bm)
        bm_t = (bm * mg).T
        for p in range(2 * g, 2 * g + 2):
            sl = slice(p * LANE, (p + 1) * LANE)
            ms = []
            for hh in range(2):
                h = 2 * p + hh
                col = cum[:, a_lane + h:a_lane + h + 1]
                ms.append(jnp.where(incl, cb * jnp.exp(jnp.minimum(col - acs_rows[h:h + 1, :], 0.0)), 0.0))
            xs = xd[:, sl]
            y = _mm(jnp.concatenate(ms, axis=1), jnp.concatenate([xs * m0, xs * m1], axis=0))
            y = y + _mm(cm, st_ref[p]) * e_acs[:, sl]
            st_ref[p] = st_ref[p] * e_tot[:, sl] + _mm(bm_t, xdec[:, sl])
            ys.append(y)
    return jnp.concatenate(ys, axis=1)


def _ssd_scan_body(xf_ref, dtf_ref, xb_ref, dtb_ref, s0_ref, yf_ref, yb_ref, sfin_ref, st_scr):
    c = pl.program_id(1)

    @pl.when(c == 0)
    def _():
        for d in range(2):
            for p in range(PAIRS):
                st_scr[d, p] = s0_ref[0, d, p].T

    yf_ref[0] = _ssd_chunk(False, 0, xf_ref[0].astype(F32), dtf_ref[0], st_scr.at[0]).astype(yf_ref.dtype)
    yb_ref[0] = _ssd_chunk(True, 1, xb_ref[0].astype(F32), dtb_ref[0], st_scr.at[1]).astype(yb_ref.dtype)

    @pl.when(c == pl.num_programs(1) - 1)
    def _():
        for d in range(2):
            for p in range(PAIRS):
                sfin_ref[0, d, p] = st_scr[d, p].T


def ssd_scan(xbc, dtp, s0_pairs):
    b, n, _ = xbc.shape
    nc = n // SSD_Q
    fw = lambda w: pl.BlockSpec((1, SSD_Q, w), lambda bi, c: (bi, c, 0))
    bw = lambda w: pl.BlockSpec((1, SSD_Q, w), lambda bi, c: (bi, nc - 1 - c, 0))
    st = pl.BlockSpec((1, 2, PAIRS, LANE, LANE), lambda bi, c: (bi, 0, 0, 0, 0))
    return pl.pallas_call(
        _ssd_scan_body,
        grid=(b, nc),
        in_specs=[fw(SSD_CONV_DIM), fw(LANE), bw(SSD_CONV_DIM), bw(LANE), st],
        out_specs=[fw(BR_WIDTH), bw(BR_WIDTH), st],
        out_shape=[jax.ShapeDtypeStruct((b, n, BR_WIDTH), BF16)] * 2
        + [jax.ShapeDtypeStruct((b, 2, PAIRS, LANE, LANE), F32)],
        scratch_shapes=[pltpu.VMEM((2, PAIRS, LANE, LANE), F32)],
        compiler_params=_cparams("parallel", "arbitrary"),
        name="ssd_scan",
    )(xbc, dtp, xbc, dtp, s0_pairs)


def ssd_state_to_pairs(h0):
    b = h0.shape[0]
    s = h0.reshape(b, 2, PAIRS, LANE, SSD_STATE)
    lo = jnp.pad(s, ((0, 0),) * 4 + ((0, SSD_STATE),))
    hi = jnp.pad(s, ((0, 0),) * 4 + ((SSD_STATE, 0),))
    return jnp.where(_pair_in_group0(), lo, hi)


def _pair_in_group0():
    return (jnp.arange(PAIRS) < PAIRS // SSD_GROUPS)[None, None, :, None, None]


def ssd_pairs_to_state(sp):
    b = sp.shape[0]
    s = jnp.where(_pair_in_group0(), sp[..., :SSD_STATE], sp[..., SSD_STATE:])
    return s.reshape(b, 2, SSD_HEADS, SSD_HEAD_DIM, SSD_STATE)


def _ssd_post_body(yf_ref, yb_ref, x_ref, gate_ref, d_ref, g_ref, o_ref):
    y = (x_ref[...].astype(F32) * d_ref[...] + yf_ref[...].astype(F32) + yb_ref[...].astype(F32)) * _silu(gate_ref[...].astype(F32))
    y = y * lax.rsqrt(jnp.mean(y * y, axis=-1, keepdims=True) + EPS)
    o_ref[...] = (y * g_ref[...]).astype(o_ref.dtype)


def ssd_post(y_f, y_b, xbc2d, z_ssd2d, d_full, norm_g, tm):
    t = y_f.shape[0]
    tile = pl.BlockSpec((tm, BR_WIDTH), lambda i: (i, 0))
    return pl.pallas_call(
        _ssd_post_body,
        grid=(t // tm,),
        in_specs=[tile] * 4 + [_const_spec((1, BR_WIDTH))] * 2,
        out_specs=tile,
        out_shape=jax.ShapeDtypeStruct((t, BR_WIDTH), BF16),
        compiler_params=_cparams("parallel"),
        name="ssd_post",
    )(y_f, y_b, xbc2d, z_ssd2d, d_full.reshape(1, -1), norm_g.reshape(1, -1))


def _lru_prep_body(z_ref, zp_ref, zn_ref, w_ref, b_ref, wa_ref, ba_ref, wx_ref, bx_ref, sp_ref,
                   a0_ref, u0_ref, a1_ref, u1_ref):
    cols = slice(0, LRU_WIDTH)
    prev8, next8 = _halo_rows(zp_ref, zn_ref, cols)
    xc = _centred_conv(z_ref[0, :, cols].astype(F32), prev8, next8, w_ref, b_ref)
    for d, (a_ref, u_ref) in enumerate(((a0_ref, u0_ref), (a1_ref, u1_ref))):
        g_r = jax.nn.sigmoid(_mm(xc, wa_ref[d]) + ba_ref[d])
        g_i = jax.nn.sigmoid(_mm(xc, wx_ref[d]) + bx_ref[d])
        log_a = -LRU_C * g_r * sp_ref[d]
        t = jnp.tanh(log_a)
        a_ref[0] = jnp.exp(log_a)
        u_ref[0] = (jnp.sqrt(-2.0 * t / (1.0 - t)) * (g_i * xc)).astype(u_ref.dtype)


def _block_diag(w):
    d, k, bi, bo = w.shape
    return jnp.einsum('dkij,kl->dkilj', w, jnp.eye(k, dtype=w.dtype)).reshape(d, k * bi, k * bo)


def lru_prep(z_lru, conv_w, conv_b, wa, ba, wx, bx, lam, tq):
    b, n, c = z_lru.shape
    tile = pl.BlockSpec((1, tq, LRU_WIDTH), lambda bi, i: (bi, i, 0))
    out = jax.ShapeDtypeStruct((b, n, LRU_WIDTH), F32)
    sp = _softplus(-lam)[:, None, :]
    return pl.pallas_call(
        _lru_prep_body,
        grid=(b, n // tq),
        in_specs=_halo_specs(tq, n, c) + [_const_spec((CONV_K, LRU_WIDTH)), _const_spec((1, LRU_WIDTH)),
                                           _const_spec((2, LRU_WIDTH, LRU_WIDTH)), _const_spec((2, 1, LRU_WIDTH)),
                                           _const_spec((2, LRU_WIDTH, LRU_WIDTH)), _const_spec((2, 1, LRU_WIDTH)),
                                           _const_spec((2, 1, LRU_WIDTH))],
        out_specs=[tile] * 4,
        out_shape=[out, jax.ShapeDtypeStruct((b, n, LRU_WIDTH), BF16)] * 2,
        compiler_params=_cparams("parallel", "parallel"),
        name="lru_prep",
    )(z_lru, z_lru, z_lru, conv_w, conv_b.reshape(1, -1), _block_diag(wa).astype(BF16), ba[:, None, :],
      _block_diag(wx).astype(BF16), bx[:, None, :], sp)


def _lru_tile_scan(rev, a, u, carry):
    n = a.shape[0]
    row = _iota2(a.shape, 0)
    s = 1
    while s < n:
        shift = n - s if rev else s
        keep = (row < n - s) if rev else (row >= s)
        a_sh = jnp.where(keep, pltpu.roll(a, shift, 0), 1.0)
        u_sh = jnp.where(keep, pltpu.roll(u, shift, 0), 0.0)
        u = u + a * u_sh
        a = a * a_sh
        s *= 2
    h = a * carry + u
    return h, (h[0:1] if rev else h[n - 1:n])


def _lru_scan_body(af_ref, uf_ref, ab_ref, ub_ref, h0_ref, yf_ref, yb_ref, hfin_ref, h_scr):
    c = pl.program_id(1)

    @pl.when(c == 0)
    def _():
        h_scr[...] = h0_ref[0]

    hf, cf = _lru_tile_scan(False, af_ref[0], uf_ref[0].astype(F32), h_scr[0:1])
    yf_ref[0] = hf.astype(yf_ref.dtype)
    h_scr[0:1] = cf
    hb, cb = _lru_tile_scan(True, ab_ref[0], ub_ref[0].astype(F32), h_scr[1:2])
    yb_ref[0] = hb.astype(yb_ref.dtype)
    h_scr[1:2] = cb

    @pl.when(c == pl.num_programs(1) - 1)
    def _():
        hfin_ref[0] = h_scr[...]


def lru_scan(a0, u0, a1, u1, h0, tq):
    b, n, _ = a0.shape
    nc = n // tq
    fw = pl.BlockSpec((1, tq, LRU_WIDTH), lambda bi, c: (bi, c, 0))
    bw = pl.BlockSpec((1, tq, LRU_WIDTH), lambda bi, c: (bi, nc - 1 - c, 0))
    st = pl.BlockSpec((1, 2, LRU_WIDTH), lambda bi, c: (bi, 0, 0))
    return pl.pallas_call(
        _lru_scan_body,
        grid=(b, nc),
        in_specs=[fw, fw, bw, bw, st],
        out_specs=[fw, bw, st],
        out_shape=[jax.ShapeDtypeStruct((b, n, LRU_WIDTH), BF16)] * 2 + [jax.ShapeDtypeStruct((b, 2, LRU_WIDTH), F32)],
        scratch_shapes=[pltpu.VMEM((2, LRU_WIDTH), F32)],
        compiler_params=_cparams("parallel", "arbitrary"),
        name="lru_scan",
    )(a0, u0, a1, u1, h0)


def _lru_post_body(yf_ref, yb_ref, gate_ref, o_ref):
    o_ref[...] = ((yf_ref[...].astype(F32) + yb_ref[...].astype(F32)) * _silu(gate_ref[...].astype(F32))).astype(o_ref.dtype)


def lru_post(y_f, y_b, z_lru2d, tm):
    t = y_f.shape[0]
    tile = pl.BlockSpec((tm, LRU_WIDTH), lambda i: (i, 0))
    return pl.pallas_call(
        _lru_post_body,
        grid=(t // tm,),
        in_specs=[tile, tile, pl.BlockSpec((tm, LRU_WIDTH), lambda i: (i, 1))],
        out_specs=tile,
        out_shape=jax.ShapeDtypeStruct((t, LRU_WIDTH), BF16),
        compiler_params=_cparams("parallel"),
        name="lru_post",
    )(y_f, y_b, z_lru2d)


def rope_tables(n):
    t = np.arange(n)
    pos = jnp.asarray(np.stack([t // GRID_W, t % GRID_W], axis=0).astype(np.float32))
    nf = MLA_ROPE // 4
    inv = jnp.asarray(ROPE_THETA, F32) ** (-jnp.arange(nf, dtype=F32) / nf)
    rel = np.arange(LANE) - MLA_NOPE
    in_rope = (rel >= 0) & (rel < MLA_ROPE)
    half = np.clip(rel // (2 * nf), 0, 1)
    within = rel % (2 * nf)
    is_x2 = within >= nf
    ang = pos[half, :].T * inv[within % nf][None, :]
    cos = jnp.where(in_rope[None, :], jnp.cos(ang), 1.0)
    sin = jnp.sin(ang)
    sin_a = jnp.where((in_rope & ~is_x2)[None, :], -sin, 0.0)
    sin_b = jnp.where((in_rope & is_x2)[None, :], sin, 0.0)
    return cos, sin_a, sin_b


def _head_norm_rope(x, g, rope):
    nf = MLA_ROPE // 4
    outs = []
    for h in range(MLA_HEADS):
        xh = x[:, h * LANE:(h + 1) * LANE]
        ms = jnp.sum(xh * xh, axis=-1, keepdims=True) * (1.0 / MLA_QK)
        xh = xh * lax.rsqrt(ms + EPS) * g
        if rope is not None:
            cos, sin_a, sin_b = rope
            xh = xh * cos + pltpu.roll(xh, LANE - nf, 1) * sin_a + pltpu.roll(xh, nf, 1) * sin_b
        outs.append(xh)
    return jnp.concatenate(outs, axis=1)


def _mla_q_body(use_rope, q_ref, qag_ref, wq_ref, qng_ref, *rest):
    *rope_refs, o_ref = rest
    x = q_ref[:, :MLA_Q_RANK].astype(F32)
    x = x * lax.rsqrt(jnp.mean(x * x, axis=-1, keepdims=True) + EPS) * qag_ref[...]
    q = _mm(x, wq_ref[...])
    rope = tuple(r[...] for r in rope_refs) if use_rope else None
    q = _head_norm_rope(q, qng_ref[...], rope)
    o_ref[...] = (q * (MLA_QK ** -0.5)).astype(o_ref.dtype)


def mla_q(z_mla, qa_g, wq_pad, qn_g_pad, rope, n, tm):
    t = z_mla.shape[0]
    per = n // tm
    in_specs = [pl.BlockSpec((tm, 512), lambda i: (i, MLA_Q_BLOCK)), _const_spec((1, MLA_Q_RANK)),
                _const_spec((MLA_Q_RANK, MLA_W)), _const_spec((1, LANE))]
    args = [z_mla, qa_g.reshape(1, -1), wq_pad, qn_g_pad]
    if rope is not None:
        in_specs += [pl.BlockSpec((tm, LANE), lambda i: (i % per, 0))] * 3
        args += list(rope)
    return pl.pallas_call(
        lambda *refs: _mla_q_body(rope is not None, *refs),
        grid=(t // tm,),
        in_specs=in_specs,
        out_specs=pl.BlockSpec((tm, MLA_W), lambda i: (i, 0)),
        out_shape=jax.ShapeDtypeStruct((t, MLA_W), BF16),
        compiler_params=_cparams("parallel"),
        name="mla_q",
    )(*args)


def _mla_kv_body(norm_ckv, use_rope, c_ref, kr_ref, kvag_ref, wk_ref, wv_ref, kng_ref, *rest):
    *rope_refs, cn_ref, k_ref, v_ref = rest
    c = c_ref[...].astype(F32)
    if norm_ckv:
        c = c * lax.rsqrt(jnp.mean(c * c, axis=-1, keepdims=True) + EPS) * kvag_ref[...]
    cn_ref[...] = c
    k = _mm(c, wk_ref[...]) + kr_ref[...].astype(F32)
    rope = tuple(r[...] for r in rope_refs) if use_rope else None
    k_ref[...] = _head_norm_rope(k, kng_ref[...], rope).astype(k_ref.dtype)
    v_ref[...] = _mm(c, wv_ref[...]).astype(v_ref.dtype)


def mla_kv(c_arr, c_block, kr_arr, kr_block, kva_g, wk_pad, wv, kn_g_pad, rope, norm_ckv, n, tm):
    t = c_arr.shape[0]
    per = n // tm
    in_specs = [pl.BlockSpec((tm, MLA_KV_RANK), lambda i: (i, c_block)),
                pl.BlockSpec((tm, MLA_W), lambda i: (i, kr_block)),
                _const_spec((1, MLA_KV_RANK)), _const_spec((MLA_KV_RANK, MLA_W)), _const_spec((MLA_KV_RANK, BR_WIDTH)),
                _const_spec((1, LANE))]
    args = [c_arr, kr_arr, kva_g.reshape(1, -1), wk_pad, wv, kn_g_pad]
    if rope is not None:
        in_specs += [pl.BlockSpec((tm, LANE), lambda i: (i % per, 0))] * 3
        args += list(rope)
    return pl.pallas_call(
        lambda *refs: _mla_kv_body(norm_ckv, rope is not None, *refs),
        grid=(t // tm,),
        in_specs=in_specs,
        out_specs=[pl.BlockSpec((tm, MLA_KV_RANK), lambda i: (i, 0)), pl.BlockSpec((tm, MLA_W), lambda i: (i, 0)),
                   pl.BlockSpec((tm, BR_WIDTH), lambda i: (i, 0))],
        out_shape=[jax.ShapeDtypeStruct((t, MLA_KV_RANK), F32), jax.ShapeDtypeStruct((t, MLA_W), BF16),
                   jax.ShapeDtypeStruct((t, BR_WIDTH), BF16)],
        compiler_params=_cparams("parallel"),
        name="mla_kv",
    )(*args)


def _attn_body(n_src, q_ref, gate_ref, *rest):
    kv_refs, o_ref = rest[:2 * n_src], rest[2 * n_src]
    tq = q_ref.shape[0]
    heads = (slice(0, LANE), slice(LANE, 2 * LANE))
    qs = [q_ref[:, hs] for hs in heads]

    def tile(carry, ks, v):
        ss = [lax.dot_general(q, k, (((1,), (1,)), ((), ())), preferred_element_type=F32) for q, k in zip(qs, ks)]
        m_new = [jnp.maximum(c[0], jnp.max(s, axis=-1, keepdims=True)) for c, s in zip(carry, ss)]
        ps = [jnp.exp(s - m) for s, m in zip(ss, m_new)]
        alphas = [jnp.exp(c[0] - m) for c, m in zip(carry, m_new)]
        pv = [jnp.dot(p.astype(BF16), v, preferred_element_type=F32) for p in ps]
        return tuple((m, a * c[1] + jnp.sum(p, axis=-1, keepdims=True), a * c[2] + o)
                     for c, m, a, p, o in zip(carry, m_new, alphas, ps, pv))

    init = (jnp.full((tq, 1), NEG_INF, F32), jnp.zeros((tq, 1), F32), jnp.zeros((tq, LANE), F32))
    carry = (init, init)
    for s_i in range(n_src):
        k_ref, v_ref = kv_refs[2 * s_i], kv_refs[2 * s_i + 1]
        nk = k_ref.shape[1]
        tk = min(MLA_TK, nk)

        def step(j, carry, k_ref=k_ref, v_ref=v_ref, tk=tk):
            rows = pl.ds(pl.multiple_of(j * tk, tk), tk)
            return tile(carry, [k_ref[0, rows, hs] for hs in heads], v_ref[0, rows, :])

        carry = lax.fori_loop(0, nk // tk, step, carry) if nk > tk else step(0, carry)
    accs = [acc / l for _, l, acc in carry]
    o = jnp.where(_iota2((tq, LANE), 1) < MLA_V, accs[0], accs[1])
    o_ref[...] = (o * _silu(gate_ref[...].astype(F32))).astype(o_ref.dtype)


def mla_attention(q, z_mla, kvs, n, tq):
    t = q.shape[0]
    b = t // n
    per = n // tq
    in_specs = [pl.BlockSpec((tq, 2 * LANE), lambda bi, p, i: (bi * per + i, p)),
                pl.BlockSpec((tq, LANE), lambda bi, p, i: (bi * per + i, MLA_GATE_LANE_BLOCK + p))]
    args = [q, z_mla]
    for k, v in kvs:
        nk = k.shape[1]
        in_specs += [pl.BlockSpec((1, nk, 2 * LANE), lambda bi, p, i: (bi, 0, p)),
                     pl.BlockSpec((1, nk, LANE), lambda bi, p, i: (bi, 0, p))]
        args += [k, v]
    return pl.pallas_call(
        lambda *refs: _attn_body(len(kvs), *refs),
        grid=(b, PAIRS, per),
        in_specs=in_specs,
        out_specs=pl.BlockSpec((tq, LANE), lambda bi, p, i: (bi * per + i, p)),
        out_shape=jax.ShapeDtypeStruct((t, BR_WIDTH), BF16),
        compiler_params=_cparams("parallel", "parallel", "arbitrary"),
        name="mla_attention",
    )(*args)


def pad_heads(w, real, lead_zero=0):
    shp = w.shape[:-1]
    w = w.reshape(*shp, MLA_HEADS, real)
    w = jnp.pad(w, [(0, 0)] * len(shp) + [(0, 0), (lead_zero, LANE - real - lead_zero)])
    return w.reshape(*shp, MLA_W)


def pad_lane(g):
    return jnp.pad(g.reshape(1, -1), ((0, 0), (0, LANE - g.size)))


def place_rope_key(k_rope):
    return pad_heads(jnp.tile(k_rope, (1,) * (k_rope.ndim - 1) + (MLA_HEADS,)), MLA_ROPE, MLA_NOPE)


def pack_weights(P, l):
    w = P['w_in'][l]
    col = lambda i: w[:, IN_OFFS[i]:IN_OFFS[i + 1]]
    zeros = lambda n: jnp.zeros((D_MODEL, n), w.dtype)
    mla_kv = col(3)
    w_rw = jnp.concatenate([col(1), col(0)], axis=1)
    w_mla = jnp.concatenate([place_rope_key(mla_kv[:, MLA_KV_RANK:]), col(4), col(2), zeros(512 - MLA_Q_RANK),
                             mla_kv[:, :MLA_KV_RANK]], axis=1)
    w_ssd = jnp.concatenate([col(5), col(6), col(7), zeros(LANE - 2 * SSD_HEADS)], axis=1)
    w_lru = jnp.concatenate([col(8), col(9)], axis=1)
    kvu = P['mla_kv_up'][l].reshape(MLA_KV_RANK, MLA_HEADS, MLA_NOPE + MLA_V)
    bf = lambda x: x.astype(BF16)
    return dict(w_rw=bf(w_rw), w_mla=bf(w_mla), w_ssd=bf(w_ssd), w_lru=bf(w_lru), w_mrg=bf(col(10)),
                wq=bf(pad_heads(P['mla_q_up'][l], MLA_QK)),
                wk=bf(pad_heads(kvu[:, :, :MLA_NOPE].reshape(MLA_KV_RANK, -1), MLA_NOPE)),
                wv=bf(kvu[:, :, MLA_NOPE:].reshape(MLA_KV_RANK, -1)),
                w_branch=bf(P['w_branch'][l]), w_out=bf(P['w_out'][l]))


def trunk_layer(x, mod, P, W, l, ctx):
    b, n, d = x.shape
    t = b * n
    tm = min(n, 512)
    tq = 256
    shift, scale, gate = mod
    x2d = x.reshape(t, d)
    h = prenorm(x2d, P['norm_g'][l], scale, shift, n, tm)
    z_rw = matmul(h, W['w_rw'], tm, Z_RW_W, BF16)
    z_mla = matmul(h, W['w_mla'], tm, Z_MLA_W, BF16)
    z_ssd = matmul(h, W['w_ssd'], tm, Z_SSD_W, F32)
    z_lru = matmul(h, W['w_lru'], tm, Z_LRU_W, BF16)
    logits = matmul(h, W['w_mrg'], tm, 2048, BF16)
    if ctx is None:
        rw0 = jnp.zeros((b, 2, PAIRS, LANE, LANE), F32)
        ssd0 = jnp.zeros((b, 2, PAIRS, LANE, LANE), F32)
        lru0 = jnp.zeros((b, 2, LRU_WIDTH), F32)
        rope = None
    else:
        ckv0, krope0, rw0, ssd0, lru0 = ctx
        rw0, ssd0 = rw_state_to_pairs(rw0), ssd_state_to_pairs(ssd0)
        rope = rope_tables(n)

    prep = rw_prep(z_rw.reshape(b, n, -1), P, l, tq)
    o_f, o_b, rw_fin = rw_scan(prep, rw0)
    flat = lambda a: a.reshape(t, -1)
    br_rw = rw_post(flat(o_f), flat(o_b), flat(prep[9]), z_rw, P['rw_ln_g'][l], P['rw_ln_b'][l], tm)

    qn_g, kn_g = pad_lane(P['mla_qn_g'][l]), pad_lane(P['mla_kn_g'][l])
    q = mla_q(z_mla, P['mla_qa_g'][l], W['wq'], qn_g, rope, n, tm)
    c_kv, k, v = mla_kv(z_mla, MLA_CKV_BLOCK, z_mla, 0, P['mla_kva_g'][l], W['wk'], W['wv'], kn_g, rope, True, n, tm)
    kvs = [(k.reshape(b, n, -1), v.reshape(b, n, -1))]
    if ctx is not None:
        past = ckv0.shape[1]
        _, k0, v0 = mla_kv(ckv0.reshape(b * past, -1), 0, place_rope_key(krope0).reshape(b * past, -1), 0,
                           P['mla_kva_g'][l], W['wk'], W['wv'], kn_g, None, False, past, past)
        kvs = [(k0.reshape(b, past, -1), v0.reshape(b, past, -1))] + kvs
    br_mla = mla_attention(q, z_mla, kvs, n, min(n, 256))
    k_rope = z_mla[:, MLA_NOPE:MLA_NOPE + MLA_ROPE].astype(F32)

    xbc, dtp = ssd_prep(z_ssd.reshape(b, n, -1), P['ssd_conv_w'][l], P['ssd_conv_b'][l], P['ssd_dt_bias'][l],
                        P['ssd_a_log'][l], tq)
    y_f, y_b, ssd_fin = ssd_scan(xbc, dtp, ssd0)
    br_ssd = ssd_post(flat(y_f), flat(y_b), flat(xbc), z_ssd, jnp.repeat(P['ssd_d'][l], SSD_HEAD_DIM),
                      P['ssd_norm_g'][l], tm)

    a0, u0, a1, u1 = lru_prep(z_lru.reshape(b, n, -1), P['lru_conv_w'][l], P['lru_conv_b'][l], P['lru_wa'][l],
                              P['lru_ba'][l], P['lru_wx'][l], P['lru_bx'][l], P['lru_lambda'][l], tq)
    h_f, h_b, lru_fin = lru_scan(a0, u0, a1, u1, lru0, tq)
    br_lru = lru_post(flat(h_f), flat(h_b), z_lru, tm)

    y = merge(x2d, [br_rw, br_mla, br_ssd, br_lru], logits, W['w_branch'], W['w_out'], gate, n, tm)
    states = (c_kv.reshape(b, n, -1), k_rope.reshape(b, n, -1), rw_pairs_to_state(rw_fin),
              ssd_pairs_to_state(ssd_fin), lru_fin)
    return y.reshape(b, n, d), states


def kernel(x_prompt, x_sample, cache_mla_ckv, cache_mla_krope, state_rwkv, state_ssd, state_lru, c, c_ctx, ada_w, ada_b, norm_g, w_in, rw_mu, rw_w0, rw_w2, rw_a0, rw_a2, rw_kk, rw_ka, rw_rk, rw_ln_g, rw_ln_b, mla_qa_g, mla_q_up, mla_kva_g, mla_kv_up, mla_qn_g, mla_kn_g, ssd_conv_w, ssd_conv_b, ssd_dt_bias, ssd_a_log, ssd_d, ssd_norm_g, lru_conv_w, lru_conv_b, lru_wa, lru_ba, lru_wx, lru_bx, lru_lambda, w_branch, w_out):
    P = dict(norm_g=norm_g, w_in=w_in,
             rw_mu=rw_mu, rw_w0=rw_w0, rw_w2=rw_w2, rw_a0=rw_a0, rw_a2=rw_a2, rw_kk=rw_kk,
             rw_ka=rw_ka, rw_rk=rw_rk.reshape(DEPTH, BR_WIDTH), rw_ln_g=rw_ln_g, rw_ln_b=rw_ln_b,
             mla_qa_g=mla_qa_g, mla_q_up=mla_q_up, mla_kva_g=mla_kva_g, mla_kv_up=mla_kv_up,
             mla_qn_g=mla_qn_g, mla_kn_g=mla_kn_g,
             ssd_conv_w=ssd_conv_w, ssd_conv_b=ssd_conv_b, ssd_dt_bias=ssd_dt_bias,
             ssd_a_log=ssd_a_log, ssd_d=ssd_d, ssd_norm_g=ssd_norm_g,
             lru_conv_w=lru_conv_w, lru_conv_b=lru_conv_b, lru_wa=lru_wa, lru_ba=lru_ba,
             lru_wx=lru_wx, lru_bx=lru_bx, lru_lambda=lru_lambda,
             w_branch=w_branch, w_out=w_out)
    dec_batch = c.shape[0]
    cond8 = jnp.concatenate([c_ctx[None, :], c, jnp.zeros((SUBLANE - 1 - dec_batch, D_MODEL), F32)], axis=0)
    weights, mods = [], []
    for l in range(DEPTH):
        weights.append(pack_weights(P, l))
        mod = ada_mod(cond8, ada_w[l], ada_b[l])
        mods.append(tuple(jnp.split(mod[:, None, :], 3, axis=-1)))

    y_prompt = x_prompt
    ctx_states = []
    for l in range(DEPTH):
        y_prompt, st = trunk_layer(y_prompt, tuple(m[0:1] for m in mods[l]), P, weights[l], l, None)
        ctx_states.append(st)
    new_states = tuple(jnp.stack([s[i] for s in ctx_states], axis=1) for i in range(5))

    y_sample = x_sample
    for l in range(DEPTH):
        ctx = (cache_mla_ckv[:, l], cache_mla_krope[:, l], state_rwkv[:, l], state_ssd[:, l], state_lru[:, l])
        y_sample, _ = trunk_layer(y_sample, tuple(m[1:1 + dec_batch] for m in mods[l]), P, weights[l], l, ctx)

    return (y_prompt, y_sample) + new_states
```

```python
import math

import jax
import jax.numpy as jnp
import numpy as np
from jax import lax
from jax.experimental import pallas as pl
from jax.experimental.pallas import tpu as pltpu

F32 = jnp.float32
BF16 = jnp.bfloat16
HIGHEST = lax.Precision.HIGHEST

D_MODEL = 1024
DEPTH = 2
GRID_W = 64
N_BRANCH = 4
BR_WIDTH = 512
EPS = 1e-6
CONV_K = 4

RW_HEADS = 8
RW_HEAD = 64
RW_LORA = 64
RW_GN_EPS = 64e-5
RW_PRE = 3 * BR_WIDTH + 2 * RW_LORA

MLA_HEADS = 8
MLA_NOPE = 64
MLA_ROPE = 32
MLA_QK = MLA_NOPE + MLA_ROPE
MLA_V = 64
MLA_Q_RANK = 384
MLA_KV_RANK = 256
ROPE_THETA = 10000.0

SSD_HEADS = 8
SSD_HEAD_DIM = 64
SSD_GROUPS = 2
SSD_STATE = 64
SSD_CONV_DIM = BR_WIDTH + 2 * SSD_GROUPS * SSD_STATE

LRU_WIDTH = 512
LRU_BLOCKS = 8
LRU_C = 8.0

IN_WIDTHS = (RW_PRE, BR_WIDTH, MLA_Q_RANK, MLA_KV_RANK + MLA_ROPE, BR_WIDTH, BR_WIDTH, SSD_CONV_DIM,
             2 * SSD_HEADS, LRU_WIDTH, LRU_WIDTH, N_BRANCH * D_MODEL)
IN_OFFS = tuple(int(v) for v in np.cumsum((0,) + IN_WIDTHS))

LANE = 128
SUBLANE = 8
HALO = 16
VMEM_LIMIT = 48 * 1024 * 1024

RW_CHUNK = 64
PAIRS = BR_WIDTH // LANE
SSD_Q = 128
MLA_W = MLA_HEADS * LANE
ATTN_TQ = 512
ATTN_SPLIT = 2
ATTN_ROWS = 16
ATTN_COLS = 640

Z_RW_W = BR_WIDTH + RW_PRE
Z_MLA_W = MLA_W + BR_WIDTH + 512 + MLA_KV_RANK
Z_SSD_W = BR_WIDTH + SSD_CONV_DIM + LANE
Z_LRU_W = 2 * LRU_WIDTH
MLA_GATE_LANE_BLOCK = MLA_W // LANE
MLA_Q_BLOCK = (MLA_W + BR_WIDTH) // 512
MLA_CKV_BLOCK = (MLA_W + BR_WIDTH + 512) // MLA_KV_RANK


def _cparams(*sem):
    return pltpu.CompilerParams(dimension_semantics=sem, vmem_limit_bytes=VMEM_LIMIT)


def _const_spec(shape):
    return pl.BlockSpec(shape, lambda *_: (0,) * len(shape))


def _mm(a, b):
    return jnp.dot(a.astype(BF16), b.astype(BF16), preferred_element_type=F32)


def _mm_nt(a, b):
    return lax.dot_general(a.astype(BF16), b.astype(BF16), (((1,), (1,)), ((), ())), preferred_element_type=F32)


def _mm_f32(a, b):
    return jnp.dot(a, b, preferred_element_type=F32, precision=HIGHEST)


def _iota2(shape, dim):
    return lax.broadcasted_iota(jnp.int32, shape, dim)


def _silu(x):
    return x * jax.nn.sigmoid(x)


def _softplus(x):
    return jnp.maximum(x, 0.0) + jnp.log(1.0 + jnp.exp(-jnp.abs(x)))


def _in_lockstep(*gens):
    results = [None] * len(gens)
    live = list(range(len(gens)))
    while live:
        for i in list(live):
            try:
                next(gens[i])
            except StopIteration as done:
                results[i] = done.value
                live.remove(i)
    return results


def _head_sum(x, width):
    bd = ((_iota2((LANE, LANE), 0) // width) == (_iota2((LANE, LANE), 1) // width)).astype(BF16)
    hi = x.astype(BF16)
    lo = (x - hi.astype(F32)).astype(BF16)
    outs = []
    for j in range(x.shape[-1] // LANE):
        sl = slice(j * LANE, (j + 1) * LANE)
        outs.append(jnp.dot(hi[:, sl], bd, preferred_element_type=F32) + jnp.dot(lo[:, sl], bd, preferred_element_type=F32))
    return jnp.concatenate(outs, axis=1)


def _from_prev(u, s, prev8):
    rolled = pltpu.roll(u, s, 0)
    head = jnp.where(_iota2(prev8.shape, 0) < s, pltpu.roll(prev8, s, 0), rolled[:SUBLANE])
    return jnp.concatenate([head, rolled[SUBLANE:]], axis=0)


def _from_next(u, next8):
    n = u.shape[0]
    rolled = pltpu.roll(u, n - 1, 0)
    tail = jnp.where(_iota2(next8.shape, 0) == SUBLANE - 1, pltpu.roll(next8, SUBLANE - 1, 0), rolled[n - SUBLANE:])
    return jnp.concatenate([rolled[:n - SUBLANE], tail], axis=0)


def _centred_conv(x, prev8, next8, w_ref, b_ref):
    return (w_ref[0:1, :] * _from_prev(x, 2, prev8) + w_ref[1:2, :] * _from_prev(x, 1, prev8)
            + w_ref[2:3, :] * x + w_ref[3:4, :] * _from_next(x, next8) + b_ref[...])


def _halo_specs(tq, n, c):
    per, nb = tq // HALO, n // HALO
    return [pl.BlockSpec((1, tq, c), lambda bi, i: (bi, i, 0)),
            pl.BlockSpec((1, HALO, c), lambda bi, i: (bi, jnp.maximum(i * per - 1, 0), 0)),
            pl.BlockSpec((1, HALO, c), lambda bi, i: (bi, jnp.minimum((i + 1) * per, nb - 1), 0))]


def _halo_rows(zp_ref, zn_ref, cols):
    i = pl.program_id(1)
    prev8 = jnp.where(i > 0, zp_ref[0, HALO - SUBLANE:, cols].astype(F32), 0.0)
    next8 = jnp.where(i < pl.num_programs(1) - 1, zn_ref[0, :SUBLANE, cols].astype(F32), 0.0)
    return prev8, next8


def _ada_body(c_ref, w_ref, b_ref, o_ref):
    o_ref[...] = _mm(_silu(c_ref[...]), w_ref[...]) + b_ref[...]


def ada_mod(cond8, w, bias):
    d = cond8.shape[1]
    return pl.pallas_call(
        _ada_body,
        grid=(w.shape[1] // d,),
        in_specs=[_const_spec((SUBLANE, d)), pl.BlockSpec((d, d), lambda j: (0, j)), pl.BlockSpec((1, d), lambda j: (0, j))],
        out_specs=pl.BlockSpec((SUBLANE, d), lambda j: (0, j)),
        out_shape=jax.ShapeDtypeStruct((SUBLANE, w.shape[1]), F32),
        compiler_params=_cparams("parallel"),
        name="ada_mod",
    )(cond8, w, bias.reshape(1, -1))


def _prenorm_body(x_ref, g_ref, sc_ref, sh_ref, h_ref):
    x = x_ref[...]
    y = x * lax.rsqrt(jnp.mean(x * x, axis=-1, keepdims=True) + EPS)
    h_ref[...] = ((y * g_ref[...]) * (1.0 + sc_ref[0]) + sh_ref[0]).astype(h_ref.dtype)


def _mod_index(bm, per):
    return (lambda i: (i // per, 0, 0)) if bm > 1 else (lambda i: (0, 0, 0))


def prenorm(x2d, g, scale, shift, n, tm):
    t, d = x2d.shape
    mod_idx = _mod_index(scale.shape[0], n // tm)
    return pl.pallas_call(
        _prenorm_body,
        grid=(t // tm,),
        in_specs=[pl.BlockSpec((tm, d), lambda i: (i, 0)), _const_spec((1, d)),
                  pl.BlockSpec((1, 1, d), mod_idx), pl.BlockSpec((1, 1, d), mod_idx)],
        out_specs=pl.BlockSpec((tm, d), lambda i: (i, 0)),
        out_shape=jax.ShapeDtypeStruct((t, d), BF16),
        compiler_params=_cparams("parallel"),
        name="prenorm",
    )(x2d, g.reshape(1, -1), scale, shift)


def _mm_body(a_ref, w_ref, o_ref):
    o_ref[...] = jnp.dot(a_ref[...], w_ref[...], preferred_element_type=F32).astype(o_ref.dtype)


def matmul(a, w, tm, tn, out_dtype):
    t, k = a.shape
    n = w.shape[1]
    return pl.pallas_call(
        _mm_body,
        grid=(n // tn, t // tm),
        in_specs=[pl.BlockSpec((tm, k), lambda j, i: (i, 0)), pl.BlockSpec((k, tn), lambda j, i: (0, j))],
        out_specs=pl.BlockSpec((tm, tn), lambda j, i: (i, j)),
        out_shape=jax.ShapeDtypeStruct((t, n), out_dtype),
        compiler_params=_cparams("parallel", "parallel"),
        name="in_proj",
    )(a, w)


def _merge_body(x_ref, b0_ref, b1_ref, b2_ref, b3_ref, lg_ref, wb_ref, wo_ref, gate_ref, y_ref):
    merged = None
    for m, b_ref in enumerate((b0_ref, b1_ref, b2_ref, b3_ref)):
        proj = jnp.dot(b_ref[...], wb_ref[m], preferred_element_type=F32)
        g = jax.nn.sigmoid(lg_ref[:, m * D_MODEL:(m + 1) * D_MODEL].astype(F32))
        merged = g * proj if merged is None else merged + g * proj
    out = jnp.dot(merged.astype(BF16), wo_ref[...], preferred_element_type=F32)
    y_ref[...] = x_ref[...] + gate_ref[0] * out


def merge(x2d, branches, logits, wb, wo, gate, n, tm):
    t, d = x2d.shape
    mod_idx = _mod_index(gate.shape[0], n // tm)
    row = lambda i: (i, 0)
    return pl.pallas_call(
        _merge_body,
        grid=(t // tm,),
        in_specs=[pl.BlockSpec((tm, d), row)] + [pl.BlockSpec((tm, BR_WIDTH), row)] * N_BRANCH
        + [pl.BlockSpec((tm, N_BRANCH * d), row), _const_spec((N_BRANCH, BR_WIDTH, d)), _const_spec((d, d)),
           pl.BlockSpec((1, 1, d), mod_idx)],
        out_specs=pl.BlockSpec((tm, d), row),
        out_shape=jax.ShapeDtypeStruct((t, d), F32),
        compiler_params=_cparams("parallel"),
        name="merge",
    )(x2d, *branches, logits, wb, wo, gate)


def _rw_prep_body(z_ref, zp_ref, zn_ref, mu_ref, kkg_ref, ka_ref, rk_ref, w0_ref, w2_ref, a0_ref, a2_ref,
                  r_ref, kk_ref, v_ref, lw0_ref, k0_ref, b0_ref, lw1_ref, k1_ref, b1_ref, bonus_ref):
    cols = slice(BR_WIDTH, Z_RW_W)
    prev8, next8 = _halo_rows(zp_ref, zn_ref, cols)
    u = z_ref[0, :, cols].astype(F32)
    u = u + mu_ref[...] * (0.5 * (_from_prev(u, 1, prev8) + _from_next(u, next8)) - u)
    r = u[:, 0:BR_WIDTH]
    k = u[:, BR_WIDTH:2 * BR_WIDTH]
    v = u[:, 2 * BR_WIDTH:3 * BR_WIDTH]
    lo = u[:, 3 * BR_WIDTH:]
    lo = jnp.where(_iota2(lo.shape, 1) < RW_LORA, jnp.tanh(lo), lo)
    kk = k * kkg_ref[...]
    kk = kk * lax.rsqrt(_head_sum(kk * kk, RW_HEAD) + 1e-12)
    r_ref[0] = r.astype(r_ref.dtype)
    kk_ref[0] = kk.astype(kk_ref.dtype)
    v_ref[0] = v.astype(v_ref.dtype)
    coef = jnp.zeros_like(r)
    lo_hi = lo.astype(BF16)
    lo_lo = (lo - lo_hi.astype(F32)).astype(BF16)
    lora = lambda w_ref, d: (jnp.dot(lo_hi, w_ref[d, 0], preferred_element_type=F32)
                             + jnp.dot(lo_hi, w_ref[d, 1], preferred_element_type=F32)
                             + jnp.dot(lo_lo, w_ref[d, 0], preferred_element_type=F32))
    for d, (lw_ref, k_ref, b_ref) in enumerate(((lw0_ref, k0_ref, b0_ref), (lw1_ref, k1_ref, b1_ref))):
        w_raw = w0_ref[d] + lora(w2_ref, d)
        lw_ref[0] = -math.exp(-0.5) * jax.nn.sigmoid(w_raw)
        a = jax.nn.sigmoid(a0_ref[d] + lora(a2_ref, d))
        k_d = k * (1.0 + (a - 1.0) * ka_ref[...])
        k_ref[0] = k_d.astype(k_ref.dtype)
        b_ref[0] = (kk * a).astype(b_ref.dtype)
        coef = coef + _head_sum(r * k_d * rk_ref[...], RW_HEAD)
    bonus_ref[0] = coef * v


def rw_prep(z_rw, P, l, tq):
    b, n, c = z_rw.shape
    w2 = jnp.concatenate([P['rw_w2'][l], jnp.zeros_like(P['rw_w2'][l])], axis=1)
    a2 = jnp.concatenate([jnp.zeros_like(P['rw_a2'][l]), P['rw_a2'][l]], axis=1)
    row = lambda x: x.reshape(1, -1)
    tile = pl.BlockSpec((1, tq, BR_WIDTH), lambda bi, i: (bi, i, 0))
    out = lambda dt: jax.ShapeDtypeStruct((b, n, BR_WIDTH), dt)
    return pl.pallas_call(
        _rw_prep_body,
        grid=(b, n // tq),
        in_specs=_halo_specs(tq, n, c) + [_const_spec((1, RW_PRE))] + [_const_spec((1, BR_WIDTH))] * 3
        + [_const_spec((2, 1, BR_WIDTH)), _const_spec((2, 2, LANE, BR_WIDTH))] * 2,
        out_specs=[tile] * 10,
        out_shape=[out(BF16)] * 3 + [out(F32), out(BF16), out(BF16)] * 2 + [out(F32)],
        compiler_params=_cparams("parallel", "parallel"),
        name="rw_prep",
    )(z_rw, z_rw, z_rw, row(P['rw_mu'][l]), row(P['rw_kk'][l]), row(P['rw_ka'][l]), row(P['rw_rk'][l]),
      P['rw_w0'][l][:, None, :], _split_bf16(w2), P['rw_a0'][l][:, None, :], _split_bf16(a2))


def _split_bf16(w):
    hi = w.astype(BF16)
    return jnp.stack([hi, (w - hi.astype(F32)).astype(BF16)], axis=1)


def _rw_masks(rev):
    n = 2 * RW_CHUNK
    ri, ci = _iota2((n, n), 0), _iota2((n, n), 1)
    same = (ri // RW_CHUNK) == (ci // RW_CHUNK)
    strict = same & ((ci > ri) if rev else (ci < ri))
    incl = same & ((ci >= ri) if rev else (ci <= ri))
    blk8 = strict & ((ri // 8) == (ci // 8))
    levels = [strict & ((ri // (2 * s)) == (ci // (2 * s))) & ((ri // s) != (ci // s)) for s in (8, 16, 32)]
    eye = (ri == ci).astype(F32)
    t_i, s_i = _iota2((RW_CHUNK, RW_CHUNK), 0), _iota2((RW_CHUNK, RW_CHUNK), 1)
    tri = ((s_i >= t_i) if rev else (s_i <= t_i)).astype(F32)
    m0 = (_iota2((RW_CHUNK, LANE), 1) < RW_HEAD).astype(F32)
    return dict(strict=strict, incl=incl, blk8=blk8, levels=levels, eye=eye, tri=tri, m0=m0, m1=1.0 - m0)


def _rw_chunks(chains):
    n = 2 * RW_CHUNK
    each = lambda f, *cols: [f(*a) for a in zip(*cols)]
    revs, mks = [c[0] for c in chains], [c[1] for c in chains]
    lws, rs_, kks, vs_, ks_, bs_, ss = ([c[i] for c in chains] for i in range(2, 9))
    stack = lambda mk, x: jnp.concatenate([x * mk['m0'], x * mk['m1']], axis=0)
    cums = each(lambda mk, lw: _mm_f32(mk['tri'], lw), mks, lws)
    totals = each(lambda rev, cum: cum[0:1] if rev else cum[RW_CHUNK - 1:RW_CHUNK], revs, cums)
    e_negs = each(lambda cum: jnp.exp(-cum), cums)
    e_tots = each(lambda cum, tot: jnp.exp(tot - cum), cums, totals)
    rs = each(lambda mk, r, cum: stack(mk, r * jnp.exp(cum)), mks, rs_, cums)
    ks = each(lambda mk, kk, cum, lw: stack(mk, kk * jnp.exp(cum - lw)), mks, kks, cums, lws)
    vs = each(stack, mks, vs_)
    both = lambda mk, b, k, e: jnp.concatenate([stack(mk, b * e), stack(mk, k * e)], axis=0)
    rhs = each(both, mks, bs_, ks_, e_negs)
    hat = each(both, mks, bs_, ks_, e_tots)
    g1 = each(_mm_nt, ks, rhs)
    g2 = each(_mm_nt, rs, rhs)
    a_b = each(lambda mk, g: jnp.where(mk['strict'], g[:, :n], 0.0), mks, g1)
    a_k = each(lambda mk, g: jnp.where(mk['strict'], g[:, n:], 0.0), mks, g1)
    m_b = each(lambda mk, g: jnp.where(mk['incl'], g[:, :n], 0.0), mks, g2)
    m_k = each(lambda mk, g: jnp.where(mk['incl'], g[:, n:], 0.0), mks, g2)
    x = each(lambda mk, a: -jnp.where(mk['blk8'], a, 0.0), mks, a_b)
    x2 = each(_mm, x, x)
    akv = each(_mm, a_k, vs)
    t_inv = each(lambda mk, x: mk['eye'] + x, mks, x)
    x4 = each(_mm, x2, x2)
    t_inv = each(lambda t, x2: t + _mm(t, x2), t_inv, x2)
    t_inv = each(lambda t, x4: t + _mm(t, x4), t_inv, x4)
    for li in range(3):
        tl = each(lambda mk, t, a: _mm(t, jnp.where(mk['levels'][li], a, 0.0)), mks, t_inv, a_b)
        t_inv = each(lambda t, tl: t - _mm(tl, t), t_inv, tl)
    zr = each(lambda ks, rs, s: _mm_nt(jnp.concatenate([ks, rs], axis=0), s), ks, rs, ss)
    u = each(lambda t, zr, akv: -_mm(t, zr[:n] + akv), t_inv, zr, akv)
    s_new = each(lambda s, tot, u, vs, hat: s * jnp.exp(tot) + _mm(jnp.concatenate([u, vs], axis=0).T, hat),
                 ss, totals, u, vs, hat)
    o_s = each(lambda zr, m_b, m_k, u, vs: zr[n:] + _mm(jnp.concatenate([m_b, m_k], axis=1), jnp.concatenate([u, vs], axis=0)),
               zr, m_b, m_k, u, vs)
    return each(lambda o_s: o_s[:RW_CHUNK] + o_s[RW_CHUNK:], o_s), s_new


def _rw_scan_body(rf_ref, kkf_ref, vf_ref, lwf_ref, kf_ref, bf_ref, rb_ref, kkb_ref, vb_ref, lwb_ref, kb_ref, bb_ref,
                  s0_ref, of_ref, ob_ref, sfin_ref, s_scr):
    c = pl.program_id(1)

    @pl.when(c == 0)
    def _():
        s_scr[...] = s0_ref[0]

    chains, dests = [], []
    for d, refs, o_ref in ((0, (lwf_ref, rf_ref, kkf_ref, vf_ref, kf_ref, bf_ref), of_ref),
                           (1, (lwb_ref, rb_ref, kkb_ref, vb_ref, kb_ref, bb_ref), ob_ref)):
        mk = _rw_masks(rev=(d == 1))
        for p in range(PAIRS):
            sl = slice(p * LANE, (p + 1) * LANE)
            chains.append((d == 1, mk) + tuple(ref[0, :, sl].astype(F32) for ref in refs) + (s_scr[d, p],))
            dests.append((o_ref, sl, d, p))
    outs, states = _rw_chunks(chains)
    for (o_ref, sl, d, p), o, s_new in zip(dests, outs, states):
        o_ref[0, :, sl] = o.astype(o_ref.dtype)
        s_scr[d, p] = s_new

    @pl.when(c == pl.num_programs(1) - 1)
    def _():
        sfin_ref[0] = s_scr[...]


def rw_scan(prep, s0_pairs):
    r, kk, v, lw0, k0, b0, lw1, k1, b1, _ = prep
    b, n, _ = r.shape
    nc = n // RW_CHUNK
    fwd = pl.BlockSpec((1, RW_CHUNK, BR_WIDTH), lambda bi, c: (bi, c, 0))
    bwd = pl.BlockSpec((1, RW_CHUNK, BR_WIDTH), lambda bi, c: (bi, nc - 1 - c, 0))
    st = pl.BlockSpec((1, 2, PAIRS, LANE, LANE), lambda bi, c: (bi, 0, 0, 0, 0))
    return pl.pallas_call(
        _rw_scan_body,
        grid=(b, nc),
        in_specs=[fwd] * 6 + [bwd] * 6 + [st],
        out_specs=[fwd, bwd, st],
        out_shape=[jax.ShapeDtypeStruct((b, n, BR_WIDTH), BF16)] * 2
        + [jax.ShapeDtypeStruct((b, 2, PAIRS, LANE, LANE), F32)],
        scratch_shapes=[pltpu.VMEM((2, PAIRS, LANE, LANE), F32)],
        compiler_params=_cparams("parallel", "arbitrary"),
        name="rw_scan",
    )(r, kk, v, lw0, k0, b0, r, kk, v, lw1, k1, b1, s0_pairs)


def rw_state_to_pairs(s0):
    b = s0.shape[0]
    s = s0.reshape(b, 2, PAIRS, 2, RW_HEAD, RW_HEAD)
    z = jnp.zeros_like(s[:, :, :, 0])
    top = jnp.concatenate([s[:, :, :, 0], z], axis=-1)
    bot = jnp.concatenate([z, s[:, :, :, 1]], axis=-1)
    return jnp.concatenate([top, bot], axis=-2)


def rw_pairs_to_state(sp):
    b = sp.shape[0]
    h0 = sp[:, :, :, :RW_HEAD, :RW_HEAD]
    h1 = sp[:, :, :, RW_HEAD:, RW_HEAD:]
    return jnp.stack([h0, h1], axis=3).reshape(b, 2, RW_HEADS, RW_HEAD, RW_HEAD)


def _rw_post_body(of_ref, ob_ref, bonus_ref, gate_ref, g_ref, bias_ref, o_ref):
    wkv = of_ref[...].astype(F32) + ob_ref[...].astype(F32)
    mu = _head_sum(wkv, RW_HEAD) * (1.0 / RW_HEAD)
    xc = wkv - mu
    var = _head_sum(xc * xc, RW_HEAD) * (1.0 / RW_HEAD)
    o = xc * lax.rsqrt(var + RW_GN_EPS) * g_ref[...] + bias_ref[...] + bonus_ref[...]
    o_ref[...] = (o * _silu(gate_ref[...].astype(F32))).astype(o_ref.dtype)


def rw_post(o_f, o_b, bonus, z_rw2d, ln_g, ln_b, tm):
    t = o_f.shape[0]
    tile = pl.BlockSpec((tm, BR_WIDTH), lambda i: (i, 0))
    return pl.pallas_call(
        _rw_post_body,
        grid=(t // tm,),
        in_specs=[tile] * 4 + [_const_spec((1, BR_WIDTH))] * 2,
        out_specs=tile,
        out_shape=jax.ShapeDtypeStruct((t, BR_WIDTH), BF16),
        compiler_params=_cparams("parallel"),
        name="rw_post",
    )(o_f, o_b, bonus, z_rw2d, ln_g.reshape(1, -1), ln_b.reshape(1, -1))


def _ssd_prep_body(z_ref, zp_ref, zn_ref, w_ref, b_ref, dtb_ref, nega_ref, xbc_ref, dtp_ref):
    cols = slice(BR_WIDTH, BR_WIDTH + SSD_CONV_DIM)
    prev8, next8 = _halo_rows(zp_ref, zn_ref, cols)
    xbc_ref[0] = _silu(_centred_conv(z_ref[0, :, cols], prev8, next8, w_ref, b_ref)).astype(xbc_ref.dtype)
    dt = _softplus(z_ref[0, :, BR_WIDTH + SSD_CONV_DIM:] + dtb_ref[...])
    dta = pltpu.roll(dt, 2 * SSD_HEADS, 1) * nega_ref[...]
    lane = _iota2(dt.shape, 1)
    dtp_ref[0] = jnp.where(lane < 2 * SSD_HEADS, dt, jnp.where(lane < 4 * SSD_HEADS, dta, 0.0))


def ssd_prep(z_ssd, conv_w, conv_b, dt_bias, a_log, tq):
    b, n, c = z_ssd.shape
    dtb = jnp.pad(dt_bias.reshape(1, -1), ((0, 0), (0, LANE - 2 * SSD_HEADS)))
    nega = jnp.pad(-jnp.exp(a_log).reshape(1, -1), ((0, 0), (2 * SSD_HEADS, LANE - 4 * SSD_HEADS)))
    return pl.pallas_call(
        _ssd_prep_body,
        grid=(b, n // tq),
        in_specs=_halo_specs(tq, n, c) + [_const_spec((CONV_K, SSD_CONV_DIM)), _const_spec((1, SSD_CONV_DIM)),
                                           _const_spec((1, LANE)), _const_spec((1, LANE))],
        out_specs=[pl.BlockSpec((1, tq, SSD_CONV_DIM), lambda bi, i: (bi, i, 0)),
                   pl.BlockSpec((1, tq, LANE), lambda bi, i: (bi, i, 0))],
        out_shape=[jax.ShapeDtypeStruct((b, n, SSD_CONV_DIM), BF16), jax.ShapeDtypeStruct((b, n, LANE), F32)],
        compiler_params=_cparams("parallel", "parallel"),
        name="ssd_prep",
    )(z_ssd, z_ssd, z_ssd, conv_w, conv_b.reshape(1, -1), dtb, nega)


def _ssd_chunk(rev, d, xbc, dtp, st_ref):
    q = SSD_Q
    li, si = _iota2((q, q), 0), _iota2((q, q), 1)
    incl = (si >= li) if rev else (si <= li)
    tri = incl.astype(F32)
    head0 = _iota2((q, LANE), 1) < SSD_HEAD_DIM
    m0 = head0.astype(F32)
    m1 = 1.0 - m0
    dt_lane, a_lane = d * SSD_HEADS, (2 + d) * SSD_HEADS

    def per_head_lanes(arr, lane0):
        cols = [arr[:, lane0 + h:lane0 + h + 1] for h in range(SSD_HEADS)]
        return jnp.concatenate([jnp.where(head0, cols[2 * p], cols[2 * p + 1]) for p in range(PAIRS)], axis=1)

    cum = _mm_f32(tri, dtp)
    yield
    acs_rows = cum.T[a_lane:a_lane + SSD_HEADS, :]
    acs = per_head_lanes(cum, a_lane)
    a_tot = acs[0:1] if rev else acs[q - 1:q]
    x, bm, cm = xbc[:, :BR_WIDTH], xbc[:, BR_WIDTH:BR_WIDTH + LANE], xbc[:, BR_WIDTH + LANE:]
    xd = x * per_head_lanes(dtp, dt_lane)
    e_acs = jnp.exp(acs)
    xdec = xd * jnp.exp(a_tot - acs)
    e_tot = jnp.exp(a_tot)
    yield
    ys = []
    for g in range(SSD_GROUPS):
        mg = m0 if g == 0 else m1
        cb = _mm_nt(cm * mg, bm)
        bm_t = (bm * mg).T
        yield
        for p in range(2 * g, 2 * g + 2):
            sl = slice(p * LANE, (p + 1) * LANE)
            ms = []
            for hh in range(2):
                h = 2 * p + hh
                col = cum[:, a_lane + h:a_lane + h + 1]
                ms.append(jnp.where(incl, cb * jnp.exp(jnp.minimum(col - acs_rows[h:h + 1, :], 0.0)), 0.0))
            xs = xd[:, sl]
            yield
            y = _mm(jnp.concatenate(ms, axis=1), jnp.concatenate([xs * m0, xs * m1], axis=0))
            y = y + _mm(cm, st_ref[p]) * e_acs[:, sl]
            st_ref[p] = st_ref[p] * e_tot[:, sl] + _mm(bm_t, xdec[:, sl])
            ys.append(y)
            yield
    return jnp.concatenate(ys, axis=1)


def _ssd_scan_body(xf_ref, dtf_ref, xb_ref, dtb_ref, s0_ref, yf_ref, yb_ref, sfin_ref, st_scr):
    c = pl.program_id(1)

    @pl.when(c == 0)
    def _():
        for d in range(2):
            for p in range(PAIRS):
                st_scr[d, p] = s0_ref[0, d, p].T

    y_f, y_b = _in_lockstep(_ssd_chunk(False, 0, xf_ref[0].astype(F32), dtf_ref[0], st_scr.at[0]),
                            _ssd_chunk(True, 1, xb_ref[0].astype(F32), dtb_ref[0], st_scr.at[1]))
    yf_ref[0] = y_f.astype(yf_ref.dtype)
    yb_ref[0] = y_b.astype(yb_ref.dtype)

    @pl.when(c == pl.num_programs(1) - 1)
    def _():
        for d in range(2):
            for p in range(PAIRS):
                sfin_ref[0, d, p] = st_scr[d, p].T


def ssd_scan(xbc, dtp, s0_pairs):
    b, n, _ = xbc.shape
    nc = n // SSD_Q
    fw = lambda w: pl.BlockSpec((1, SSD_Q, w), lambda bi, c: (bi, c, 0))
    bw = lambda w: pl.BlockSpec((1, SSD_Q, w), lambda bi, c: (bi, nc - 1 - c, 0))
    st = pl.BlockSpec((1, 2, PAIRS, LANE, LANE), lambda bi, c: (bi, 0, 0, 0, 0))
    return pl.pallas_call(
        _ssd_scan_body,
        grid=(b, nc),
        in_specs=[fw(SSD_CONV_DIM), fw(LANE), bw(SSD_CONV_DIM), bw(LANE), st],
        out_specs=[fw(BR_WIDTH), bw(BR_WIDTH), st],
        out_shape=[jax.ShapeDtypeStruct((b, n, BR_WIDTH), BF16)] * 2
        + [jax.ShapeDtypeStruct((b, 2, PAIRS, LANE, LANE), F32)],
        scratch_shapes=[pltpu.VMEM((2, PAIRS, LANE, LANE), F32)],
        compiler_params=_cparams("parallel", "arbitrary"),
        name="ssd_scan",
    )(xbc, dtp, xbc, dtp, s0_pairs)


def ssd_state_to_pairs(h0):
    b = h0.shape[0]
    s = h0.reshape(b, 2, PAIRS, LANE, SSD_STATE)
    lo = jnp.pad(s, ((0, 0),) * 4 + ((0, SSD_STATE),))
    hi = jnp.pad(s, ((0, 0),) * 4 + ((SSD_STATE, 0),))
    return jnp.where(_pair_in_group0(), lo, hi)


def _pair_in_group0():
    return (jnp.arange(PAIRS) < PAIRS // SSD_GROUPS)[None, None, :, None, None]


def ssd_pairs_to_state(sp):
    b = sp.shape[0]
    s = jnp.where(_pair_in_group0(), sp[..., :SSD_STATE], sp[..., SSD_STATE:])
    return s.reshape(b, 2, SSD_HEADS, SSD_HEAD_DIM, SSD_STATE)


def _ssd_post_body(yf_ref, yb_ref, x_ref, gate_ref, d_ref, g_ref, o_ref):
    y = (x_ref[...].astype(F32) * d_ref[...] + yf_ref[...].astype(F32) + yb_ref[...].astype(F32)) * _silu(gate_ref[...].astype(F32))
    y = y * lax.rsqrt(jnp.mean(y * y, axis=-1, keepdims=True) + EPS)
    o_ref[...] = (y * g_ref[...]).astype(o_ref.dtype)


def ssd_post(y_f, y_b, xbc2d, z_ssd2d, d_full, norm_g, tm):
    t = y_f.shape[0]
    tile = pl.BlockSpec((tm, BR_WIDTH), lambda i: (i, 0))
    return pl.pallas_call(
        _ssd_post_body,
        grid=(t // tm,),
        in_specs=[tile] * 4 + [_const_spec((1, BR_WIDTH))] * 2,
        out_specs=tile,
        out_shape=jax.ShapeDtypeStruct((t, BR_WIDTH), BF16),
        compiler_params=_cparams("parallel"),
        name="ssd_post",
    )(y_f, y_b, xbc2d, z_ssd2d, d_full.reshape(1, -1), norm_g.reshape(1, -1))


def _lru_prep_body(z_ref, zp_ref, zn_ref, w_ref, b_ref, wa_ref, ba_ref, wx_ref, bx_ref, sp_ref,
                   a0_ref, u0_ref, a1_ref, u1_ref):
    cols = slice(0, LRU_WIDTH)
    prev8, next8 = _halo_rows(zp_ref, zn_ref, cols)
    xc = _centred_conv(z_ref[0, :, cols].astype(F32), prev8, next8, w_ref, b_ref)
    for d, (a_ref, u_ref) in enumerate(((a0_ref, u0_ref), (a1_ref, u1_ref))):
        g_r = jax.nn.sigmoid(_mm(xc, wa_ref[d]) + ba_ref[d])
        g_i = jax.nn.sigmoid(_mm(xc, wx_ref[d]) + bx_ref[d])
        log_a = -LRU_C * g_r * sp_ref[d]
        t = jnp.tanh(log_a)
        a_ref[0] = jnp.exp(log_a)
        u_ref[0] = (jnp.sqrt(-2.0 * t / (1.0 - t)) * (g_i * xc)).astype(u_ref.dtype)


def _block_diag(w):
    d, k, bi, bo = w.shape
    return jnp.einsum('dkij,kl->dkilj', w, jnp.eye(k, dtype=w.dtype)).reshape(d, k * bi, k * bo)


def lru_prep(z_lru, conv_w, conv_b, wa, ba, wx, bx, lam, tq):
    b, n, c = z_lru.shape
    tile = pl.BlockSpec((1, tq, LRU_WIDTH), lambda bi, i: (bi, i, 0))
    out = jax.ShapeDtypeStruct((b, n, LRU_WIDTH), F32)
    sp = _softplus(-lam)[:, None, :]
    return pl.pallas_call(
        _lru_prep_body,
        grid=(b, n // tq),
        in_specs=_halo_specs(tq, n, c) + [_const_spec((CONV_K, LRU_WIDTH)), _const_spec((1, LRU_WIDTH)),
                                           _const_spec((2, LRU_WIDTH, LRU_WIDTH)), _const_spec((2, 1, LRU_WIDTH)),
                                           _const_spec((2, LRU_WIDTH, LRU_WIDTH)), _const_spec((2, 1, LRU_WIDTH)),
                                           _const_spec((2, 1, LRU_WIDTH))],
        out_specs=[tile] * 4,
        out_shape=[out, jax.ShapeDtypeStruct((b, n, LRU_WIDTH), BF16)] * 2,
        compiler_params=_cparams("parallel", "parallel"),
        name="lru_prep",
    )(z_lru, z_lru, z_lru, conv_w, conv_b.reshape(1, -1), _block_diag(wa).astype(BF16), ba[:, None, :],
      _block_diag(wx).astype(BF16), bx[:, None, :], sp)


def _lru_tile_scan(rev, a, u, carry):
    n = a.shape[0]
    row = _iota2(a.shape, 0)
    s = 1
    while s < n:
        shift = n - s if rev else s
        keep = (row < n - s) if rev else (row >= s)
        a_sh = jnp.where(keep, pltpu.roll(a, shift, 0), 1.0)
        u_sh = jnp.where(keep, pltpu.roll(u, shift, 0), 0.0)
        u = u + a * u_sh
        a = a * a_sh
        s *= 2
    h = a * carry + u
    return h, (h[0:1] if rev else h[n - 1:n])


def _lru_scan_body(af_ref, uf_ref, ab_ref, ub_ref, h0_ref, yf_ref, yb_ref, hfin_ref, h_scr):
    c = pl.program_id(1)

    @pl.when(c == 0)
    def _():
        h_scr[...] = h0_ref[0]

    hf, cf = _lru_tile_scan(False, af_ref[0], uf_ref[0].astype(F32), h_scr[0:1])
    yf_ref[0] = hf.astype(yf_ref.dtype)
    h_scr[0:1] = cf
    hb, cb = _lru_tile_scan(True, ab_ref[0], ub_ref[0].astype(F32), h_scr[1:2])
    yb_ref[0] = hb.astype(yb_ref.dtype)
    h_scr[1:2] = cb

    @pl.when(c == pl.num_programs(1) - 1)
    def _():
        hfin_ref[0] = h_scr[...]


def lru_scan(a0, u0, a1, u1, h0, tq):
    b, n, _ = a0.shape
    nc = n // tq
    fw = pl.BlockSpec((1, tq, LRU_WIDTH), lambda bi, c: (bi, c, 0))
    bw = pl.BlockSpec((1, tq, LRU_WIDTH), lambda bi, c: (bi, nc - 1 - c, 0))
    st = pl.BlockSpec((1, 2, LRU_WIDTH), lambda bi, c: (bi, 0, 0))
    return pl.pallas_call(
        _lru_scan_body,
        grid=(b, nc),
        in_specs=[fw, fw, bw, bw, st],
        out_specs=[fw, bw, st],
        out_shape=[jax.ShapeDtypeStruct((b, n, LRU_WIDTH), BF16)] * 2 + [jax.ShapeDtypeStruct((b, 2, LRU_WIDTH), F32)],
        scratch_shapes=[pltpu.VMEM((2, LRU_WIDTH), F32)],
        compiler_params=_cparams("parallel", "arbitrary"),
        name="lru_scan",
    )(a0, u0, a1, u1, h0)


def _lru_post_body(yf_ref, yb_ref, gate_ref, o_ref):
    o_ref[...] = ((yf_ref[...].astype(F32) + yb_ref[...].astype(F32)) * _silu(gate_ref[...].astype(F32))).astype(o_ref.dtype)


def lru_post(y_f, y_b, z_lru2d, tm):
    t = y_f.shape[0]
    tile = pl.BlockSpec((tm, LRU_WIDTH), lambda i: (i, 0))
    return pl.pallas_call(
        _lru_post_body,
        grid=(t // tm,),
        in_specs=[tile, tile, pl.BlockSpec((tm, LRU_WIDTH), lambda i: (i, 1))],
        out_specs=tile,
        out_shape=jax.ShapeDtypeStruct((t, LRU_WIDTH), BF16),
        compiler_params=_cparams("parallel"),
        name="lru_post",
    )(y_f, y_b, z_lru2d)


def rope_tables(n):
    t = np.arange(n)
    pos = jnp.asarray(np.stack([t // GRID_W, t % GRID_W], axis=0).astype(np.float32))
    nf = MLA_ROPE // 4
    inv = jnp.asarray(ROPE_THETA, F32) ** (-jnp.arange(nf, dtype=F32) / nf)
    rel = np.arange(LANE) - MLA_NOPE
    in_rope = (rel >= 0) & (rel < MLA_ROPE)
    half = np.clip(rel // (2 * nf), 0, 1)
    within = rel % (2 * nf)
    is_x2 = within >= nf
    ang = pos[half, :].T * inv[within % nf][None, :]
    cos = jnp.where(in_rope[None, :], jnp.cos(ang), 1.0)
    sin = jnp.sin(ang)
    sin_a = jnp.where((in_rope & ~is_x2)[None, :], -sin, 0.0)
    sin_b = jnp.where((in_rope & is_x2)[None, :], sin, 0.0)
    return cos, sin_a, sin_b


def _head_norm_rope(x, g, rope):
    nf = MLA_ROPE // 4
    outs = []
    for h in range(MLA_HEADS):
        xh = x[:, h * LANE:(h + 1) * LANE]
        ms = jnp.sum(xh * xh, axis=-1, keepdims=True) * (1.0 / MLA_QK)
        xh = xh * lax.rsqrt(ms + EPS) * g
        if rope is not None:
            cos, sin_a, sin_b = rope
            xh = xh * cos + pltpu.roll(xh, LANE - nf, 1) * sin_a + pltpu.roll(xh, nf, 1) * sin_b
        outs.append(xh)
    return jnp.concatenate(outs, axis=1)


def _mla_q_body(use_rope, q_ref, qag_ref, wq_ref, qng_ref, *rest):
    *rope_refs, o_ref = rest
    x = q_ref[:, :MLA_Q_RANK].astype(F32)
    x = x * lax.rsqrt(jnp.mean(x * x, axis=-1, keepdims=True) + EPS) * qag_ref[...]
    q = _mm(x, wq_ref[...])
    rope = tuple(r[...] for r in rope_refs) if use_rope else None
    q = _head_norm_rope(q, qng_ref[...], rope)
    o_ref[...] = (q * (MLA_QK ** -0.5)).astype(o_ref.dtype)


def mla_q(z_mla, qa_g, wq_pad, qn_g_pad, rope, n, tm):
    t = z_mla.shape[0]
    per = n // tm
    in_specs = [pl.BlockSpec((tm, 512), lambda i: (i, MLA_Q_BLOCK)), _const_spec((1, MLA_Q_RANK)),
                _const_spec((MLA_Q_RANK, MLA_W)), _const_spec((1, LANE))]
    args = [z_mla, qa_g.reshape(1, -1), wq_pad, qn_g_pad]
    if rope is not None:
        in_specs += [pl.BlockSpec((tm, LANE), lambda i: (i % per, 0))] * 3
        args += list(rope)
    return pl.pallas_call(
        lambda *refs: _mla_q_body(rope is not None, *refs),
        grid=(t // tm,),
        in_specs=in_specs,
        out_specs=pl.BlockSpec((tm, MLA_W), lambda i: (i, 0)),
        out_shape=jax.ShapeDtypeStruct((t, MLA_W), BF16),
        compiler_params=_cparams("parallel"),
        name="mla_q",
    )(*args)


def _mla_kv_body(norm_ckv, use_rope, c_ref, kr_ref, kvag_ref, wk_ref, wv_ref, kng_ref, *rest):
    *rope_refs, cn_ref, k_ref, v_ref = rest
    c = c_ref[...].astype(F32)
    if norm_ckv:
        c = c * lax.rsqrt(jnp.mean(c * c, axis=-1, keepdims=True) + EPS) * kvag_ref[...]
    cn_ref[...] = c
    k = _mm(c, wk_ref[...]) + kr_ref[...].astype(F32)
    rope = tuple(r[...] for r in rope_refs) if use_rope else None
    k_ref[...] = _head_norm_rope(k, kng_ref[...], rope).astype(k_ref.dtype)
    v_ref[...] = _mm(c, wv_ref[...]).astype(v_ref.dtype)


def mla_kv(c_arr, c_block, kr_arr, kr_block, kva_g, wk_pad, wv, kn_g_pad, rope, norm_ckv, n, tm):
    t = c_arr.shape[0]
    per = n // tm
    in_specs = [pl.BlockSpec((tm, MLA_KV_RANK), lambda i: (i, c_block)),
                pl.BlockSpec((tm, MLA_W), lambda i: (i, kr_block)),
                _const_spec((1, MLA_KV_RANK)), _const_spec((MLA_KV_RANK, MLA_W)), _const_spec((MLA_KV_RANK, BR_WIDTH)),
                _const_spec((1, LANE))]
    args = [c_arr, kr_arr, kva_g.reshape(1, -1), wk_pad, wv, kn_g_pad]
    if rope is not None:
        in_specs += [pl.BlockSpec((tm, LANE), lambda i: (i % per, 0))] * 3
        args += list(rope)
    return pl.pallas_call(
        lambda *refs: _mla_kv_body(norm_ckv, rope is not None, *refs),
        grid=(t // tm,),
        in_specs=in_specs,
        out_specs=[pl.BlockSpec((tm, MLA_KV_RANK), lambda i: (i, 0)), pl.BlockSpec((tm, MLA_W), lambda i: (i, 0)),
                   pl.BlockSpec((tm, BR_WIDTH), lambda i: (i, 0))],
        out_shape=[jax.ShapeDtypeStruct((t, MLA_KV_RANK), F32), jax.ShapeDtypeStruct((t, MLA_W), BF16),
                   jax.ShapeDtypeStruct((t, BR_WIDTH), BF16)],
        compiler_params=_cparams("parallel"),
        name="mla_kv",
    )(*args)


def _attn_body(n_src, q_ref, gate_ref, *rest):
    kv_refs = rest[:2 * n_src]
    o_ref = rest[2 * n_src]
    scr = rest[2 * n_src + 1:]
    tq = q_ref.shape[0]
    part = tq // ATTN_SPLIT
    offs = [0]
    for s_i in range(n_src):
        offs.append(offs[-1] + kv_refs[2 * s_i].shape[1])
    tiles = [(h, r) for h in range(2) for r in range(ATTN_SPLIT)]
    out = {}

    def scores(t):
        h, r = tiles[t]
        s_scr = scr[3 * t]
        for s_i in range(n_src):
            s_scr[:, offs[s_i]:offs[s_i + 1]] = lax.dot_general(
                q_ref[r * part:(r + 1) * part, h * LANE:(h + 1) * LANE], kv_refs[2 * s_i][0, :, h * LANE:(h + 1) * LANE],
                (((1,), (1,)), ((), ())), preferred_element_type=F32)

    def softmax(t):
        s_scr, p_scr, l_scr = scr[3 * t:3 * t + 3]
        for i in range(part // ATTN_ROWS):
            rows = slice(i * ATTN_ROWS, (i + 1) * ATTN_ROWS)
            m = jnp.max(s_scr[rows, :], axis=-1, keepdims=True)
            l = jnp.zeros((ATTN_ROWS, 1), F32)
            for c0 in range(0, offs[-1], ATTN_COLS):
                cols = slice(c0, min(c0 + ATTN_COLS, offs[-1]))
                p = jnp.exp(s_scr[rows, cols] - m)
                p_scr[rows, cols] = p.astype(BF16)
                l = l + jnp.sum(p, axis=-1, keepdims=True)
            l_scr[rows, :] = jnp.broadcast_to(l, (ATTN_ROWS, LANE))

    def values(t):
        _, p_scr, l_scr = scr[3 * t:3 * t + 3]
        acc = None
        for s_i in range(n_src):
            pv = jnp.dot(p_scr[:, offs[s_i]:offs[s_i + 1]], kv_refs[2 * s_i + 1][0], preferred_element_type=F32)
            acc = pv if acc is None else acc + pv
        out[tiles[t]] = acc / l_scr[...]

    nt = len(tiles)
    scores(0)
    for t in range(nt):
        if t + 1 < nt:
            scores(t + 1)
        softmax(t)
        if t >= 1:
            values(t - 1)
    values(nt - 1)
    accs = [jnp.concatenate([out[(h, r)] for r in range(ATTN_SPLIT)], axis=0) for h in range(2)]
    o = jnp.where(_iota2((tq, LANE), 1) < MLA_V, accs[0], accs[1])
    o_ref[...] = (o * _silu(gate_ref[...].astype(F32))).astype(o_ref.dtype)


def mla_attention(q, z_mla, kvs, n, tq):
    t = q.shape[0]
    b = t // n
    per = n // tq
    in_specs = [pl.BlockSpec((tq, 2 * LANE), lambda bi, p, i: (bi * per + i, p)),
                pl.BlockSpec((tq, LANE), lambda bi, p, i: (bi * per + i, MLA_GATE_LANE_BLOCK + p))]
    args = [q, z_mla]
    for k, v in kvs:
        nk = k.shape[1]
        in_specs += [pl.BlockSpec((1, nk, 2 * LANE), lambda bi, p, i: (bi, 0, p)),
                     pl.BlockSpec((1, nk, LANE), lambda bi, p, i: (bi, 0, p))]
        args += [k, v]
    n_keys = sum(k.shape[1] for k, _ in kvs)
    return pl.pallas_call(
        lambda *refs: _attn_body(len(kvs), *refs),
        grid=(b, PAIRS, per),
        in_specs=in_specs,
        out_specs=pl.BlockSpec((tq, LANE), lambda bi, p, i: (bi * per + i, p)),
        out_shape=jax.ShapeDtypeStruct((t, BR_WIDTH), BF16),
        scratch_shapes=[pltpu.VMEM((tq // ATTN_SPLIT, n_keys), F32), pltpu.VMEM((tq // ATTN_SPLIT, n_keys), BF16),
                        pltpu.VMEM((tq // ATTN_SPLIT, LANE), F32)] * (2 * ATTN_SPLIT),
        compiler_params=_cparams("parallel", "parallel", "arbitrary"),
        name="mla_attention",
    )(*args)


def pad_heads(w, real, lead_zero=0):
    shp = w.shape[:-1]
    w = w.reshape(*shp, MLA_HEADS, real)
    w = jnp.pad(w, [(0, 0)] * len(shp) + [(0, 0), (lead_zero, LANE - real - lead_zero)])
    return w.reshape(*shp, MLA_W)


def pad_lane(g):
    return jnp.pad(g.reshape(1, -1), ((0, 0), (0, LANE - g.size)))


def place_rope_key(k_rope):
    return pad_heads(jnp.tile(k_rope, (1,) * (k_rope.ndim - 1) + (MLA_HEADS,)), MLA_ROPE, MLA_NOPE)


def pack_weights(P, l):
    w = P['w_in'][l]
    col = lambda i: w[:, IN_OFFS[i]:IN_OFFS[i + 1]]
    zeros = lambda n: jnp.zeros((D_MODEL, n), w.dtype)
    mla_kv = col(3)
    w_rw = jnp.concatenate([col(1), col(0)], axis=1)
    w_mla = jnp.concatenate([place_rope_key(mla_kv[:, MLA_KV_RANK:]), col(4), col(2), zeros(512 - MLA_Q_RANK),
                             mla_kv[:, :MLA_KV_RANK]], axis=1)
    w_ssd = jnp.concatenate([col(5), col(6), col(7), zeros(LANE - 2 * SSD_HEADS)], axis=1)
    w_lru = jnp.concatenate([col(8), col(9)], axis=1)
    kvu = P['mla_kv_up'][l].reshape(MLA_KV_RANK, MLA_HEADS, MLA_NOPE + MLA_V)
    bf = lambda x: x.astype(BF16)
    return dict(w_rw=bf(w_rw), w_mla=bf(w_mla), w_ssd=bf(w_ssd), w_lru=bf(w_lru), w_mrg=bf(col(10)),
                wq=bf(pad_heads(P['mla_q_up'][l], MLA_QK)),
                wk=bf(pad_heads(kvu[:, :, :MLA_NOPE].reshape(MLA_KV_RANK, -1), MLA_NOPE)),
                wv=bf(kvu[:, :, MLA_NOPE:].reshape(MLA_KV_RANK, -1)),
                w_branch=bf(P['w_branch'][l]), w_out=bf(P['w_out'][l]))


def trunk_layer(x, mod, P, W, l, ctx):
    b, n, d = x.shape
    t = b * n
    tm = min(n, 512)
    tq = 256
    shift, scale, gate = mod
    x2d = x.reshape(t, d)
    h = prenorm(x2d, P['norm_g'][l], scale, shift, n, tm)
    z_rw = matmul(h, W['w_rw'], tm, Z_RW_W, BF16)
    z_mla = matmul(h, W['w_mla'], tm, Z_MLA_W, BF16)
    z_ssd = matmul(h, W['w_ssd'], tm, Z_SSD_W, F32)
    z_lru = matmul(h, W['w_lru'], tm, Z_LRU_W, BF16)
    logits = matmul(h, W['w_mrg'], tm, 2048, BF16)
    if ctx is None:
        rw0 = jnp.zeros((b, 2, PAIRS, LANE, LANE), F32)
        ssd0 = jnp.zeros((b, 2, PAIRS, LANE, LANE), F32)
        lru0 = jnp.zeros((b, 2, LRU_WIDTH), F32)
        rope = None
    else:
        ckv0, krope0, rw0, ssd0, lru0 = ctx
        rw0, ssd0 = rw_state_to_pairs(rw0), ssd_state_to_pairs(ssd0)
        rope = rope_tables(n)

    prep = rw_prep(z_rw.reshape(b, n, -1), P, l, tq)
    o_f, o_b, rw_fin = rw_scan(prep, rw0)
    flat = lambda a: a.reshape(t, -1)
    br_rw = rw_post(flat(o_f), flat(o_b), flat(prep[9]), z_rw, P['rw_ln_g'][l], P['rw_ln_b'][l], tm)

    qn_g, kn_g = pad_lane(P['mla_qn_g'][l]), pad_lane(P['mla_kn_g'][l])
    q = mla_q(z_mla, P['mla_qa_g'][l], W['wq'], qn_g, rope, n, tm)
    c_kv, k, v = mla_kv(z_mla, MLA_CKV_BLOCK, z_mla, 0, P['mla_kva_g'][l], W['wk'], W['wv'], kn_g, rope, True, n, tm)
    kvs = [(k.reshape(b, n, -1), v.reshape(b, n, -1))]
    if ctx is not None:
        past = ckv0.shape[1]
        _, k0, v0 = mla_kv(ckv0.reshape(b * past, -1), 0, place_rope_key(krope0).reshape(b * past, -1), 0,
                           P['mla_kva_g'][l], W['wk'], W['wv'], kn_g, None, False, past, past)
        kvs = [(k0.reshape(b, past, -1), v0.reshape(b, past, -1))] + kvs
    br_mla = mla_attention(q, z_mla, kvs, n, min(n, ATTN_TQ))
    k_rope = z_mla[:, MLA_NOPE:MLA_NOPE + MLA_ROPE].astype(F32)

    xbc, dtp = ssd_prep(z_ssd.reshape(b, n, -1), P['ssd_conv_w'][l], P['ssd_conv_b'][l], P['ssd_dt_bias'][l],
                        P['ssd_a_log'][l], tq)
    y_f, y_b, ssd_fin = ssd_scan(xbc, dtp, ssd0)
    br_ssd = ssd_post(flat(y_f), flat(y_b), flat(xbc), z_ssd, jnp.repeat(P['ssd_d'][l], SSD_HEAD_DIM),
                      P['ssd_norm_g'][l], tm)

    a0, u0, a1, u1 = lru_prep(z_lru.reshape(b, n, -1), P['lru_conv_w'][l], P['lru_conv_b'][l], P['lru_wa'][l],
                              P['lru_ba'][l], P['lru_wx'][l], P['lru_bx'][l], P['lru_lambda'][l], tq)
    h_f, h_b, lru_fin = lru_scan(a0, u0, a1, u1, lru0, tq)
    br_lru = lru_post(flat(h_f), flat(h_b), z_lru, tm)

    y = merge(x2d, [br_rw, br_mla, br_ssd, br_lru], logits, W['w_branch'], W['w_out'], gate, n, tm)
    states = (c_kv.reshape(b, n, -1), k_rope.reshape(b, n, -1), rw_pairs_to_state(rw_fin),
              ssd_pairs_to_state(ssd_fin), lru_fin)
    return y.reshape(b, n, d), states


def kernel(x_prompt, x_sample, cache_mla_ckv, cache_mla_krope, state_rwkv, state_ssd, state_lru, c, c_ctx, ada_w, ada_b, norm_g, w_in, rw_mu, rw_w0, rw_w2, rw_a0, rw_a2, rw_kk, rw_ka, rw_rk, rw_ln_g, rw_ln_b, mla_qa_g, mla_q_up, mla_kva_g, mla_kv_up, mla_qn_g, mla_kn_g, ssd_conv_w, ssd_conv_b, ssd_dt_bias, ssd_a_log, ssd_d, ssd_norm_g, lru_conv_w, lru_conv_b, lru_wa, lru_ba, lru_wx, lru_bx, lru_lambda, w_branch, w_out):
    P = dict(norm_g=norm_g, w_in=w_in,
             rw_mu=rw_mu, rw_w0=rw_w0, rw_w2=rw_w2, rw_a0=rw_a0, rw_a2=rw_a2, rw_kk=rw_kk,
             rw_ka=rw_ka, rw_rk=rw_rk.reshape(DEPTH, BR_WIDTH), rw_ln_g=rw_ln_g, rw_ln_b=rw_ln_b,
             mla_qa_g=mla_qa_g, mla_q_up=mla_q_up, mla_kva_g=mla_kva_g, mla_kv_up=mla_kv_up,
             mla_qn_g=mla_qn_g, mla_kn_g=mla_kn_g,
             ssd_conv_w=ssd_conv_w, ssd_conv_b=ssd_conv_b, ssd_dt_bias=ssd_dt_bias,
             ssd_a_log=ssd_a_log, ssd_d=ssd_d, ssd_norm_g=ssd_norm_g,
             lru_conv_w=lru_conv_w, lru_conv_b=lru_conv_b, lru_wa=lru_wa, lru_ba=lru_ba,
             lru_wx=lru_wx, lru_bx=lru_bx, lru_lambda=lru_lambda,
             w_branch=w_branch, w_out=w_out)
    dec_batch = c.shape[0]
    cond8 = jnp.concatenate([c_ctx[None, :], c, jnp.zeros((SUBLANE - 1 - dec_batch, D_MODEL), F32)], axis=0)
    weights, mods = [], []
    for l in range(DEPTH):
        weights.append(pack_weights(P, l))
        mod = ada_mod(cond8, ada_w[l], ada_b[l])
        mods.append(tuple(jnp.split(mod[:, None, :], 3, axis=-1)))

    y_prompt = x_prompt
    ctx_states = []
    for l in range(DEPTH):
        y_prompt, st = trunk_layer(y_prompt, tuple(m[0:1] for m in mods[l]), P, weights[l], l, None)
        ctx_states.append(st)
    new_states = tuple(jnp.stack([s[i] for s in ctx_states], axis=1) for i in range(5))

    y_sample = x_sample
    for l in range(DEPTH):
        ctx = (cache_mla_ckv[:, l], cache_mla_krope[:, l], state_rwkv[:, l], state_ssd[:, l], state_lru[:, l])
        y_sample, _ = trunk_layer(y_sample, tuple(m[1:1 + dec_batch] for m in mods[l]), P, weights[l], l, ctx)

    return (y_prompt, y_sample) + new_states
```

```python
import math

import jax
import jax.numpy as jnp
import numpy as np
from jax import lax
from jax.experimental import pallas as pl
from jax.experimental.pallas import tpu as pltpu

F32 = jnp.float32
BF16 = jnp.bfloat16

D_MODEL = 1024
DEPTH = 2
GRID_W = 64
N_BRANCH = 4
BR_WIDTH = 512
EPS = 1e-6
CONV_K = 4

RW_HEADS = 8
RW_HEAD = 64
RW_LORA = 64
RW_GN_EPS = 64e-5
RW_PRE = 3 * BR_WIDTH + 2 * RW_LORA

MLA_HEADS = 8
MLA_NOPE = 64
MLA_ROPE = 32
MLA_QK = MLA_NOPE + MLA_ROPE
MLA_V = 64
MLA_Q_RANK = 384
MLA_KV_RANK = 256
ROPE_THETA = 10000.0

SSD_HEADS = 8
SSD_HEAD_DIM = 64
SSD_GROUPS = 2
SSD_STATE = 64
SSD_CONV_DIM = BR_WIDTH + 2 * SSD_GROUPS * SSD_STATE

LRU_WIDTH = 512
LRU_BLOCKS = 8
LRU_C = 8.0

IN_WIDTHS = (RW_PRE, BR_WIDTH, MLA_Q_RANK, MLA_KV_RANK + MLA_ROPE, BR_WIDTH, BR_WIDTH, SSD_CONV_DIM,
             2 * SSD_HEADS, LRU_WIDTH, LRU_WIDTH, N_BRANCH * D_MODEL)
IN_OFFS = tuple(int(v) for v in np.cumsum((0,) + IN_WIDTHS))

LANE = 128
SUBLANE = 8
HALO = 16
VMEM_LIMIT = 48 * 1024 * 1024

RW_CHUNK = 64
PAIRS = BR_WIDTH // LANE
SSD_Q = 128
MLA_W = MLA_HEADS * LANE
ATTN_TQ = 512
ATTN_SPLIT = 2
ATTN_ROWS = 16
ATTN_COLS = 640

Z_RW_W = BR_WIDTH + RW_PRE
Z_MLA_W = MLA_W + BR_WIDTH + 512 + MLA_KV_RANK
Z_SSD_W = BR_WIDTH + SSD_CONV_DIM + LANE
Z_LRU_W = 2 * LRU_WIDTH
MLA_GATE_LANE_BLOCK = MLA_W // LANE
MLA_Q_BLOCK = (MLA_W + BR_WIDTH) // 512
MLA_CKV_BLOCK = (MLA_W + BR_WIDTH + 512) // MLA_KV_RANK


def _cparams(*sem):
    return pltpu.CompilerParams(dimension_semantics=sem, vmem_limit_bytes=VMEM_LIMIT)


def _const_spec(shape):
    return pl.BlockSpec(shape, lambda *_: (0,) * len(shape))


def _mm(a, b):
    return jnp.dot(a.astype(BF16), b.astype(BF16), preferred_element_type=F32)


def _mm_nt(a, b):
    return lax.dot_general(a.astype(BF16), b.astype(BF16), (((1,), (1,)), ((), ())), preferred_element_type=F32)


def _mm_01(m01, x):
    m01 = m01.astype(BF16)
    x1 = x.astype(BF16)
    r1 = x - x1.astype(F32)
    x2 = r1.astype(BF16)
    x3 = (r1 - x2.astype(F32)).astype(BF16)
    return (jnp.dot(m01, x1, preferred_element_type=F32) + jnp.dot(m01, x2, preferred_element_type=F32)
            + jnp.dot(m01, x3, preferred_element_type=F32))


def _iota2(shape, dim):
    return lax.broadcasted_iota(jnp.int32, shape, dim)


def _sigmoid(x):
    return 0.5 * jnp.tanh(0.5 * x) + 0.5


def _silu(x):
    return x * _sigmoid(x)


def _softplus(x):
    return jnp.maximum(x, 0.0) + jnp.log(1.0 + jnp.exp(-jnp.abs(x)))


def _in_lockstep(*gens):
    results = [None] * len(gens)
    live = list(range(len(gens)))
    while live:
        for i in list(live):
            try:
                next(gens[i])
            except StopIteration as done:
                results[i] = done.value
                live.remove(i)
    return results


def _head_sum(x, width):
    bd = ((_iota2((LANE, LANE), 0) // width) == (_iota2((LANE, LANE), 1) // width)).astype(BF16)
    hi = x.astype(BF16)
    lo = (x - hi.astype(F32)).astype(BF16)
    outs = []
    for j in range(x.shape[-1] // LANE):
        sl = slice(j * LANE, (j + 1) * LANE)
        outs.append(jnp.dot(hi[:, sl], bd, preferred_element_type=F32) + jnp.dot(lo[:, sl], bd, preferred_element_type=F32))
    return jnp.concatenate(outs, axis=1)


def _from_prev(u, s, prev8):
    rolled = pltpu.roll(u, s, 0)
    head = jnp.where(_iota2(prev8.shape, 0) < s, pltpu.roll(prev8, s, 0), rolled[:SUBLANE])
    return jnp.concatenate([head, rolled[SUBLANE:]], axis=0)


def _from_next(u, next8):
    n = u.shape[0]
    rolled = pltpu.roll(u, n - 1, 0)
    tail = jnp.where(_iota2(next8.shape, 0) == SUBLANE - 1, pltpu.roll(next8, SUBLANE - 1, 0), rolled[n - SUBLANE:])
    return jnp.concatenate([rolled[:n - SUBLANE], tail], axis=0)


def _centred_conv(x, prev8, next8, w_ref, b_ref):
    return (w_ref[0:1, :] * _from_prev(x, 2, prev8) + w_ref[1:2, :] * _from_prev(x, 1, prev8)
            + w_ref[2:3, :] * x + w_ref[3:4, :] * _from_next(x, next8) + b_ref[...])


def _halo_specs(tq, n, c):
    per, nb = tq // HALO, n // HALO
    return [pl.BlockSpec((1, tq, c), lambda bi, i: (bi, i, 0)),
            pl.BlockSpec((1, HALO, c), lambda bi, i: (bi, jnp.maximum(i * per - 1, 0), 0)),
            pl.BlockSpec((1, HALO, c), lambda bi, i: (bi, jnp.minimum((i + 1) * per, nb - 1), 0))]


def _halo_rows(zp_ref, zn_ref, cols):
    i = pl.program_id(1)
    prev8 = jnp.where(i > 0, zp_ref[0, HALO - SUBLANE:, cols].astype(F32), 0.0)
    next8 = jnp.where(i < pl.num_programs(1) - 1, zn_ref[0, :SUBLANE, cols].astype(F32), 0.0)
    return prev8, next8


def _ada_body(c_ref, w_ref, b_ref, o_ref):
    o_ref[...] = _mm(_silu(c_ref[...]), w_ref[...]) + b_ref[...]


def ada_mod(cond8, w, bias):
    d = cond8.shape[1]
    return pl.pallas_call(
        _ada_body,
        grid=(w.shape[1] // d,),
        in_specs=[_const_spec((SUBLANE, d)), pl.BlockSpec((d, d), lambda j: (0, j)), pl.BlockSpec((1, d), lambda j: (0, j))],
        out_specs=pl.BlockSpec((SUBLANE, d), lambda j: (0, j)),
        out_shape=jax.ShapeDtypeStruct((SUBLANE, w.shape[1]), F32),
        compiler_params=_cparams("parallel"),
        name="ada_mod",
    )(cond8, w, bias.reshape(1, -1))


def _prenorm_body(x_ref, g_ref, sc_ref, sh_ref, h_ref):
    x = x_ref[...]
    y = x * lax.rsqrt(jnp.mean(x * x, axis=-1, keepdims=True) + EPS)
    h_ref[...] = ((y * g_ref[...]) * (1.0 + sc_ref[0]) + sh_ref[0]).astype(h_ref.dtype)


def _mod_index(bm, per):
    return (lambda i: (i // per, 0, 0)) if bm > 1 else (lambda i: (0, 0, 0))


def prenorm(x2d, g, scale, shift, n, tm):
    t, d = x2d.shape
    mod_idx = _mod_index(scale.shape[0], n // tm)
    return pl.pallas_call(
        _prenorm_body,
        grid=(t // tm,),
        in_specs=[pl.BlockSpec((tm, d), lambda i: (i, 0)), _const_spec((1, d)),
                  pl.BlockSpec((1, 1, d), mod_idx), pl.BlockSpec((1, 1, d), mod_idx)],
        out_specs=pl.BlockSpec((tm, d), lambda i: (i, 0)),
        out_shape=jax.ShapeDtypeStruct((t, d), BF16),
        compiler_params=_cparams("parallel"),
        name="prenorm",
    )(x2d, g.reshape(1, -1), scale, shift)


def _mm_body(a_ref, w_ref, o_ref):
    o_ref[...] = jnp.dot(a_ref[...], w_ref[...], preferred_element_type=F32).astype(o_ref.dtype)


def matmul(a, w, tm, tn, out_dtype):
    t, k = a.shape
    n = w.shape[1]
    return pl.pallas_call(
        _mm_body,
        grid=(n // tn, t // tm),
        in_specs=[pl.BlockSpec((tm, k), lambda j, i: (i, 0)), pl.BlockSpec((k, tn), lambda j, i: (0, j))],
        out_specs=pl.BlockSpec((tm, tn), lambda j, i: (i, j)),
        out_shape=jax.ShapeDtypeStruct((t, n), out_dtype),
        compiler_params=_cparams("parallel", "parallel"),
        name="in_proj",
    )(a, w)


def _merge_body(x_ref, b0_ref, b1_ref, b2_ref, b3_ref, lg_ref, wb_ref, wo_ref, gate_ref, y_ref):
    merged = None
    for m, b_ref in enumerate((b0_ref, b1_ref, b2_ref, b3_ref)):
        proj = jnp.dot(b_ref[...], wb_ref[m], preferred_element_type=F32)
        g = jnp.tanh(lg_ref[:, m * D_MODEL:(m + 1) * D_MODEL].astype(F32)) + 1.0
        merged = g * proj if merged is None else merged + g * proj
    out = jnp.dot(merged.astype(BF16), wo_ref[...], preferred_element_type=F32)
    y_ref[...] = x_ref[...] + (0.5 * gate_ref[0]) * out


def merge(x2d, branches, logits, wb, wo, gate, n, tm):
    t, d = x2d.shape
    mod_idx = _mod_index(gate.shape[0], n // tm)
    row = lambda i: (i, 0)
    return pl.pallas_call(
        _merge_body,
        grid=(t // tm,),
        in_specs=[pl.BlockSpec((tm, d), row)] + [pl.BlockSpec((tm, BR_WIDTH), row)] * N_BRANCH
        + [pl.BlockSpec((tm, N_BRANCH * d), row), _const_spec((N_BRANCH, BR_WIDTH, d)), _const_spec((d, d)),
           pl.BlockSpec((1, 1, d), mod_idx)],
        out_specs=pl.BlockSpec((tm, d), row),
        out_shape=jax.ShapeDtypeStruct((t, d), F32),
        compiler_params=_cparams("parallel"),
        name="merge",
    )(x2d, *branches, logits, wb, wo, gate)


def _rw_prep_body(z_ref, zp_ref, zn_ref, mu_ref, kkg_ref, ka_ref, rk_ref, w0_ref, w2_ref, a0_ref, a2_ref,
                  r_ref, kk_ref, v_ref, lw0_ref, k0_ref, b0_ref, lw1_ref, k1_ref, b1_ref, bonus_ref):
    cols = slice(BR_WIDTH, Z_RW_W)
    prev8, next8 = _halo_rows(zp_ref, zn_ref, cols)
    u = z_ref[0, :, cols].astype(F32)
    u = u + mu_ref[...] * (0.5 * (_from_prev(u, 1, prev8) + _from_next(u, next8)) - u)
    r = u[:, 0:BR_WIDTH]
    k = u[:, BR_WIDTH:2 * BR_WIDTH]
    v = u[:, 2 * BR_WIDTH:3 * BR_WIDTH]
    lo = u[:, 3 * BR_WIDTH:]
    lo = jnp.where(_iota2(lo.shape, 1) < RW_LORA, jnp.tanh(lo), lo)
    kk = k * kkg_ref[...]
    kk = kk * lax.rsqrt(_head_sum(kk * kk, RW_HEAD) + 1e-12)
    r_ref[0] = r.astype(r_ref.dtype)
    kk_ref[0] = kk.astype(kk_ref.dtype)
    v_ref[0] = v.astype(v_ref.dtype)
    coef = jnp.zeros_like(r)
    lo_hi = lo.astype(BF16)
    lo_lo = (lo - lo_hi.astype(F32)).astype(BF16)
    lora = lambda w_ref, d: (jnp.dot(lo_hi, w_ref[d, 0], preferred_element_type=F32)
                             + jnp.dot(lo_hi, w_ref[d, 1], preferred_element_type=F32)
                             + jnp.dot(lo_lo, w_ref[d, 0], preferred_element_type=F32))
    for d, (lw_ref, k_ref, b_ref) in enumerate(((lw0_ref, k0_ref, b0_ref), (lw1_ref, k1_ref, b1_ref))):
        w_raw = w0_ref[d] + lora(w2_ref, d)
        lw_ref[0] = -math.exp(-0.5) * _sigmoid(w_raw)
        a = _sigmoid(a0_ref[d] + lora(a2_ref, d))
        k_d = k * (1.0 + (a - 1.0) * ka_ref[...])
        k_ref[0] = k_d.astype(k_ref.dtype)
        b_ref[0] = (kk * a).astype(b_ref.dtype)
        coef = coef + _head_sum(r * k_d * rk_ref[...], RW_HEAD)
    bonus_ref[0] = coef * v


def rw_prep(z_rw, P, l, tq):
    b, n, c = z_rw.shape
    w2 = jnp.concatenate([P['rw_w2'][l], jnp.zeros_like(P['rw_w2'][l])], axis=1)
    a2 = jnp.concatenate([jnp.zeros_like(P['rw_a2'][l]), P['rw_a2'][l]], axis=1)
    row = lambda x: x.reshape(1, -1)
    tile = pl.BlockSpec((1, tq, BR_WIDTH), lambda bi, i: (bi, i, 0))
    out = lambda dt: jax.ShapeDtypeStruct((b, n, BR_WIDTH), dt)
    return pl.pallas_call(
        _rw_prep_body,
        grid=(b, n // tq),
        in_specs=_halo_specs(tq, n, c) + [_const_spec((1, RW_PRE))] + [_const_spec((1, BR_WIDTH))] * 3
        + [_const_spec((2, 1, BR_WIDTH)), _const_spec((2, 2, LANE, BR_WIDTH))] * 2,
        out_specs=[tile] * 10,
        out_shape=[out(BF16)] * 3 + [out(F32), out(BF16), out(BF16)] * 2 + [out(F32)],
        compiler_params=_cparams("parallel", "parallel"),
        name="rw_prep",
    )(z_rw, z_rw, z_rw, row(P['rw_mu'][l]), row(P['rw_kk'][l]), row(P['rw_ka'][l]), row(P['rw_rk'][l]),
      P['rw_w0'][l][:, None, :], _split_bf16(w2), P['rw_a0'][l][:, None, :], _split_bf16(a2))


def _split_bf16(w):
    hi = w.astype(BF16)
    return jnp.stack([hi, (w - hi.astype(F32)).astype(BF16)], axis=1)


def _rw_masks(rev):
    n = 2 * RW_CHUNK
    ri, ci = _iota2((n, n), 0), _iota2((n, n), 1)
    same = (ri // RW_CHUNK) == (ci // RW_CHUNK)
    strict = same & ((ci > ri) if rev else (ci < ri))
    incl = same & ((ci >= ri) if rev else (ci <= ri))
    blk8 = strict & ((ri // 8) == (ci // 8))
    levels = [strict & ((ri // (2 * s)) == (ci // (2 * s))) & ((ri // s) != (ci // s)) for s in (8, 16, 32)]
    eye = (ri == ci).astype(F32)
    t_i, s_i = _iota2((RW_CHUNK, RW_CHUNK), 0), _iota2((RW_CHUNK, RW_CHUNK), 1)
    tri = ((s_i >= t_i) if rev else (s_i <= t_i)).astype(F32)
    m0 = (_iota2((RW_CHUNK, LANE), 1) < RW_HEAD).astype(F32)
    return dict(strict=strict, incl=incl, blk8=blk8, levels=levels, eye=eye, tri=tri, m0=m0, m1=1.0 - m0)


def _rw_chunks(chains):
    n = 2 * RW_CHUNK
    each = lambda f, *cols: [f(*a) for a in zip(*cols)]
    revs, mks = [c[0] for c in chains], [c[1] for c in chains]
    lws, rs_, kks, vs_, ks_, bs_, ss = ([c[i] for c in chains] for i in range(2, 9))
    stack = lambda mk, x: jnp.concatenate([x * mk['m0'], x * mk['m1']], axis=0)
    cums = each(lambda mk, lw: _mm_01(mk['tri'], lw), mks, lws)
    totals = each(lambda rev, cum: cum[0:1] if rev else cum[RW_CHUNK - 1:RW_CHUNK], revs, cums)
    e_negs = each(lambda cum: jnp.exp(-cum), cums)
    e_tots = each(lambda cum, tot: jnp.exp(tot - cum), cums, totals)
    rs = each(lambda mk, r, cum: stack(mk, r * jnp.exp(cum)), mks, rs_, cums)
    ks = each(lambda mk, kk, cum, lw: stack(mk, kk * jnp.exp(cum - lw)), mks, kks, cums, lws)
    vs = each(stack, mks, vs_)
    both = lambda mk, b, k, e: jnp.concatenate([stack(mk, b * e), stack(mk, k * e)], axis=0)
    rhs = each(both, mks, bs_, ks_, e_negs)
    hat = each(both, mks, bs_, ks_, e_tots)
    g1 = each(_mm_nt, ks, rhs)
    g2 = each(_mm_nt, rs, rhs)
    a_b = each(lambda mk, g: jnp.where(mk['strict'], g[:, :n], 0.0), mks, g1)
    a_k = each(lambda mk, g: jnp.where(mk['strict'], g[:, n:], 0.0), mks, g1)
    m_b = each(lambda mk, g: jnp.where(mk['incl'], g[:, :n], 0.0), mks, g2)
    m_k = each(lambda mk, g: jnp.where(mk['incl'], g[:, n:], 0.0), mks, g2)
    x = each(lambda mk, a: -jnp.where(mk['blk8'], a, 0.0), mks, a_b)
    x2 = each(_mm, x, x)
    akv = each(_mm, a_k, vs)
    t_inv = each(lambda mk, x: mk['eye'] + x, mks, x)
    x4 = each(_mm, x2, x2)
    t_inv = each(lambda t, x2: t + _mm(t, x2), t_inv, x2)
    t_inv = each(lambda t, x4: t + _mm(t, x4), t_inv, x4)
    for li in range(3):
        tl = each(lambda mk, t, a: _mm(t, jnp.where(mk['levels'][li], a, 0.0)), mks, t_inv, a_b)
        t_inv = each(lambda t, tl: t - _mm(tl, t), t_inv, tl)
    zr = each(lambda ks, rs, s: _mm_nt(jnp.concatenate([ks, rs], axis=0), s), ks, rs, ss)
    u = each(lambda t, zr, akv: -_mm(t, zr[:n] + akv), t_inv, zr, akv)
    s_new = each(lambda s, tot, u, vs, hat: s * jnp.exp(tot) + _mm(jnp.concatenate([u, vs], axis=0).T, hat),
                 ss, totals, u, vs, hat)
    o_s = each(lambda zr, m_b, m_k, u, vs: zr[n:] + _mm(jnp.concatenate([m_b, m_k], axis=1), jnp.concatenate([u, vs], axis=0)),
               zr, m_b, m_k, u, vs)
    return each(lambda o_s: o_s[:RW_CHUNK] + o_s[RW_CHUNK:], o_s), s_new


def _rw_scan_body(rf_ref, kkf_ref, vf_ref, lwf_ref, kf_ref, bf_ref, rb_ref, kkb_ref, vb_ref, lwb_ref, kb_ref, bb_ref,
                  s0_ref, of_ref, ob_ref, sfin_ref, s_scr):
    c = pl.program_id(1)

    @pl.when(c == 0)
    def _():
        s_scr[...] = s0_ref[0]

    chains, dests = [], []
    for d, refs, o_ref in ((0, (lwf_ref, rf_ref, kkf_ref, vf_ref, kf_ref, bf_ref), of_ref),
                           (1, (lwb_ref, rb_ref, kkb_ref, vb_ref, kb_ref, bb_ref), ob_ref)):
        mk = _rw_masks(rev=(d == 1))
        for p in range(PAIRS):
            sl = slice(p * LANE, (p + 1) * LANE)
            chains.append((d == 1, mk) + tuple(ref[0, :, sl].astype(F32) for ref in refs) + (s_scr[d, p],))
            dests.append((o_ref, sl, d, p))
    outs, states = _rw_chunks(chains)
    for (o_ref, sl, d, p), o, s_new in zip(dests, outs, states):
        o_ref[0, :, sl] = o.astype(o_ref.dtype)
        s_scr[d, p] = s_new

    @pl.when(c == pl.num_programs(1) - 1)
    def _():
        sfin_ref[0] = s_scr[...]


def rw_scan(prep, s0_pairs):
    r, kk, v, lw0, k0, b0, lw1, k1, b1, _ = prep
    b, n, _ = r.shape
    nc = n // RW_CHUNK
    fwd = pl.BlockSpec((1, RW_CHUNK, BR_WIDTH), lambda bi, c: (bi, c, 0))
    bwd = pl.BlockSpec((1, RW_CHUNK, BR_WIDTH), lambda bi, c: (bi, nc - 1 - c, 0))
    st = pl.BlockSpec((1, 2, PAIRS, LANE, LANE), lambda bi, c: (bi, 0, 0, 0, 0))
    return pl.pallas_call(
        _rw_scan_body,
        grid=(b, nc),
        in_specs=[fwd] * 6 + [bwd] * 6 + [st],
        out_specs=[fwd, bwd, st],
        out_shape=[jax.ShapeDtypeStruct((b, n, BR_WIDTH), BF16)] * 2
        + [jax.ShapeDtypeStruct((b, 2, PAIRS, LANE, LANE), F32)],
        scratch_shapes=[pltpu.VMEM((2, PAIRS, LANE, LANE), F32)],
        compiler_params=_cparams("parallel", "arbitrary"),
        name="rw_scan",
    )(r, kk, v, lw0, k0, b0, r, kk, v, lw1, k1, b1, s0_pairs)


def rw_state_to_pairs(s0):
    b = s0.shape[0]
    s = s0.reshape(b, 2, PAIRS, 2, RW_HEAD, RW_HEAD)
    z = jnp.zeros_like(s[:, :, :, 0])
    top = jnp.concatenate([s[:, :, :, 0], z], axis=-1)
    bot = jnp.concatenate([z, s[:, :, :, 1]], axis=-1)
    return jnp.concatenate([top, bot], axis=-2)


def rw_pairs_to_state(sp):
    b = sp.shape[0]
    h0 = sp[:, :, :, :RW_HEAD, :RW_HEAD]
    h1 = sp[:, :, :, RW_HEAD:, RW_HEAD:]
    return jnp.stack([h0, h1], axis=3).reshape(b, 2, RW_HEADS, RW_HEAD, RW_HEAD)


def _rw_post_body(of_ref, ob_ref, bonus_ref, gate_ref, g_ref, bias_ref, o_ref):
    wkv = of_ref[...].astype(F32) + ob_ref[...].astype(F32)
    mu = _head_sum(wkv, RW_HEAD) * (1.0 / RW_HEAD)
    xc = wkv - mu
    var = _head_sum(xc * xc, RW_HEAD) * (1.0 / RW_HEAD)
    o = xc * lax.rsqrt(var + RW_GN_EPS) * g_ref[...] + bias_ref[...] + bonus_ref[...]
    o_ref[...] = (o * _silu(gate_ref[...].astype(F32))).astype(o_ref.dtype)


def rw_post(o_f, o_b, bonus, z_rw2d, ln_g, ln_b, tm):
    t = o_f.shape[0]
    tile = pl.BlockSpec((tm, BR_WIDTH), lambda i: (i, 0))
    return pl.pallas_call(
        _rw_post_body,
        grid=(t // tm,),
        in_specs=[tile] * 4 + [_const_spec((1, BR_WIDTH))] * 2,
        out_specs=tile,
        out_shape=jax.ShapeDtypeStruct((t, BR_WIDTH), BF16),
        compiler_params=_cparams("parallel"),
        name="rw_post",
    )(o_f, o_b, bonus, z_rw2d, ln_g.reshape(1, -1), ln_b.reshape(1, -1))


def _ssd_prep_body(z_ref, zp_ref, zn_ref, w_ref, b_ref, dtb_ref, nega_ref, xbc_ref, dtp_ref):
    cols = slice(BR_WIDTH, BR_WIDTH + SSD_CONV_DIM)
    prev8, next8 = _halo_rows(zp_ref, zn_ref, cols)
    xbc_ref[0] = _silu(_centred_conv(z_ref[0, :, cols], prev8, next8, w_ref, b_ref)).astype(xbc_ref.dtype)
    dt = _softplus(z_ref[0, :, BR_WIDTH + SSD_CONV_DIM:] + dtb_ref[...])
    dta = pltpu.roll(dt, 2 * SSD_HEADS, 1) * nega_ref[...]
    lane = _iota2(dt.shape, 1)
    dtp_ref[0] = jnp.where(lane < 2 * SSD_HEADS, dt, jnp.where(lane < 4 * SSD_HEADS, dta, 0.0))


def ssd_prep(z_ssd, conv_w, conv_b, dt_bias, a_log, tq):
    b, n, c = z_ssd.shape
    dtb = jnp.pad(dt_bias.reshape(1, -1), ((0, 0), (0, LANE - 2 * SSD_HEADS)))
    nega = jnp.pad(-jnp.exp(a_log).reshape(1, -1), ((0, 0), (2 * SSD_HEADS, LANE - 4 * SSD_HEADS)))
    return pl.pallas_call(
        _ssd_prep_body,
        grid=(b, n // tq),
        in_specs=_halo_specs(tq, n, c) + [_const_spec((CONV_K, SSD_CONV_DIM)), _const_spec((1, SSD_CONV_DIM)),
                                           _const_spec((1, LANE)), _const_spec((1, LANE))],
        out_specs=[pl.BlockSpec((1, tq, SSD_CONV_DIM), lambda bi, i: (bi, i, 0)),
                   pl.BlockSpec((1, tq, LANE), lambda bi, i: (bi, i, 0))],
        out_shape=[jax.ShapeDtypeStruct((b, n, SSD_CONV_DIM), BF16), jax.ShapeDtypeStruct((b, n, LANE), F32)],
        compiler_params=_cparams("parallel", "parallel"),
        name="ssd_prep",
    )(z_ssd, z_ssd, z_ssd, conv_w, conv_b.reshape(1, -1), dtb, nega)


def _ssd_chunk(rev, d, xbc, dtp, st_ref):
    q = SSD_Q
    li, si = _iota2((q, q), 0), _iota2((q, q), 1)
    incl = (si >= li) if rev else (si <= li)
    tri = incl.astype(F32)
    head0 = _iota2((q, LANE), 1) < SSD_HEAD_DIM
    m0 = head0.astype(F32)
    m1 = 1.0 - m0
    dt_lane, a_lane = d * SSD_HEADS, (2 + d) * SSD_HEADS

    def per_head_lanes(arr, lane0):
        cols = [arr[:, lane0 + h:lane0 + h + 1] for h in range(SSD_HEADS)]
        return jnp.concatenate([jnp.where(head0, cols[2 * p], cols[2 * p + 1]) for p in range(PAIRS)], axis=1)

    cum = _mm_01(tri, dtp)
    yield
    acs_rows = cum.T[a_lane:a_lane + SSD_HEADS, :]
    acs = per_head_lanes(cum, a_lane)
    a_tot = acs[0:1] if rev else acs[q - 1:q]
    x, bm, cm = xbc[:, :BR_WIDTH], xbc[:, BR_WIDTH:BR_WIDTH + LANE], xbc[:, BR_WIDTH + LANE:]
    xd = x * per_head_lanes(dtp, dt_lane)
    e_acs = jnp.exp(acs)
    xdec = xd * jnp.exp(a_tot - acs)
    e_tot = jnp.exp(a_tot)
    yield
    ys = []
    for g in range(SSD_GROUPS):
        mg = m0 if g == 0 else m1
        cb = _mm_nt(cm * mg, bm)
        bm_t = (bm * mg).T
        yield
        for p in range(2 * g, 2 * g + 2):
            sl = slice(p * LANE, (p + 1) * LANE)
            ms = []
            for hh in range(2):
                h = 2 * p + hh
                col = cum[:, a_lane + h:a_lane + h + 1]
                ms.append(jnp.where(incl, cb * jnp.exp(jnp.minimum(col - acs_rows[h:h + 1, :], 0.0)), 0.0))
            xs = xd[:, sl]
            yield
            y = _mm(jnp.concatenate(ms, axis=1), jnp.concatenate([xs * m0, xs * m1], axis=0))
            y = y + _mm(cm, st_ref[p]) * e_acs[:, sl]
            st_ref[p] = st_ref[p] * e_tot[:, sl] + _mm(bm_t, xdec[:, sl])
            ys.append(y)
            yield
    return jnp.concatenate(ys, axis=1)


def _ssd_scan_body(xf_ref, dtf_ref, xb_ref, dtb_ref, s0_ref, yf_ref, yb_ref, sfin_ref, st_scr):
    c = pl.program_id(1)

    @pl.when(c == 0)
    def _():
        for d in range(2):
            for p in range(PAIRS):
                st_scr[d, p] = s0_ref[0, d, p].T

    y_f, y_b = _in_lockstep(_ssd_chunk(False, 0, xf_ref[0].astype(F32), dtf_ref[0], st_scr.at[0]),
                            _ssd_chunk(True, 1, xb_ref[0].astype(F32), dtb_ref[0], st_scr.at[1]))
    yf_ref[0] = y_f.astype(yf_ref.dtype)
    yb_ref[0] = y_b.astype(yb_ref.dtype)

    @pl.when(c == pl.num_programs(1) - 1)
    def _():
        for d in range(2):
            for p in range(PAIRS):
                sfin_ref[0, d, p] = st_scr[d, p].T


def ssd_scan(xbc, dtp, s0_pairs):
    b, n, _ = xbc.shape
    nc = n // SSD_Q
    fw = lambda w: pl.BlockSpec((1, SSD_Q, w), lambda bi, c: (bi, c, 0))
    bw = lambda w: pl.BlockSpec((1, SSD_Q, w), lambda bi, c: (bi, nc - 1 - c, 0))
    st = pl.BlockSpec((1, 2, PAIRS, LANE, LANE), lambda bi, c: (bi, 0, 0, 0, 0))
    return pl.pallas_call(
        _ssd_scan_body,
        grid=(b, nc),
        in_specs=[fw(SSD_CONV_DIM), fw(LANE), bw(SSD_CONV_DIM), bw(LANE), st],
        out_specs=[fw(BR_WIDTH), bw(BR_WIDTH), st],
        out_shape=[jax.ShapeDtypeStruct((b, n, BR_WIDTH), BF16)] * 2
        + [jax.ShapeDtypeStruct((b, 2, PAIRS, LANE, LANE), F32)],
        scratch_shapes=[pltpu.VMEM((2, PAIRS, LANE, LANE), F32)],
        compiler_params=_cparams("parallel", "arbitrary"),
        name="ssd_scan",
    )(xbc, dtp, xbc, dtp, s0_pairs)


def ssd_state_to_pairs(h0):
    b = h0.shape[0]
    s = h0.reshape(b, 2, PAIRS, LANE, SSD_STATE)
    lo = jnp.pad(s, ((0, 0),) * 4 + ((0, SSD_STATE),))
    hi = jnp.pad(s, ((0, 0),) * 4 + ((SSD_STATE, 0),))
    return jnp.where(_pair_in_group0(), lo, hi)


def _pair_in_group0():
    return (jnp.arange(PAIRS) < PAIRS // SSD_GROUPS)[None, None, :, None, None]


def ssd_pairs_to_state(sp):
    b = sp.shape[0]
    s = jnp.where(_pair_in_group0(), sp[..., :SSD_STATE], sp[..., SSD_STATE:])
    return s.reshape(b, 2, SSD_HEADS, SSD_HEAD_DIM, SSD_STATE)


def _ssd_post_body(yf_ref, yb_ref, x_ref, gate_ref, d_ref, g_ref, o_ref):
    y = (x_ref[...].astype(F32) * d_ref[...] + yf_ref[...].astype(F32) + yb_ref[...].astype(F32)) * _silu(gate_ref[...].astype(F32))
    y = y * lax.rsqrt(jnp.mean(y * y, axis=-1, keepdims=True) + EPS)
    o_ref[...] = (y * g_ref[...]).astype(o_ref.dtype)


def ssd_post(y_f, y_b, xbc2d, z_ssd2d, d_full, norm_g, tm):
    t = y_f.shape[0]
    tile = pl.BlockSpec((tm, BR_WIDTH), lambda i: (i, 0))
    return pl.pallas_call(
        _ssd_post_body,
        grid=(t // tm,),
        in_specs=[tile] * 4 + [_const_spec((1, BR_WIDTH))] * 2,
        out_specs=tile,
        out_shape=jax.ShapeDtypeStruct((t, BR_WIDTH), BF16),
        compiler_params=_cparams("parallel"),
        name="ssd_post",
    )(y_f, y_b, xbc2d, z_ssd2d, d_full.reshape(1, -1), norm_g.reshape(1, -1))


def _lru_prep_body(z_ref, zp_ref, zn_ref, w_ref, b_ref, wa_ref, ba_ref, wx_ref, bx_ref, sp_ref,
                   a0_ref, u0_ref, a1_ref, u1_ref):
    cols = slice(0, LRU_WIDTH)
    prev8, next8 = _halo_rows(zp_ref, zn_ref, cols)
    xc = _centred_conv(z_ref[0, :, cols].astype(F32), prev8, next8, w_ref, b_ref)
    for d, (a_ref, u_ref) in enumerate(((a0_ref, u0_ref), (a1_ref, u1_ref))):
        g_r = _sigmoid(_mm(xc, wa_ref[d]) + ba_ref[d])
        g_i = _sigmoid(_mm(xc, wx_ref[d]) + bx_ref[d])
        log_a = -LRU_C * g_r * sp_ref[d]
        t = jnp.tanh(log_a)
        a_ref[0] = jnp.exp(log_a)
        u_ref[0] = (jnp.sqrt(-2.0 * t / (1.0 - t)) * (g_i * xc)).astype(u_ref.dtype)


def _block_diag(w):
    d, k, bi, bo = w.shape
    return jnp.einsum('dkij,kl->dkilj', w, jnp.eye(k, dtype=w.dtype)).reshape(d, k * bi, k * bo)


def lru_prep(z_lru, conv_w, conv_b, wa, ba, wx, bx, lam, tq):
    b, n, c = z_lru.shape
    tile = pl.BlockSpec((1, tq, LRU_WIDTH), lambda bi, i: (bi, i, 0))
    out = jax.ShapeDtypeStruct((b, n, LRU_WIDTH), F32)
    sp = _softplus(-lam)[:, None, :]
    return pl.pallas_call(
        _lru_prep_body,
        grid=(b, n // tq),
        in_specs=_halo_specs(tq, n, c) + [_const_spec((CONV_K, LRU_WIDTH)), _const_spec((1, LRU_WIDTH)),
                                           _const_spec((2, LRU_WIDTH, LRU_WIDTH)), _const_spec((2, 1, LRU_WIDTH)),
                                           _const_spec((2, LRU_WIDTH, LRU_WIDTH)), _const_spec((2, 1, LRU_WIDTH)),
                                           _const_spec((2, 1, LRU_WIDTH))],
        out_specs=[tile] * 4,
        out_shape=[out, jax.ShapeDtypeStruct((b, n, LRU_WIDTH), BF16)] * 2,
        compiler_params=_cparams("parallel", "parallel"),
        name="lru_prep",
    )(z_lru, z_lru, z_lru, conv_w, conv_b.reshape(1, -1), _block_diag(wa).astype(BF16), ba[:, None, :],
      _block_diag(wx).astype(BF16), bx[:, None, :], sp)


def _lru_tile_scan(rev, a, u, carry):
    n = a.shape[0]
    sub = _iota2(a.shape, 0) & (SUBLANE - 1)
    s = 1
    while s < SUBLANE:
        shift = n - s if rev else s
        keep = (sub < SUBLANE - s) if rev else (sub >= s)
        a_sh = jnp.where(keep, pltpu.roll(a, shift, 0), 1.0)
        u_sh = jnp.where(keep, pltpu.roll(u, shift, 0), 0.0)
        u = u + a * u_sh
        a = a * a_sh
        s *= 2
        yield
    groups = range(n // SUBLANE)
    hs = [None] * len(groups)
    for g in (reversed(groups) if rev else groups):
        rows = slice(g * SUBLANE, (g + 1) * SUBLANE)
        hs[g] = a[rows] * carry + u[rows]
        carry = hs[g][0:1] if rev else hs[g][SUBLANE - 1:SUBLANE]
        yield
    return jnp.concatenate(hs, axis=0), carry


def _lru_scan_body(af_ref, uf_ref, ab_ref, ub_ref, h0_ref, yf_ref, yb_ref, hfin_ref, h_scr):
    c = pl.program_id(1)

    @pl.when(c == 0)
    def _():
        h_scr[...] = h0_ref[0]

    (hf, cf), (hb, cb) = _in_lockstep(_lru_tile_scan(False, af_ref[0], uf_ref[0].astype(F32), h_scr[0:1]),
                                      _lru_tile_scan(True, ab_ref[0], ub_ref[0].astype(F32), h_scr[1:2]))
    yf_ref[0] = hf.astype(yf_ref.dtype)
    h_scr[0:1] = cf
    yb_ref[0] = hb.astype(yb_ref.dtype)
    h_scr[1:2] = cb

    @pl.when(c == pl.num_programs(1) - 1)
    def _():
        hfin_ref[0] = h_scr[...]


def lru_scan(a0, u0, a1, u1, h0, tq):
    b, n, _ = a0.shape
    nc = n // tq
    fw = pl.BlockSpec((1, tq, LRU_WIDTH), lambda bi, c: (bi, c, 0))
    bw = pl.BlockSpec((1, tq, LRU_WIDTH), lambda bi, c: (bi, nc - 1 - c, 0))
    st = pl.BlockSpec((1, 2, LRU_WIDTH), lambda bi, c: (bi, 0, 0))
    return pl.pallas_call(
        _lru_scan_body,
        grid=(b, nc),
        in_specs=[fw, fw, bw, bw, st],
        out_specs=[fw, bw, st],
        out_shape=[jax.ShapeDtypeStruct((b, n, LRU_WIDTH), BF16)] * 2 + [jax.ShapeDtypeStruct((b, 2, LRU_WIDTH), F32)],
        scratch_shapes=[pltpu.VMEM((2, LRU_WIDTH), F32)],
        compiler_params=_cparams("parallel", "arbitrary"),
        name="lru_scan",
    )(a0, u0, a1, u1, h0)


def _lru_post_body(yf_ref, yb_ref, gate_ref, o_ref):
    o_ref[...] = ((yf_ref[...].astype(F32) + yb_ref[...].astype(F32)) * _silu(gate_ref[...].astype(F32))).astype(o_ref.dtype)


def lru_post(y_f, y_b, z_lru2d, tm):
    t = y_f.shape[0]
    tile = pl.BlockSpec((tm, LRU_WIDTH), lambda i: (i, 0))
    return pl.pallas_call(
        _lru_post_body,
        grid=(t // tm,),
        in_specs=[tile, tile, pl.BlockSpec((tm, LRU_WIDTH), lambda i: (i, 1))],
        out_specs=tile,
        out_shape=jax.ShapeDtypeStruct((t, LRU_WIDTH), BF16),
        compiler_params=_cparams("parallel"),
        name="lru_post",
    )(y_f, y_b, z_lru2d)


def rope_tables(n):
    t = np.arange(n)
    pos = jnp.asarray(np.stack([t // GRID_W, t % GRID_W], axis=0).astype(np.float32))
    nf = MLA_ROPE // 4
    inv = jnp.asarray(ROPE_THETA, F32) ** (-jnp.arange(nf, dtype=F32) / nf)
    rel = np.arange(LANE) - MLA_NOPE
    in_rope = (rel >= 0) & (rel < MLA_ROPE)
    half = np.clip(rel // (2 * nf), 0, 1)
    within = rel % (2 * nf)
    is_x2 = within >= nf
    ang = pos[half, :].T * inv[within % nf][None, :]
    cos = jnp.where(in_rope[None, :], jnp.cos(ang), 1.0)
    sin = jnp.sin(ang)
    sin_a = jnp.where((in_rope & ~is_x2)[None, :], -sin, 0.0)
    sin_b = jnp.where((in_rope & is_x2)[None, :], sin, 0.0)
    return cos, sin_a, sin_b


def _head_norm_rope(x, g, rope):
    nf = MLA_ROPE // 4
    outs = []
    for h in range(MLA_HEADS):
        xh = x[:, h * LANE:(h + 1) * LANE]
        ms = jnp.sum(xh * xh, axis=-1, keepdims=True) * (1.0 / MLA_QK)
        xh = xh * lax.rsqrt(ms + EPS) * g
        if rope is not None:
            cos, sin_a, sin_b = rope
            xh = xh * cos + pltpu.roll(xh, LANE - nf, 1) * sin_a + pltpu.roll(xh, nf, 1) * sin_b
        outs.append(xh)
    return jnp.concatenate(outs, axis=1)


def _mla_q_body(use_rope, q_ref, qag_ref, wq_ref, qng_ref, *rest):
    *rope_refs, o_ref = rest
    x = q_ref[:, :MLA_Q_RANK].astype(F32)
    x = x * lax.rsqrt(jnp.mean(x * x, axis=-1, keepdims=True) + EPS) * qag_ref[...]
    q = _mm(x, wq_ref[...])
    rope = tuple(r[...] for r in rope_refs) if use_rope else None
    q = _head_norm_rope(q, qng_ref[...], rope)
    o_ref[...] = (q * (MLA_QK ** -0.5)).astype(o_ref.dtype)


def mla_q(z_mla, qa_g, wq_pad, qn_g_pad, rope, n, tm):
    t = z_mla.shape[0]
    per = n // tm
    in_specs = [pl.BlockSpec((tm, 512), lambda i: (i, MLA_Q_BLOCK)), _const_spec((1, MLA_Q_RANK)),
                _const_spec((MLA_Q_RANK, MLA_W)), _const_spec((1, LANE))]
    args = [z_mla, qa_g.reshape(1, -1), wq_pad, qn_g_pad]
    if rope is not None:
        in_specs += [pl.BlockSpec((tm, LANE), lambda i: (i % per, 0))] * 3
        args += list(rope)
    return pl.pallas_call(
        lambda *refs: _mla_q_body(rope is not None, *refs),
        grid=(t // tm,),
        in_specs=in_specs,
        out_specs=pl.BlockSpec((tm, MLA_W), lambda i: (i, 0)),
        out_shape=jax.ShapeDtypeStruct((t, MLA_W), BF16),
        compiler_params=_cparams("parallel"),
        name="mla_q",
    )(*args)


def _mla_kv_body(norm_ckv, use_rope, c_ref, kr_ref, kvag_ref, wk_ref, wv_ref, kng_ref, *rest):
    *rope_refs, cn_ref, k_ref, v_ref = rest
    c = c_ref[...].astype(F32)
    if norm_ckv:
        c = c * lax.rsqrt(jnp.mean(c * c, axis=-1, keepdims=True) + EPS) * kvag_ref[...]
    cn_ref[...] = c
    k = _mm(c, wk_ref[...]) + kr_ref[...].astype(F32)
    rope = tuple(r[...] for r in rope_refs) if use_rope else None
    k_ref[...] = _head_norm_rope(k, kng_ref[...], rope).astype(k_ref.dtype)
    v_ref[...] = _mm(c, wv_ref[...]).astype(v_ref.dtype)


def mla_kv(c_arr, c_block, kr_arr, kr_block, kva_g, wk_pad, wv, kn_g_pad, rope, norm_ckv, n, tm):
    t = c_arr.shape[0]
    per = n // tm
    in_specs = [pl.BlockSpec((tm, MLA_KV_RANK), lambda i: (i, c_block)),
                pl.BlockSpec((tm, MLA_W), lambda i: (i, kr_block)),
                _const_spec((1, MLA_KV_RANK)), _const_spec((MLA_KV_RANK, MLA_W)), _const_spec((MLA_KV_RANK, BR_WIDTH)),
                _const_spec((1, LANE))]
    args = [c_arr, kr_arr, kva_g.reshape(1, -1), wk_pad, wv, kn_g_pad]
    if rope is not None:
        in_specs += [pl.BlockSpec((tm, LANE), lambda i: (i % per, 0))] * 3
        args += list(rope)
    return pl.pallas_call(
        lambda *refs: _mla_kv_body(norm_ckv, rope is not None, *refs),
        grid=(t // tm,),
        in_specs=in_specs,
        out_specs=[pl.BlockSpec((tm, MLA_KV_RANK), lambda i: (i, 0)), pl.BlockSpec((tm, MLA_W), lambda i: (i, 0)),
                   pl.BlockSpec((tm, BR_WIDTH), lambda i: (i, 0))],
        out_shape=[jax.ShapeDtypeStruct((t, MLA_KV_RANK), F32), jax.ShapeDtypeStruct((t, MLA_W), BF16),
                   jax.ShapeDtypeStruct((t, BR_WIDTH), BF16)],
        compiler_params=_cparams("parallel"),
        name="mla_kv",
    )(*args)


def _attn_body(n_src, q_ref, gate_ref, *rest):
    kv_refs = rest[:2 * n_src]
    o_ref = rest[2 * n_src]
    scr = rest[2 * n_src + 1:]
    tq = q_ref.shape[0]
    part = tq // ATTN_SPLIT
    offs = [0]
    for s_i in range(n_src):
        offs.append(offs[-1] + kv_refs[2 * s_i].shape[1])
    tiles = [(h, r) for h in range(2) for r in range(ATTN_SPLIT)]
    out = {}

    def scores(t):
        h, r = tiles[t]
        s_scr = scr[3 * t]
        for s_i in range(n_src):
            s_scr[:, offs[s_i]:offs[s_i + 1]] = lax.dot_general(
                q_ref[r * part:(r + 1) * part, h * LANE:(h + 1) * LANE], kv_refs[2 * s_i][0, :, h * LANE:(h + 1) * LANE],
                (((1,), (1,)), ((), ())), preferred_element_type=F32)

    def softmax(t):
        s_scr, p_scr, l_scr = scr[3 * t:3 * t + 3]
        for i in range(part // ATTN_ROWS):
            rows = slice(i * ATTN_ROWS, (i + 1) * ATTN_ROWS)
            m = jnp.max(s_scr[rows, :], axis=-1, keepdims=True)
            l = jnp.zeros((ATTN_ROWS, 1), F32)
            for c0 in range(0, offs[-1], ATTN_COLS):
                cols = slice(c0, min(c0 + ATTN_COLS, offs[-1]))
                p = jnp.exp(s_scr[rows, cols] - m)
                p_scr[rows, cols] = p.astype(BF16)
                l = l + jnp.sum(p, axis=-1, keepdims=True)
            l_scr[rows, :] = jnp.broadcast_to(l, (ATTN_ROWS, LANE))

    def values(t):
        _, p_scr, l_scr = scr[3 * t:3 * t + 3]
        acc = None
        for s_i in range(n_src):
            pv = jnp.dot(p_scr[:, offs[s_i]:offs[s_i + 1]], kv_refs[2 * s_i + 1][0], preferred_element_type=F32)
            acc = pv if acc is None else acc + pv
        out[tiles[t]] = acc / l_scr[...]

    nt = len(tiles)
    scores(0)
    for t in range(nt):
        if t + 1 < nt:
            scores(t + 1)
        softmax(t)
        if t >= 1:
            values(t - 1)
    values(nt - 1)
    accs = [jnp.concatenate([out[(h, r)] for r in range(ATTN_SPLIT)], axis=0) for h in range(2)]
    o = jnp.where(_iota2((tq, LANE), 1) < MLA_V, accs[0], accs[1])
    o_ref[...] = (o * _silu(gate_ref[...].astype(F32))).astype(o_ref.dtype)


def mla_attention(q, z_mla, kvs, n, tq):
    t = q.shape[0]
    b = t // n
    per = n // tq
    in_specs = [pl.BlockSpec((tq, 2 * LANE), lambda bi, p, i: (bi * per + i, p)),
                pl.BlockSpec((tq, LANE), lambda bi, p, i: (bi * per + i, MLA_GATE_LANE_BLOCK + p))]
    args = [q, z_mla]
    for k, v in kvs:
        nk = k.shape[1]
        in_specs += [pl.BlockSpec((1, nk, 2 * LANE), lambda bi, p, i: (bi, 0, p)),
                     pl.BlockSpec((1, nk, LANE), lambda bi, p, i: (bi, 0, p))]
        args += [k, v]
    n_keys = sum(k.shape[1] for k, _ in kvs)
    return pl.pallas_call(
        lambda *refs: _attn_body(len(kvs), *refs),
        grid=(b, PAIRS, per),
        in_specs=in_specs,
        out_specs=pl.BlockSpec((tq, LANE), lambda bi, p, i: (bi * per + i, p)),
        out_shape=jax.ShapeDtypeStruct((t, BR_WIDTH), BF16),
        scratch_shapes=[pltpu.VMEM((tq // ATTN_SPLIT, n_keys), F32), pltpu.VMEM((tq // ATTN_SPLIT, n_keys), BF16),
                        pltpu.VMEM((tq // ATTN_SPLIT, LANE), F32)] * (2 * ATTN_SPLIT),
        compiler_params=_cparams("parallel", "parallel", "arbitrary"),
        name="mla_attention",
    )(*args)


def pad_heads(w, real, lead_zero=0):
    shp = w.shape[:-1]
    w = w.reshape(*shp, MLA_HEADS, real)
    w = jnp.pad(w, [(0, 0)] * len(shp) + [(0, 0), (lead_zero, LANE - real - lead_zero)])
    return w.reshape(*shp, MLA_W)


def pad_lane(g):
    return jnp.pad(g.reshape(1, -1), ((0, 0), (0, LANE - g.size)))


def place_rope_key(k_rope):
    return pad_heads(jnp.tile(k_rope, (1,) * (k_rope.ndim - 1) + (MLA_HEADS,)), MLA_ROPE, MLA_NOPE)


def pack_weights(P, l):
    w = P['w_in'][l]
    col = lambda i: w[:, IN_OFFS[i]:IN_OFFS[i + 1]]
    zeros = lambda n: jnp.zeros((D_MODEL, n), w.dtype)
    mla_kv = col(3)
    w_rw = jnp.concatenate([col(1), col(0)], axis=1)
    w_mla = jnp.concatenate([place_rope_key(mla_kv[:, MLA_KV_RANK:]), col(4), col(2), zeros(512 - MLA_Q_RANK),
                             mla_kv[:, :MLA_KV_RANK]], axis=1)
    w_ssd = jnp.concatenate([col(5), col(6), col(7), zeros(LANE - 2 * SSD_HEADS)], axis=1)
    w_lru = jnp.concatenate([col(8), col(9)], axis=1)
    kvu = P['mla_kv_up'][l].reshape(MLA_KV_RANK, MLA_HEADS, MLA_NOPE + MLA_V)
    bf = lambda x: x.astype(BF16)
    return dict(w_rw=bf(w_rw), w_mla=bf(w_mla), w_ssd=bf(w_ssd), w_lru=bf(w_lru), w_mrg=bf(0.5 * col(10)),
                wq=bf(pad_heads(P['mla_q_up'][l], MLA_QK)),
                wk=bf(pad_heads(kvu[:, :, :MLA_NOPE].reshape(MLA_KV_RANK, -1), MLA_NOPE)),
                wv=bf(kvu[:, :, MLA_NOPE:].reshape(MLA_KV_RANK, -1)),
                w_branch=bf(P['w_branch'][l]), w_out=bf(P['w_out'][l]))


def trunk_layer(x, mod, P, W, l, ctx):
    b, n, d = x.shape
    t = b * n
    tm = min(n, 512)
    tq = 256
    shift, scale, gate = mod
    x2d = x.reshape(t, d)
    h = prenorm(x2d, P['norm_g'][l], scale, shift, n, tm)
    z_rw = matmul(h, W['w_rw'], tm, Z_RW_W, BF16)
    z_mla = matmul(h, W['w_mla'], tm, Z_MLA_W, BF16)
    z_ssd = matmul(h, W['w_ssd'], tm, Z_SSD_W, F32)
    z_lru = matmul(h, W['w_lru'], tm, Z_LRU_W, BF16)
    logits = matmul(h, W['w_mrg'], tm, 2048, BF16)
    if ctx is None:
        rw0 = jnp.zeros((b, 2, PAIRS, LANE, LANE), F32)
        ssd0 = jnp.zeros((b, 2, PAIRS, LANE, LANE), F32)
        lru0 = jnp.zeros((b, 2, LRU_WIDTH), F32)
        rope = None
    else:
        ckv0, krope0, rw0, ssd0, lru0 = ctx
        rw0, ssd0 = rw_state_to_pairs(rw0), ssd_state_to_pairs(ssd0)
        rope = rope_tables(n)

    prep = rw_prep(z_rw.reshape(b, n, -1), P, l, tq)
    o_f, o_b, rw_fin = rw_scan(prep, rw0)
    flat = lambda a: a.reshape(t, -1)
    br_rw = rw_post(flat(o_f), flat(o_b), flat(prep[9]), z_rw, P['rw_ln_g'][l], P['rw_ln_b'][l], tm)

    qn_g, kn_g = pad_lane(P['mla_qn_g'][l]), pad_lane(P['mla_kn_g'][l])
    q = mla_q(z_mla, P['mla_qa_g'][l], W['wq'], qn_g, rope, n, tm)
    c_kv, k, v = mla_kv(z_mla, MLA_CKV_BLOCK, z_mla, 0, P['mla_kva_g'][l], W['wk'], W['wv'], kn_g, rope, True, n, tm)
    kvs = [(k.reshape(b, n, -1), v.reshape(b, n, -1))]
    if ctx is not None:
        past = ckv0.shape[1]
        _, k0, v0 = mla_kv(ckv0.reshape(b * past, -1), 0, place_rope_key(krope0).reshape(b * past, -1), 0,
                           P['mla_kva_g'][l], W['wk'], W['wv'], kn_g, None, False, past, past)
        kvs = [(k0.reshape(b, past, -1), v0.reshape(b, past, -1))] + kvs
    br_mla = mla_attention(q, z_mla, kvs, n, min(n, ATTN_TQ))
    k_rope = z_mla[:, MLA_NOPE:MLA_NOPE + MLA_ROPE].astype(F32)

    xbc, dtp = ssd_prep(z_ssd.reshape(b, n, -1), P['ssd_conv_w'][l], P['ssd_conv_b'][l], P['ssd_dt_bias'][l],
                        P['ssd_a_log'][l], tq)
    y_f, y_b, ssd_fin = ssd_scan(xbc, dtp, ssd0)
    br_ssd = ssd_post(flat(y_f), flat(y_b), flat(xbc), z_ssd, jnp.repeat(P['ssd_d'][l], SSD_HEAD_DIM),
                      P['ssd_norm_g'][l], tm)

    a0, u0, a1, u1 = lru_prep(z_lru.reshape(b, n, -1), P['lru_conv_w'][l], P['lru_conv_b'][l], P['lru_wa'][l],
                              P['lru_ba'][l], P['lru_wx'][l], P['lru_bx'][l], P['lru_lambda'][l], tq)
    h_f, h_b, lru_fin = lru_scan(a0, u0, a1, u1, lru0, tq)
    br_lru = lru_post(flat(h_f), flat(h_b), z_lru, tm)

    y = merge(x2d, [br_rw, br_mla, br_ssd, br_lru], logits, W['w_branch'], W['w_out'], gate, n, tm)
    states = (c_kv.reshape(b, n, -1), k_rope.reshape(b, n, -1), rw_pairs_to_state(rw_fin),
              ssd_pairs_to_state(ssd_fin), lru_fin)
    return y.reshape(b, n, d), states


def kernel(x_prompt, x_sample, cache_mla_ckv, cache_mla_krope, state_rwkv, state_ssd, state_lru, c, c_ctx, ada_w, ada_b, norm_g, w_in, rw_mu, rw_w0, rw_w2, rw_a0, rw_a2, rw_kk, rw_ka, rw_rk, rw_ln_g, rw_ln_b, mla_qa_g, mla_q_up, mla_kva_g, mla_kv_up, mla_qn_g, mla_kn_g, ssd_conv_w, ssd_conv_b, ssd_dt_bias, ssd_a_log, ssd_d, ssd_norm_g, lru_conv_w, lru_conv_b, lru_wa, lru_ba, lru_wx, lru_bx, lru_lambda, w_branch, w_out):
    P = dict(norm_g=norm_g, w_in=w_in,
             rw_mu=rw_mu, rw_w0=rw_w0, rw_w2=rw_w2, rw_a0=rw_a0, rw_a2=rw_a2, rw_kk=rw_kk,
             rw_ka=rw_ka, rw_rk=rw_rk.reshape(DEPTH, BR_WIDTH), rw_ln_g=rw_ln_g, rw_ln_b=rw_ln_b,
             mla_qa_g=mla_qa_g, mla_q_up=mla_q_up, mla_kva_g=mla_kva_g, mla_kv_up=mla_kv_up,
             mla_qn_g=mla_qn_g, mla_kn_g=mla_kn_g,
             ssd_conv_w=ssd_conv_w, ssd_conv_b=ssd_conv_b, ssd_dt_bias=ssd_dt_bias,
             ssd_a_log=ssd_a_log, ssd_d=ssd_d, ssd_norm_g=ssd_norm_g,
             lru_conv_w=lru_conv_w, lru_conv_b=lru_conv_b, lru_wa=lru_wa, lru_ba=lru_ba,
             lru_wx=lru_wx, lru_bx=lru_bx, lru_lambda=lru_lambda,
             w_branch=w_branch, w_out=w_out)
    dec_batch = c.shape[0]
    cond8 = jnp.concatenate([c_ctx[None, :], c, jnp.zeros((SUBLANE - 1 - dec_batch, D_MODEL), F32)], axis=0)
    weights, mods = [], []
    for l in range(DEPTH):
        weights.append(pack_weights(P, l))
        mod = ada_mod(cond8, ada_w[l], ada_b[l])
        mods.append(tuple(jnp.split(mod[:, None, :], 3, axis=-1)))

    y_prompt = x_prompt
    ctx_states = []
    for l in range(DEPTH):
        y_prompt, st = trunk_layer(y_prompt, tuple(m[0:1] for m in mods[l]), P, weights[l], l, None)
        ctx_states.append(st)
    new_states = tuple(jnp.stack([s[i] for s in ctx_states], axis=1) for i in range(5))

    y_sample = x_sample
    for l in range(DEPTH):
        ctx = (cache_mla_ckv[:, l], cache_mla_krope[:, l], state_rwkv[:, l], state_ssd[:, l], state_lru[:, l])
        y_sample, _ = trunk_layer(y_sample, tuple(m[1:1 + dec_batch] for m in mods[l]), P, weights[l], l, ctx)

    return (y_prompt, y_sample) + new_states
```

```python
import math

import jax
import jax.numpy as jnp
import numpy as np
from jax import lax
from jax.experimental import pallas as pl
from jax.experimental.pallas import tpu as pltpu

F32 = jnp.float32
BF16 = jnp.bfloat16

D_MODEL = 1024
DEPTH = 2
GRID_W = 64
N_BRANCH = 4
BR_WIDTH = 512
EPS = 1e-6
CONV_K = 4

RW_HEADS = 8
RW_HEAD = 64
RW_LORA = 64
RW_GN_EPS = 64e-5
RW_PRE = 3 * BR_WIDTH + 2 * RW_LORA

MLA_HEADS = 8
MLA_NOPE = 64
MLA_ROPE = 32
MLA_QK = MLA_NOPE + MLA_ROPE
MLA_V = 64
MLA_Q_RANK = 384
MLA_KV_RANK = 256
ROPE_THETA = 10000.0

SSD_HEADS = 8
SSD_HEAD_DIM = 64
SSD_GROUPS = 2
SSD_STATE = 64
SSD_CONV_DIM = BR_WIDTH + 2 * SSD_GROUPS * SSD_STATE

LRU_WIDTH = 512
LRU_BLOCKS = 8
LRU_C = 8.0

IN_WIDTHS = (RW_PRE, BR_WIDTH, MLA_Q_RANK, MLA_KV_RANK + MLA_ROPE, BR_WIDTH, BR_WIDTH, SSD_CONV_DIM,
             2 * SSD_HEADS, LRU_WIDTH, LRU_WIDTH, N_BRANCH * D_MODEL)
IN_OFFS = tuple(int(v) for v in np.cumsum((0,) + IN_WIDTHS))

LANE = 128
SUBLANE = 8
HALO = 16
VMEM_LIMIT = 48 * 1024 * 1024

RW_CHUNK = 64
RW_STEP_CHUNKS = 2
PAIRS = BR_WIDTH // LANE
SSD_Q = 128
MLA_W = MLA_HEADS * LANE
ATTN_TQ = 512
ATTN_SPLIT = 2
ATTN_ROWS = 16
ATTN_COLS = 640

Z_RW_W = BR_WIDTH + RW_PRE
Z_MLA_W = MLA_W + BR_WIDTH + 512 + MLA_KV_RANK
Z_SSD_W = BR_WIDTH + SSD_CONV_DIM + LANE
Z_LRU_W = 2 * LRU_WIDTH
MLA_GATE_LANE_BLOCK = MLA_W // LANE
MLA_Q_BLOCK = (MLA_W + BR_WIDTH) // 512
MLA_CKV_BLOCK = (MLA_W + BR_WIDTH + 512) // MLA_KV_RANK


def _cparams(*sem):
    return pltpu.CompilerParams(dimension_semantics=sem, vmem_limit_bytes=VMEM_LIMIT)


def _const_spec(shape):
    return pl.BlockSpec(shape, lambda *_: (0,) * len(shape))


def _mm(a, b):
    return jnp.dot(a.astype(BF16), b.astype(BF16), preferred_element_type=F32)


def _mm_nt(a, b):
    return lax.dot_general(a.astype(BF16), b.astype(BF16), (((1,), (1,)), ((), ())), preferred_element_type=F32)


def _mm_01(m01, x):
    m01 = m01.astype(BF16)
    x1 = x.astype(BF16)
    r1 = x - x1.astype(F32)
    x2 = r1.astype(BF16)
    x3 = (r1 - x2.astype(F32)).astype(BF16)
    return (jnp.dot(m01, x1, preferred_element_type=F32) + jnp.dot(m01, x2, preferred_element_type=F32)
            + jnp.dot(m01, x3, preferred_element_type=F32))


def _iota2(shape, dim):
    return lax.broadcasted_iota(jnp.int32, shape, dim)


def _sigmoid(x):
    return 0.5 * jnp.tanh(0.5 * x) + 0.5


def _silu(x):
    return x * _sigmoid(x)


def _softplus(x):
    return jnp.maximum(x, 0.0) + jnp.log(1.0 + jnp.exp(-jnp.abs(x)))


def _in_lockstep(*gens):
    results = [None] * len(gens)
    live = list(range(len(gens)))
    while live:
        for i in list(live):
            try:
                next(gens[i])
            except StopIteration as done:
                results[i] = done.value
                live.remove(i)
    return results


def _head_sum(x, width):
    bd = ((_iota2((LANE, LANE), 0) // width) == (_iota2((LANE, LANE), 1) // width)).astype(BF16)
    hi = x.astype(BF16)
    lo = (x - hi.astype(F32)).astype(BF16)
    outs = []
    for j in range(x.shape[-1] // LANE):
        sl = slice(j * LANE, (j + 1) * LANE)
        outs.append(jnp.dot(hi[:, sl], bd, preferred_element_type=F32) + jnp.dot(lo[:, sl], bd, preferred_element_type=F32))
    return jnp.concatenate(outs, axis=1)


def _from_prev(u, s, prev8):
    rolled = pltpu.roll(u, s, 0)
    head = jnp.where(_iota2(prev8.shape, 0) < s, pltpu.roll(prev8, s, 0), rolled[:SUBLANE])
    return jnp.concatenate([head, rolled[SUBLANE:]], axis=0)


def _from_next(u, next8):
    n = u.shape[0]
    rolled = pltpu.roll(u, n - 1, 0)
    tail = jnp.where(_iota2(next8.shape, 0) == SUBLANE - 1, pltpu.roll(next8, SUBLANE - 1, 0), rolled[n - SUBLANE:])
    return jnp.concatenate([rolled[:n - SUBLANE], tail], axis=0)


def _centred_conv(x, prev8, next8, w_ref, b_ref):
    return (w_ref[0:1, :] * _from_prev(x, 2, prev8) + w_ref[1:2, :] * _from_prev(x, 1, prev8)
            + w_ref[2:3, :] * x + w_ref[3:4, :] * _from_next(x, next8) + b_ref[...])


def _halo_specs(tq, n, c):
    per, nb = tq // HALO, n // HALO
    return [pl.BlockSpec((1, tq, c), lambda bi, i: (bi, i, 0)),
            pl.BlockSpec((1, HALO, c), lambda bi, i: (bi, jnp.maximum(i * per - 1, 0), 0)),
            pl.BlockSpec((1, HALO, c), lambda bi, i: (bi, jnp.minimum((i + 1) * per, nb - 1), 0))]


def _halo_rows(zp_ref, zn_ref, cols):
    i = pl.program_id(1)
    prev8 = jnp.where(i > 0, zp_ref[0, HALO - SUBLANE:, cols].astype(F32), 0.0)
    next8 = jnp.where(i < pl.num_programs(1) - 1, zn_ref[0, :SUBLANE, cols].astype(F32), 0.0)
    return prev8, next8


def _ada_body(c_ref, w_ref, b_ref, o_ref):
    o_ref[...] = _mm(_silu(c_ref[...]), w_ref[...]) + b_ref[...]


def ada_mod(cond8, w, bias):
    d = cond8.shape[1]
    return pl.pallas_call(
        _ada_body,
        grid=(w.shape[1] // d,),
        in_specs=[_const_spec((SUBLANE, d)), pl.BlockSpec((d, d), lambda j: (0, j)), pl.BlockSpec((1, d), lambda j: (0, j))],
        out_specs=pl.BlockSpec((SUBLANE, d), lambda j: (0, j)),
        out_shape=jax.ShapeDtypeStruct((SUBLANE, w.shape[1]), F32),
        compiler_params=_cparams("parallel"),
        name="ada_mod",
    )(cond8, w, bias.reshape(1, -1))


def _prenorm_body(x_ref, g_ref, sc_ref, sh_ref, h_ref):
    x = x_ref[...]
    y = x * lax.rsqrt(jnp.mean(x * x, axis=-1, keepdims=True) + EPS)
    h_ref[...] = ((y * g_ref[...]) * (1.0 + sc_ref[0]) + sh_ref[0]).astype(h_ref.dtype)


def _mod_index(bm, per):
    return (lambda i: (i // per, 0, 0)) if bm > 1 else (lambda i: (0, 0, 0))


def prenorm(x2d, g, scale, shift, n, tm):
    t, d = x2d.shape
    mod_idx = _mod_index(scale.shape[0], n // tm)
    return pl.pallas_call(
        _prenorm_body,
        grid=(t // tm,),
        in_specs=[pl.BlockSpec((tm, d), lambda i: (i, 0)), _const_spec((1, d)),
                  pl.BlockSpec((1, 1, d), mod_idx), pl.BlockSpec((1, 1, d), mod_idx)],
        out_specs=pl.BlockSpec((tm, d), lambda i: (i, 0)),
        out_shape=jax.ShapeDtypeStruct((t, d), BF16),
        compiler_params=_cparams("parallel"),
        name="prenorm",
    )(x2d, g.reshape(1, -1), scale, shift)


def _mm_body(a_ref, w_ref, o_ref):
    o_ref[...] = jnp.dot(a_ref[...], w_ref[...], preferred_element_type=F32).astype(o_ref.dtype)


def matmul(a, w, tm, tn, out_dtype):
    t, k = a.shape
    n = w.shape[1]
    return pl.pallas_call(
        _mm_body,
        grid=(n // tn, t // tm),
        in_specs=[pl.BlockSpec((tm, k), lambda j, i: (i, 0)), pl.BlockSpec((k, tn), lambda j, i: (0, j))],
        out_specs=pl.BlockSpec((tm, tn), lambda j, i: (i, j)),
        out_shape=jax.ShapeDtypeStruct((t, n), out_dtype),
        compiler_params=_cparams("parallel", "parallel"),
        name="in_proj",
    )(a, w)


def _merge_body(x_ref, b0_ref, b1_ref, b2_ref, b3_ref, lg_ref, wb_ref, wo_ref, gate_ref, y_ref):
    merged = None
    for m, b_ref in enumerate((b0_ref, b1_ref, b2_ref, b3_ref)):
        proj = jnp.dot(b_ref[...], wb_ref[m], preferred_element_type=F32)
        g = jnp.tanh(lg_ref[:, m * D_MODEL:(m + 1) * D_MODEL].astype(F32)) + 1.0
        merged = g * proj if merged is None else merged + g * proj
    out = jnp.dot(merged.astype(BF16), wo_ref[...], preferred_element_type=F32)
    y_ref[...] = x_ref[...] + (0.5 * gate_ref[0]) * out


def merge(x2d, branches, logits, wb, wo, gate, n, tm):
    t, d = x2d.shape
    mod_idx = _mod_index(gate.shape[0], n // tm)
    row = lambda i: (i, 0)
    return pl.pallas_call(
        _merge_body,
        grid=(t // tm,),
        in_specs=[pl.BlockSpec((tm, d), row)] + [pl.BlockSpec((tm, BR_WIDTH), row)] * N_BRANCH
        + [pl.BlockSpec((tm, N_BRANCH * d), row), _const_spec((N_BRANCH, BR_WIDTH, d)), _const_spec((d, d)),
           pl.BlockSpec((1, 1, d), mod_idx)],
        out_specs=pl.BlockSpec((tm, d), row),
        out_shape=jax.ShapeDtypeStruct((t, d), F32),
        compiler_params=_cparams("parallel"),
        name="merge",
    )(x2d, *branches, logits, wb, wo, gate)


def _rw_prep_body(z_ref, zp_ref, zn_ref, mu_ref, kkg_ref, ka_ref, rk_ref, w0_ref, w2_ref, a0_ref, a2_ref,
                  r_ref, kk_ref, v_ref, lw0_ref, k0_ref, b0_ref, lw1_ref, k1_ref, b1_ref, bonus_ref):
    cols = slice(BR_WIDTH, Z_RW_W)
    prev8, next8 = _halo_rows(zp_ref, zn_ref, cols)
    u = z_ref[0, :, cols].astype(F32)
    u = u + mu_ref[...] * (0.5 * (_from_prev(u, 1, prev8) + _from_next(u, next8)) - u)
    r = u[:, 0:BR_WIDTH]
    k = u[:, BR_WIDTH:2 * BR_WIDTH]
    v = u[:, 2 * BR_WIDTH:3 * BR_WIDTH]
    lo = u[:, 3 * BR_WIDTH:]
    lo = jnp.where(_iota2(lo.shape, 1) < RW_LORA, jnp.tanh(lo), lo)
    kk = k * kkg_ref[...]
    kk = kk * lax.rsqrt(_head_sum(kk * kk, RW_HEAD) + 1e-12)
    r_ref[0] = r.astype(r_ref.dtype)
    kk_ref[0] = kk.astype(kk_ref.dtype)
    v_ref[0] = v.astype(v_ref.dtype)
    coef = jnp.zeros_like(r)
    lo_hi = lo.astype(BF16)
    lo_lo = (lo - lo_hi.astype(F32)).astype(BF16)
    lora = lambda w_ref, d: (jnp.dot(lo_hi, w_ref[d, 0], preferred_element_type=F32)
                             + jnp.dot(lo_hi, w_ref[d, 1], preferred_element_type=F32)
                             + jnp.dot(lo_lo, w_ref[d, 0], preferred_element_type=F32))
    for d, (lw_ref, k_ref, b_ref) in enumerate(((lw0_ref, k0_ref, b0_ref), (lw1_ref, k1_ref, b1_ref))):
        w_raw = w0_ref[d] + lora(w2_ref, d)
        lw_ref[0] = -math.exp(-0.5) * _sigmoid(w_raw)
        a = _sigmoid(a0_ref[d] + lora(a2_ref, d))
        k_d = k * (1.0 + (a - 1.0) * ka_ref[...])
        k_ref[0] = k_d.astype(k_ref.dtype)
        b_ref[0] = (kk * a).astype(b_ref.dtype)
        coef = coef + _head_sum(r * k_d * rk_ref[...], RW_HEAD)
    bonus_ref[0] = coef * v


def rw_prep(z_rw, P, l, tq):
    b, n, c = z_rw.shape
    w2 = jnp.concatenate([P['rw_w2'][l], jnp.zeros_like(P['rw_w2'][l])], axis=1)
    a2 = jnp.concatenate([jnp.zeros_like(P['rw_a2'][l]), P['rw_a2'][l]], axis=1)
    row = lambda x: x.reshape(1, -1)
    tile = pl.BlockSpec((1, tq, BR_WIDTH), lambda bi, i: (bi, i, 0))
    out = lambda dt: jax.ShapeDtypeStruct((b, n, BR_WIDTH), dt)
    return pl.pallas_call(
        _rw_prep_body,
        grid=(b, n // tq),
        in_specs=_halo_specs(tq, n, c) + [_const_spec((1, RW_PRE))] + [_const_spec((1, BR_WIDTH))] * 3
        + [_const_spec((2, 1, BR_WIDTH)), _const_spec((2, 2, LANE, BR_WIDTH))] * 2,
        out_specs=[tile] * 10,
        out_shape=[out(BF16)] * 3 + [out(F32), out(BF16), out(BF16)] * 2 + [out(F32)],
        compiler_params=_cparams("parallel", "parallel"),
        name="rw_prep",
    )(z_rw, z_rw, z_rw, row(P['rw_mu'][l]), row(P['rw_kk'][l]), row(P['rw_ka'][l]), row(P['rw_rk'][l]),
      P['rw_w0'][l][:, None, :], _split_bf16(w2), P['rw_a0'][l][:, None, :], _split_bf16(a2))


def _split_bf16(w):
    hi = w.astype(BF16)
    return jnp.stack([hi, (w - hi.astype(F32)).astype(BF16)], axis=1)


def _rw_masks(rev):
    n = 2 * RW_CHUNK
    ri, ci = _iota2((n, n), 0), _iota2((n, n), 1)
    same = (ri // RW_CHUNK) == (ci // RW_CHUNK)
    strict = same & ((ci > ri) if rev else (ci < ri))
    incl = same & ((ci >= ri) if rev else (ci <= ri))
    blk8 = strict & ((ri // 8) == (ci // 8))
    levels = [strict & ((ri // (2 * s)) == (ci // (2 * s))) & ((ri // s) != (ci // s)) for s in (8, 16, 32)]
    eye = (ri == ci).astype(F32)
    t_i, s_i = _iota2((RW_CHUNK, RW_CHUNK), 0), _iota2((RW_CHUNK, RW_CHUNK), 1)
    tri = ((s_i >= t_i) if rev else (s_i <= t_i)).astype(F32)
    m0 = (_iota2((RW_CHUNK, LANE), 1) < RW_HEAD).astype(F32)
    return dict(strict=strict, incl=incl, blk8=blk8, levels=levels, eye=eye, tri=tri, m0=m0, m1=1.0 - m0)


def _rw_chunks(chains):
    n = 2 * RW_CHUNK
    each = lambda f, *cols: [f(*a) for a in zip(*cols)]
    revs, mks = [c[0] for c in chains], [c[1] for c in chains]
    lws, rs_, kks, vs_, ks_, bs_, ss = ([c[i] for c in chains] for i in range(2, 9))
    stack = lambda mk, x: jnp.concatenate([x * mk['m0'], x * mk['m1']], axis=0)
    cums = each(lambda mk, lw: _mm_01(mk['tri'], lw), mks, lws)
    totals = each(lambda rev, cum: cum[0:1] if rev else cum[RW_CHUNK - 1:RW_CHUNK], revs, cums)
    e_negs = each(lambda cum: jnp.exp(-cum), cums)
    e_tots = each(lambda cum, tot: jnp.exp(tot - cum), cums, totals)
    rs = each(lambda mk, r, cum: stack(mk, r * jnp.exp(cum)), mks, rs_, cums)
    ks = each(lambda mk, kk, cum, lw: stack(mk, kk * jnp.exp(cum - lw)), mks, kks, cums, lws)
    vs = each(stack, mks, vs_)
    both = lambda mk, b, k, e: jnp.concatenate([stack(mk, b * e), stack(mk, k * e)], axis=0)
    rhs = each(both, mks, bs_, ks_, e_negs)
    hat = each(both, mks, bs_, ks_, e_tots)
    g1 = each(_mm_nt, ks, rhs)
    g2 = each(_mm_nt, rs, rhs)
    a_b = each(lambda mk, g: jnp.where(mk['strict'], g[:, :n], 0.0), mks, g1)
    a_k = each(lambda mk, g: jnp.where(mk['strict'], g[:, n:], 0.0), mks, g1)
    m_b = each(lambda mk, g: jnp.where(mk['incl'], g[:, :n], 0.0), mks, g2)
    m_k = each(lambda mk, g: jnp.where(mk['incl'], g[:, n:], 0.0), mks, g2)
    x = each(lambda mk, a: -jnp.where(mk['blk8'], a, 0.0), mks, a_b)
    x2 = each(_mm, x, x)
    akv = each(_mm, a_k, vs)
    t_inv = each(lambda mk, x: mk['eye'] + x, mks, x)
    x4 = each(_mm, x2, x2)
    t_inv = each(lambda t, x2: t + _mm(t, x2), t_inv, x2)
    t_inv = each(lambda t, x4: t + _mm(t, x4), t_inv, x4)
    for li in range(3):
        tl = each(lambda mk, t, a: _mm(t, jnp.where(mk['levels'][li], a, 0.0)), mks, t_inv, a_b)
        t_inv = each(lambda t, tl: t - _mm(tl, t), t_inv, tl)
    zr = each(lambda ks, rs, s: _mm_nt(jnp.concatenate([ks, rs], axis=0), s), ks, rs, ss)
    u = each(lambda t, zr, akv: -_mm(t, zr[:n] + akv), t_inv, zr, akv)
    s_new = each(lambda s, tot, u, vs, hat: s * jnp.exp(tot) + _mm(jnp.concatenate([u, vs], axis=0).T, hat),
                 ss, totals, u, vs, hat)
    o_s = each(lambda zr, m_b, m_k, u, vs: zr[n:] + _mm(jnp.concatenate([m_b, m_k], axis=1), jnp.concatenate([u, vs], axis=0)),
               zr, m_b, m_k, u, vs)
    return each(lambda o_s: o_s[:RW_CHUNK] + o_s[RW_CHUNK:], o_s), s_new


def _rw_scan_body(rf_ref, kkf_ref, vf_ref, lwf_ref, kf_ref, bf_ref, rb_ref, kkb_ref, vb_ref, lwb_ref, kb_ref, bb_ref,
                  s0_ref, of_ref, ob_ref, sfin_ref, s_scr):
    c = pl.program_id(1)

    @pl.when(c == 0)
    def _():
        s_scr[...] = s0_ref[0]

    masks = (_rw_masks(rev=False), _rw_masks(rev=True))
    dirs = ((0, (lwf_ref, rf_ref, kkf_ref, vf_ref, kf_ref, bf_ref), of_ref),
            (1, (lwb_ref, rb_ref, kkb_ref, vb_ref, kb_ref, bb_ref), ob_ref))
    states = {(d, p): s_scr[d, p] for d in range(2) for p in range(PAIRS)}
    for j in range(RW_STEP_CHUNKS):
        chains, dests = [], []
        for d, refs, o_ref in dirs:
            sub = RW_STEP_CHUNKS - 1 - j if d == 1 else j
            rows = slice(sub * RW_CHUNK, (sub + 1) * RW_CHUNK)
            for p in range(PAIRS):
                sl = slice(p * LANE, (p + 1) * LANE)
                chains.append((d == 1, masks[d]) + tuple(ref[0, rows, sl].astype(F32) for ref in refs)
                              + (states[(d, p)],))
                dests.append((o_ref, rows, sl, d, p))
        outs, new_states = _rw_chunks(chains)
        for (o_ref, rows, sl, d, p), o, s_new in zip(dests, outs, new_states):
            o_ref[0, rows, sl] = o.astype(o_ref.dtype)
            states[(d, p)] = s_new
    for (d, p), s_new in states.items():
        s_scr[d, p] = s_new

    @pl.when(c == pl.num_programs(1) - 1)
    def _():
        sfin_ref[0] = s_scr[...]


def rw_scan(prep, s0_pairs):
    r, kk, v, lw0, k0, b0, lw1, k1, b1, _ = prep
    b, n, _ = r.shape
    rows = RW_CHUNK * RW_STEP_CHUNKS
    nc = n // rows
    fwd = pl.BlockSpec((1, rows, BR_WIDTH), lambda bi, c: (bi, c, 0))
    bwd = pl.BlockSpec((1, rows, BR_WIDTH), lambda bi, c: (bi, nc - 1 - c, 0))
    st = pl.BlockSpec((1, 2, PAIRS, LANE, LANE), lambda bi, c: (bi, 0, 0, 0, 0))
    return pl.pallas_call(
        _rw_scan_body,
        grid=(b, nc),
        in_specs=[fwd] * 6 + [bwd] * 6 + [st],
        out_specs=[fwd, bwd, st],
        out_shape=[jax.ShapeDtypeStruct((b, n, BR_WIDTH), BF16)] * 2
        + [jax.ShapeDtypeStruct((b, 2, PAIRS, LANE, LANE), F32)],
        scratch_shapes=[pltpu.VMEM((2, PAIRS, LANE, LANE), F32)],
        compiler_params=_cparams("parallel", "arbitrary"),
        name="rw_scan",
    )(r, kk, v, lw0, k0, b0, r, kk, v, lw1, k1, b1, s0_pairs)


def rw_state_to_pairs(s0):
    b = s0.shape[0]
    s = s0.reshape(b, 2, PAIRS, 2, RW_HEAD, RW_HEAD)
    z = jnp.zeros_like(s[:, :, :, 0])
    top = jnp.concatenate([s[:, :, :, 0], z], axis=-1)
    bot = jnp.concatenate([z, s[:, :, :, 1]], axis=-1)
    return jnp.concatenate([top, bot], axis=-2)


def rw_pairs_to_state(sp):
    b = sp.shape[0]
    h0 = sp[:, :, :, :RW_HEAD, :RW_HEAD]
    h1 = sp[:, :, :, RW_HEAD:, RW_HEAD:]
    return jnp.stack([h0, h1], axis=3).reshape(b, 2, RW_HEADS, RW_HEAD, RW_HEAD)


def _rw_post_body(of_ref, ob_ref, bonus_ref, gate_ref, g_ref, bias_ref, o_ref):
    wkv = of_ref[...].astype(F32) + ob_ref[...].astype(F32)
    mu = _head_sum(wkv, RW_HEAD) * (1.0 / RW_HEAD)
    xc = wkv - mu
    var = _head_sum(xc * xc, RW_HEAD) * (1.0 / RW_HEAD)
    o = xc * lax.rsqrt(var + RW_GN_EPS) * g_ref[...] + bias_ref[...] + bonus_ref[...]
    o_ref[...] = (o * _silu(gate_ref[...].astype(F32))).astype(o_ref.dtype)


def rw_post(o_f, o_b, bonus, z_rw2d, ln_g, ln_b, tm):
    t = o_f.shape[0]
    tile = pl.BlockSpec((tm, BR_WIDTH), lambda i: (i, 0))
    return pl.pallas_call(
        _rw_post_body,
        grid=(t // tm,),
        in_specs=[tile] * 4 + [_const_spec((1, BR_WIDTH))] * 2,
        out_specs=tile,
        out_shape=jax.ShapeDtypeStruct((t, BR_WIDTH), BF16),
        compiler_params=_cparams("parallel"),
        name="rw_post",
    )(o_f, o_b, bonus, z_rw2d, ln_g.reshape(1, -1), ln_b.reshape(1, -1))


def _ssd_prep_body(z_ref, zp_ref, zn_ref, w_ref, b_ref, dtb_ref, nega_ref, xbc_ref, dtp_ref):
    cols = slice(BR_WIDTH, BR_WIDTH + SSD_CONV_DIM)
    prev8, next8 = _halo_rows(zp_ref, zn_ref, cols)
    xbc_ref[0] = _silu(_centred_conv(z_ref[0, :, cols], prev8, next8, w_ref, b_ref)).astype(xbc_ref.dtype)
    dt = _softplus(z_ref[0, :, BR_WIDTH + SSD_CONV_DIM:] + dtb_ref[...])
    dta = pltpu.roll(dt, 2 * SSD_HEADS, 1) * nega_ref[...]
    lane = _iota2(dt.shape, 1)
    dtp_ref[0] = jnp.where(lane < 2 * SSD_HEADS, dt, jnp.where(lane < 4 * SSD_HEADS, dta, 0.0))


def ssd_prep(z_ssd, conv_w, conv_b, dt_bias, a_log, tq):
    b, n, c = z_ssd.shape
    dtb = jnp.pad(dt_bias.reshape(1, -1), ((0, 0), (0, LANE - 2 * SSD_HEADS)))
    nega = jnp.pad(-jnp.exp(a_log).reshape(1, -1), ((0, 0), (2 * SSD_HEADS, LANE - 4 * SSD_HEADS)))
    return pl.pallas_call(
        _ssd_prep_body,
        grid=(b, n // tq),
        in_specs=_halo_specs(tq, n, c) + [_const_spec((CONV_K, SSD_CONV_DIM)), _const_spec((1, SSD_CONV_DIM)),
                                           _const_spec((1, LANE)), _const_spec((1, LANE))],
        out_specs=[pl.BlockSpec((1, tq, SSD_CONV_DIM), lambda bi, i: (bi, i, 0)),
                   pl.BlockSpec((1, tq, LANE), lambda bi, i: (bi, i, 0))],
        out_shape=[jax.ShapeDtypeStruct((b, n, SSD_CONV_DIM), BF16), jax.ShapeDtypeStruct((b, n, LANE), F32)],
        compiler_params=_cparams("parallel", "parallel"),
        name="ssd_prep",
    )(z_ssd, z_ssd, z_ssd, conv_w, conv_b.reshape(1, -1), dtb, nega)


def _ssd_chunk(rev, d, xbc, dtp, st_ref):
    q = SSD_Q
    li, si = _iota2((q, q), 0), _iota2((q, q), 1)
    incl = (si >= li) if rev else (si <= li)
    tri = incl.astype(F32)
    head0 = _iota2((q, LANE), 1) < SSD_HEAD_DIM
    m0 = head0.astype(F32)
    m1 = 1.0 - m0
    dt_lane, a_lane = d * SSD_HEADS, (2 + d) * SSD_HEADS

    def per_head_lanes(arr, lane0):
        cols = [arr[:, lane0 + h:lane0 + h + 1] for h in range(SSD_HEADS)]
        return jnp.concatenate([jnp.where(head0, cols[2 * p], cols[2 * p + 1]) for p in range(PAIRS)], axis=1)

    cum = _mm_01(tri, dtp)
    yield
    acs_rows = cum.T[a_lane:a_lane + SSD_HEADS, :]
    acs = per_head_lanes(cum, a_lane)
    a_tot = acs[0:1] if rev else acs[q - 1:q]
    x, bm, cm = xbc[:, :BR_WIDTH], xbc[:, BR_WIDTH:BR_WIDTH + LANE], xbc[:, BR_WIDTH + LANE:]
    xd = x * per_head_lanes(dtp, dt_lane)
    e_acs = jnp.exp(acs)
    xdec = xd * jnp.exp(a_tot - acs)
    e_tot = jnp.exp(a_tot)
    yield
    ys = []
    for g in range(SSD_GROUPS):
        mg = m0 if g == 0 else m1
        cb = _mm_nt(cm * mg, bm)
        bm_t = (bm * mg).T
        yield
        for p in range(2 * g, 2 * g + 2):
            sl = slice(p * LANE, (p + 1) * LANE)
            ms = []
            for hh in range(2):
                h = 2 * p + hh
                col = cum[:, a_lane + h:a_lane + h + 1]
                ms.append(jnp.where(incl, cb * jnp.exp(jnp.minimum(col - acs_rows[h:h + 1, :], 0.0)), 0.0))
            xs = xd[:, sl]
            yield
            y = _mm(jnp.concatenate(ms, axis=1), jnp.concatenate([xs * m0, xs * m1], axis=0))
            y = y + _mm(cm, st_ref[p]) * e_acs[:, sl]
            st_ref[p] = st_ref[p] * e_tot[:, sl] + _mm(bm_t, xdec[:, sl])
            ys.append(y)
            yield
    return jnp.concatenate(ys, axis=1)


def _ssd_scan_body(xf_ref, dtf_ref, xb_ref, dtb_ref, s0_ref, yf_ref, yb_ref, sfin_ref, st_scr):
    c = pl.program_id(1)

    @pl.when(c == 0)
    def _():
        for d in range(2):
            for p in range(PAIRS):
                st_scr[d, p] = s0_ref[0, d, p].T

    y_f, y_b = _in_lockstep(_ssd_chunk(False, 0, xf_ref[0].astype(F32), dtf_ref[0], st_scr.at[0]),
                            _ssd_chunk(True, 1, xb_ref[0].astype(F32), dtb_ref[0], st_scr.at[1]))
    yf_ref[0] = y_f.astype(yf_ref.dtype)
    yb_ref[0] = y_b.astype(yb_ref.dtype)

    @pl.when(c == pl.num_programs(1) - 1)
    def _():
        for d in range(2):
            for p in range(PAIRS):
                sfin_ref[0, d, p] = st_scr[d, p].T


def ssd_scan(xbc, dtp, s0_pairs):
    b, n, _ = xbc.shape
    nc = n // SSD_Q
    fw = lambda w: pl.BlockSpec((1, SSD_Q, w), lambda bi, c: (bi, c, 0))
    bw = lambda w: pl.BlockSpec((1, SSD_Q, w), lambda bi, c: (bi, nc - 1 - c, 0))
    st = pl.BlockSpec((1, 2, PAIRS, LANE, LANE), lambda bi, c: (bi, 0, 0, 0, 0))
    return pl.pallas_call(
        _ssd_scan_body,
        grid=(b, nc),
        in_specs=[fw(SSD_CONV_DIM), fw(LANE), bw(SSD_CONV_DIM), bw(LANE), st],
        out_specs=[fw(BR_WIDTH), bw(BR_WIDTH), st],
        out_shape=[jax.ShapeDtypeStruct((b, n, BR_WIDTH), BF16)] * 2
        + [jax.ShapeDtypeStruct((b, 2, PAIRS, LANE, LANE), F32)],
        scratch_shapes=[pltpu.VMEM((2, PAIRS, LANE, LANE), F32)],
        compiler_params=_cparams("parallel", "arbitrary"),
        name="ssd_scan",
    )(xbc, dtp, xbc, dtp, s0_pairs)


def ssd_state_to_pairs(h0):
    b = h0.shape[0]
    s = h0.reshape(b, 2, PAIRS, LANE, SSD_STATE)
    lo = jnp.pad(s, ((0, 0),) * 4 + ((0, SSD_STATE),))
    hi = jnp.pad(s, ((0, 0),) * 4 + ((SSD_STATE, 0),))
    return jnp.where(_pair_in_group0(), lo, hi)


def _pair_in_group0():
    return (jnp.arange(PAIRS) < PAIRS // SSD_GROUPS)[None, None, :, None, None]


def ssd_pairs_to_state(sp):
    b = sp.shape[0]
    s = jnp.where(_pair_in_group0(), sp[..., :SSD_STATE], sp[..., SSD_STATE:])
    return s.reshape(b, 2, SSD_HEADS, SSD_HEAD_DIM, SSD_STATE)


def _ssd_post_body(yf_ref, yb_ref, x_ref, gate_ref, d_ref, g_ref, o_ref):
    y = (x_ref[...].astype(F32) * d_ref[...] + yf_ref[...].astype(F32) + yb_ref[...].astype(F32)) * _silu(gate_ref[...].astype(F32))
    y = y * lax.rsqrt(jnp.mean(y * y, axis=-1, keepdims=True) + EPS)
    o_ref[...] = (y * g_ref[...]).astype(o_ref.dtype)


def ssd_post(y_f, y_b, xbc2d, z_ssd2d, d_full, norm_g, tm):
    t = y_f.shape[0]
    tile = pl.BlockSpec((tm, BR_WIDTH), lambda i: (i, 0))
    return pl.pallas_call(
        _ssd_post_body,
        grid=(t // tm,),
        in_specs=[tile] * 4 + [_const_spec((1, BR_WIDTH))] * 2,
        out_specs=tile,
        out_shape=jax.ShapeDtypeStruct((t, BR_WIDTH), BF16),
        compiler_params=_cparams("parallel"),
        name="ssd_post",
    )(y_f, y_b, xbc2d, z_ssd2d, d_full.reshape(1, -1), norm_g.reshape(1, -1))


def _lru_prep_body(z_ref, zp_ref, zn_ref, w_ref, b_ref, wa_ref, ba_ref, wx_ref, bx_ref, sp_ref,
                   a0_ref, u0_ref, a1_ref, u1_ref):
    cols = slice(0, LRU_WIDTH)
    prev8, next8 = _halo_rows(zp_ref, zn_ref, cols)
    xc = _centred_conv(z_ref[0, :, cols].astype(F32), prev8, next8, w_ref, b_ref)
    for d, (a_ref, u_ref) in enumerate(((a0_ref, u0_ref), (a1_ref, u1_ref))):
        g_r = _sigmoid(_mm(xc, wa_ref[d]) + ba_ref[d])
        g_i = _sigmoid(_mm(xc, wx_ref[d]) + bx_ref[d])
        log_a = -LRU_C * g_r * sp_ref[d]
        t = jnp.tanh(log_a)
        a_ref[0] = jnp.exp(log_a)
        u_ref[0] = (jnp.sqrt(-2.0 * t / (1.0 - t)) * (g_i * xc)).astype(u_ref.dtype)


def _block_diag(w):
    d, k, bi, bo = w.shape
    return jnp.einsum('dkij,kl->dkilj', w, jnp.eye(k, dtype=w.dtype)).reshape(d, k * bi, k * bo)


def lru_prep(z_lru, conv_w, conv_b, wa, ba, wx, bx, lam, tq):
    b, n, c = z_lru.shape
    tile = pl.BlockSpec((1, tq, LRU_WIDTH), lambda bi, i: (bi, i, 0))
    out = jax.ShapeDtypeStruct((b, n, LRU_WIDTH), F32)
    sp = _softplus(-lam)[:, None, :]
    return pl.pallas_call(
        _lru_prep_body,
        grid=(b, n // tq),
        in_specs=_halo_specs(tq, n, c) + [_const_spec((CONV_K, LRU_WIDTH)), _const_spec((1, LRU_WIDTH)),
                                           _const_spec((2, LRU_WIDTH, LRU_WIDTH)), _const_spec((2, 1, LRU_WIDTH)),
                                           _const_spec((2, LRU_WIDTH, LRU_WIDTH)), _const_spec((2, 1, LRU_WIDTH)),
                                           _const_spec((2, 1, LRU_WIDTH))],
        out_specs=[tile] * 4,
        out_shape=[out, jax.ShapeDtypeStruct((b, n, LRU_WIDTH), BF16)] * 2,
        compiler_params=_cparams("parallel", "parallel"),
        name="lru_prep",
    )(z_lru, z_lru, z_lru, conv_w, conv_b.reshape(1, -1), _block_diag(wa).astype(BF16), ba[:, None, :],
      _block_diag(wx).astype(BF16), bx[:, None, :], sp)


def _lru_tile_scan(rev, a, u, carry):
    n = a.shape[0]
    sub = _iota2(a.shape, 0) & (SUBLANE - 1)
    s = 1
    while s < SUBLANE:
        shift = n - s if rev else s
        keep = (sub < SUBLANE - s) if rev else (sub >= s)
        a_sh = jnp.where(keep, pltpu.roll(a, shift, 0), 1.0)
        u_sh = jnp.where(keep, pltpu.roll(u, shift, 0), 0.0)
        u = u + a * u_sh
        a = a * a_sh
        s *= 2
        yield
    groups = range(n // SUBLANE)
    hs = [None] * len(groups)
    for g in (reversed(groups) if rev else groups):
        rows = slice(g * SUBLANE, (g + 1) * SUBLANE)
        hs[g] = a[rows] * carry + u[rows]
        carry = hs[g][0:1] if rev else hs[g][SUBLANE - 1:SUBLANE]
        yield
    return jnp.concatenate(hs, axis=0), carry


def _lru_scan_body(af_ref, uf_ref, ab_ref, ub_ref, h0_ref, yf_ref, yb_ref, hfin_ref, h_scr):
    c = pl.program_id(1)

    @pl.when(c == 0)
    def _():
        h_scr[...] = h0_ref[0]

    (hf, cf), (hb, cb) = _in_lockstep(_lru_tile_scan(False, af_ref[0], uf_ref[0].astype(F32), h_scr[0:1]),
                                      _lru_tile_scan(True, ab_ref[0], ub_ref[0].astype(F32), h_scr[1:2]))
    yf_ref[0] = hf.astype(yf_ref.dtype)
    h_scr[0:1] = cf
    yb_ref[0] = hb.astype(yb_ref.dtype)
    h_scr[1:2] = cb

    @pl.when(c == pl.num_programs(1) - 1)
    def _():
        hfin_ref[0] = h_scr[...]


def lru_scan(a0, u0, a1, u1, h0, tq):
    b, n, _ = a0.shape
    nc = n // tq
    fw = pl.BlockSpec((1, tq, LRU_WIDTH), lambda bi, c: (bi, c, 0))
    bw = pl.BlockSpec((1, tq, LRU_WIDTH), lambda bi, c: (bi, nc - 1 - c, 0))
    st = pl.BlockSpec((1, 2, LRU_WIDTH), lambda bi, c: (bi, 0, 0))
    return pl.pallas_call(
        _lru_scan_body,
        grid=(b, nc),
        in_specs=[fw, fw, bw, bw, st],
        out_specs=[fw, bw, st],
        out_shape=[jax.ShapeDtypeStruct((b, n, LRU_WIDTH), BF16)] * 2 + [jax.ShapeDtypeStruct((b, 2, LRU_WIDTH), F32)],
        scratch_shapes=[pltpu.VMEM((2, LRU_WIDTH), F32)],
        compiler_params=_cparams("parallel", "arbitrary"),
        name="lru_scan",
    )(a0, u0, a1, u1, h0)


def _lru_post_body(yf_ref, yb_ref, gate_ref, o_ref):
    o_ref[...] = ((yf_ref[...].astype(F32) + yb_ref[...].astype(F32)) * _silu(gate_ref[...].astype(F32))).astype(o_ref.dtype)


def lru_post(y_f, y_b, z_lru2d, tm):
    t = y_f.shape[0]
    tile = pl.BlockSpec((tm, LRU_WIDTH), lambda i: (i, 0))
    return pl.pallas_call(
        _lru_post_body,
        grid=(t // tm,),
        in_specs=[tile, tile, pl.BlockSpec((tm, LRU_WIDTH), lambda i: (i, 1))],
        out_specs=tile,
        out_shape=jax.ShapeDtypeStruct((t, LRU_WIDTH), BF16),
        compiler_params=_cparams("parallel"),
        name="lru_post",
    )(y_f, y_b, z_lru2d)


def rope_tables(n):
    t = np.arange(n)
    pos = jnp.asarray(np.stack([t // GRID_W, t % GRID_W], axis=0).astype(np.float32))
    nf = MLA_ROPE // 4
    inv = jnp.asarray(ROPE_THETA, F32) ** (-jnp.arange(nf, dtype=F32) / nf)
    rel = np.arange(LANE) - MLA_NOPE
    in_rope = (rel >= 0) & (rel < MLA_ROPE)
    half = np.clip(rel // (2 * nf), 0, 1)
    within = rel % (2 * nf)
    is_x2 = within >= nf
    ang = pos[half, :].T * inv[within % nf][None, :]
    cos = jnp.where(in_rope[None, :], jnp.cos(ang), 1.0)
    sin = jnp.sin(ang)
    sin_a = jnp.where((in_rope & ~is_x2)[None, :], -sin, 0.0)
    sin_b = jnp.where((in_rope & is_x2)[None, :], sin, 0.0)
    return cos, sin_a, sin_b


def _head_norm_rope(x, g, rope):
    nf = MLA_ROPE // 4
    outs = []
    for h in range(MLA_HEADS):
        xh = x[:, h * LANE:(h + 1) * LANE]
        ms = jnp.sum(xh * xh, axis=-1, keepdims=True) * (1.0 / MLA_QK)
        xh = xh * lax.rsqrt(ms + EPS) * g
        if rope is not None:
            cos, sin_a, sin_b = rope
            xh = xh * cos + pltpu.roll(xh, LANE - nf, 1) * sin_a + pltpu.roll(xh, nf, 1) * sin_b
        outs.append(xh)
    return jnp.concatenate(outs, axis=1)


def _mla_q_body(use_rope, q_ref, qag_ref, wq_ref, qng_ref, *rest):
    *rope_refs, o_ref = rest
    x = q_ref[:, :MLA_Q_RANK].astype(F32)
    x = x * lax.rsqrt(jnp.mean(x * x, axis=-1, keepdims=True) + EPS) * qag_ref[...]
    q = _mm(x, wq_ref[...])
    rope = tuple(r[...] for r in rope_refs) if use_rope else None
    q = _head_norm_rope(q, qng_ref[...], rope)
    o_ref[...] = (q * (MLA_QK ** -0.5)).astype(o_ref.dtype)


def mla_q(z_mla, qa_g, wq_pad, qn_g_pad, rope, n, tm):
    t = z_mla.shape[0]
    per = n // tm
    in_specs = [pl.BlockSpec((tm, 512), lambda i: (i, MLA_Q_BLOCK)), _const_spec((1, MLA_Q_RANK)),
                _const_spec((MLA_Q_RANK, MLA_W)), _const_spec((1, LANE))]
    args = [z_mla, qa_g.reshape(1, -1), wq_pad, qn_g_pad]
    if rope is not None:
        in_specs += [pl.BlockSpec((tm, LANE), lambda i: (i % per, 0))] * 3
        args += list(rope)
    return pl.pallas_call(
        lambda *refs: _mla_q_body(rope is not None, *refs),
        grid=(t // tm,),
        in_specs=in_specs,
        out_specs=pl.BlockSpec((tm, MLA_W), lambda i: (i, 0)),
        out_shape=jax.ShapeDtypeStruct((t, MLA_W), BF16),
        compiler_params=_cparams("parallel"),
        name="mla_q",
    )(*args)


def _mla_kv_body(norm_ckv, use_rope, c_ref, kr_ref, kvag_ref, wk_ref, wv_ref, kng_ref, *rest):
    *rope_refs, cn_ref, k_ref, v_ref = rest
    c = c_ref[...].astype(F32)
    if norm_ckv:
        c = c * lax.rsqrt(jnp.mean(c * c, axis=-1, keepdims=True) + EPS) * kvag_ref[...]
    cn_ref[...] = c
    k = _mm(c, wk_ref[...]) + kr_ref[...].astype(F32)
    rope = tuple(r[...] for r in rope_refs) if use_rope else None
    k_ref[...] = _head_norm_rope(k, kng_ref[...], rope).astype(k_ref.dtype)
    v_ref[...] = _mm(c, wv_ref[...]).astype(v_ref.dtype)


def mla_kv(c_arr, c_block, kr_arr, kr_block, kva_g, wk_pad, wv, kn_g_pad, rope, norm_ckv, n, tm):
    t = c_arr.shape[0]
    per = n // tm
    in_specs = [pl.BlockSpec((tm, MLA_KV_RANK), lambda i: (i, c_block)),
                pl.BlockSpec((tm, MLA_W), lambda i: (i, kr_block)),
                _const_spec((1, MLA_KV_RANK)), _const_spec((MLA_KV_RANK, MLA_W)), _const_spec((MLA_KV_RANK, BR_WIDTH)),
                _const_spec((1, LANE))]
    args = [c_arr, kr_arr, kva_g.reshape(1, -1), wk_pad, wv, kn_g_pad]
    if rope is not None:
        in_specs += [pl.BlockSpec((tm, LANE), lambda i: (i % per, 0))] * 3
        args += list(rope)
    return pl.pallas_call(
        lambda *refs: _mla_kv_body(norm_ckv, rope is not None, *refs),
        grid=(t // tm,),
        in_specs=in_specs,
        out_specs=[pl.BlockSpec((tm, MLA_KV_RANK), lambda i: (i, 0)), pl.BlockSpec((tm, MLA_W), lambda i: (i, 0)),
                   pl.BlockSpec((tm, BR_WIDTH), lambda i: (i, 0))],
        out_shape=[jax.ShapeDtypeStruct((t, MLA_KV_RANK), F32), jax.ShapeDtypeStruct((t, MLA_W), BF16),
                   jax.ShapeDtypeStruct((t, BR_WIDTH), BF16)],
        compiler_params=_cparams("parallel"),
        name="mla_kv",
    )(*args)


def _attn_body(n_src, q_ref, gate_ref, *rest):
    kv_refs = rest[:2 * n_src]
    o_ref = rest[2 * n_src]
    scr = rest[2 * n_src + 1:]
    tq = q_ref.shape[0]
    part = tq // ATTN_SPLIT
    offs = [0]
    for s_i in range(n_src):
        offs.append(offs[-1] + kv_refs[2 * s_i].shape[1])
    tiles = [(h, r) for h in range(2) for r in range(ATTN_SPLIT)]
    out = {}

    def scores(t):
        h, r = tiles[t]
        s_scr = scr[3 * t]
        for s_i in range(n_src):
            s_scr[:, offs[s_i]:offs[s_i + 1]] = lax.dot_general(
                q_ref[r * part:(r + 1) * part, h * LANE:(h + 1) * LANE], kv_refs[2 * s_i][0, :, h * LANE:(h + 1) * LANE],
                (((1,), (1,)), ((), ())), preferred_element_type=F32)

    def softmax(t):
        s_scr, p_scr, l_scr = scr[3 * t:3 * t + 3]
        for i in range(part // ATTN_ROWS):
            rows = slice(i * ATTN_ROWS, (i + 1) * ATTN_ROWS)
            m = jnp.max(s_scr[rows, :], axis=-1, keepdims=True)
            l = jnp.zeros((ATTN_ROWS, 1), F32)
            for c0 in range(0, offs[-1], ATTN_COLS):
                cols = slice(c0, min(c0 + ATTN_COLS, offs[-1]))
                p = jnp.exp(s_scr[rows, cols] - m)
                p_scr[rows, cols] = p.astype(BF16)
                l = l + jnp.sum(p, axis=-1, keepdims=True)
            l_scr[rows, :] = jnp.broadcast_to(l, (ATTN_ROWS, LANE))

    def values(t):
        _, p_scr, l_scr = scr[3 * t:3 * t + 3]
        acc = None
        for s_i in range(n_src):
            pv = jnp.dot(p_scr[:, offs[s_i]:offs[s_i + 1]], kv_refs[2 * s_i + 1][0], preferred_element_type=F32)
            acc = pv if acc is None else acc + pv
        out[tiles[t]] = acc / l_scr[...]

    nt = len(tiles)
    scores(0)
    for t in range(nt):
        if t + 1 < nt:
            scores(t + 1)
        softmax(t)
        if t >= 1:
            values(t - 1)
    values(nt - 1)
    accs = [jnp.concatenate([out[(h, r)] for r in range(ATTN_SPLIT)], axis=0) for h in range(2)]
    o = jnp.where(_iota2((tq, LANE), 1) < MLA_V, accs[0], accs[1])
    o_ref[...] = (o * _silu(gate_ref[...].astype(F32))).astype(o_ref.dtype)


def mla_attention(q, z_mla, kvs, n, tq):
    t = q.shape[0]
    b = t // n
    per = n // tq
    in_specs = [pl.BlockSpec((tq, 2 * LANE), lambda bi, p, i: (bi * per + i, p)),
                pl.BlockSpec((tq, LANE), lambda bi, p, i: (bi * per + i, MLA_GATE_LANE_BLOCK + p))]
    args = [q, z_mla]
    for k, v in kvs:
        nk = k.shape[1]
        in_specs += [pl.BlockSpec((1, nk, 2 * LANE), lambda bi, p, i: (bi, 0, p)),
                     pl.BlockSpec((1, nk, LANE), lambda bi, p, i: (bi, 0, p))]
        args += [k, v]
    n_keys = sum(k.shape[1] for k, _ in kvs)
    return pl.pallas_call(
        lambda *refs: _attn_body(len(kvs), *refs),
        grid=(b, PAIRS, per),
        in_specs=in_specs,
        out_specs=pl.BlockSpec((tq, LANE), lambda bi, p, i: (bi * per + i, p)),
        out_shape=jax.ShapeDtypeStruct((t, BR_WIDTH), BF16),
        scratch_shapes=[pltpu.VMEM((tq // ATTN_SPLIT, n_keys), F32), pltpu.VMEM((tq // ATTN_SPLIT, n_keys), BF16),
                        pltpu.VMEM((tq // ATTN_SPLIT, LANE), F32)] * (2 * ATTN_SPLIT),
        compiler_params=_cparams("parallel", "parallel", "arbitrary"),
        name="mla_attention",
    )(*args)


def pad_heads(w, real, lead_zero=0):
    shp = w.shape[:-1]
    w = w.reshape(*shp, MLA_HEADS, real)
    w = jnp.pad(w, [(0, 0)] * len(shp) + [(0, 0), (lead_zero, LANE - real - lead_zero)])
    return w.reshape(*shp, MLA_W)


def pad_lane(g):
    return jnp.pad(g.reshape(1, -1), ((0, 0), (0, LANE - g.size)))


def place_rope_key(k_rope):
    return pad_heads(jnp.tile(k_rope, (1,) * (k_rope.ndim - 1) + (MLA_HEADS,)), MLA_ROPE, MLA_NOPE)


def pack_weights(P, l):
    w = P['w_in'][l]
    col = lambda i: w[:, IN_OFFS[i]:IN_OFFS[i + 1]]
    zeros = lambda n: jnp.zeros((D_MODEL, n), w.dtype)
    mla_kv = col(3)
    w_rw = jnp.concatenate([col(1), col(0)], axis=1)
    w_mla = jnp.concatenate([place_rope_key(mla_kv[:, MLA_KV_RANK:]), col(4), col(2), zeros(512 - MLA_Q_RANK),
                             mla_kv[:, :MLA_KV_RANK]], axis=1)
    w_ssd = jnp.concatenate([col(5), col(6), col(7), zeros(LANE - 2 * SSD_HEADS)], axis=1)
    w_lru = jnp.concatenate([col(8), col(9)], axis=1)
    kvu = P['mla_kv_up'][l].reshape(MLA_KV_RANK, MLA_HEADS, MLA_NOPE + MLA_V)
    bf = lambda x: x.astype(BF16)
    return dict(w_rw=bf(w_rw), w_mla=bf(w_mla), w_ssd=bf(w_ssd), w_lru=bf(w_lru), w_mrg=bf(0.5 * col(10)),
                wq=bf(pad_heads(P['mla_q_up'][l], MLA_QK)),
                wk=bf(pad_heads(kvu[:, :, :MLA_NOPE].reshape(MLA_KV_RANK, -1), MLA_NOPE)),
                wv=bf(kvu[:, :, MLA_NOPE:].reshape(MLA_KV_RANK, -1)),
                w_branch=bf(P['w_branch'][l]), w_out=bf(P['w_out'][l]))


def trunk_layer(x, mod, P, W, l, ctx):
    b, n, d = x.shape
    t = b * n
    tm = min(n, 512)
    tq = 256
    shift, scale, gate = mod
    x2d = x.reshape(t, d)
    h = prenorm(x2d, P['norm_g'][l], scale, shift, n, tm)
    z_rw = matmul(h, W['w_rw'], tm, Z_RW_W, BF16)
    z_mla = matmul(h, W['w_mla'], tm, Z_MLA_W, BF16)
    z_ssd = matmul(h, W['w_ssd'], tm, Z_SSD_W, F32)
    z_lru = matmul(h, W['w_lru'], tm, Z_LRU_W, BF16)
    logits = matmul(h, W['w_mrg'], tm, 2048, BF16)
    if ctx is None:
        rw0 = jnp.zeros((b, 2, PAIRS, LANE, LANE), F32)
        ssd0 = jnp.zeros((b, 2, PAIRS, LANE, LANE), F32)
        lru0 = jnp.zeros((b, 2, LRU_WIDTH), F32)
        rope = None
    else:
        ckv0, krope0, rw0, ssd0, lru0 = ctx
        rw0, ssd0 = rw_state_to_pairs(rw0), ssd_state_to_pairs(ssd0)
        rope = rope_tables(n)

    prep = rw_prep(z_rw.reshape(b, n, -1), P, l, tq)
    o_f, o_b, rw_fin = rw_scan(prep, rw0)
    flat = lambda a: a.reshape(t, -1)
    br_rw = rw_post(flat(o_f), flat(o_b), flat(prep[9]), z_rw, P['rw_ln_g'][l], P['rw_ln_b'][l], tm)

    qn_g, kn_g = pad_lane(P['mla_qn_g'][l]), pad_lane(P['mla_kn_g'][l])
    q = mla_q(z_mla, P['mla_qa_g'][l], W['wq'], qn_g, rope, n, tm)
    c_kv, k, v = mla_kv(z_mla, MLA_CKV_BLOCK, z_mla, 0, P['mla_kva_g'][l], W['wk'], W['wv'], kn_g, rope, True, n, tm)
    kvs = [(k.reshape(b, n, -1), v.reshape(b, n, -1))]
    if ctx is not None:
        past = ckv0.shape[1]
        _, k0, v0 = mla_kv(ckv0.reshape(b * past, -1), 0, place_rope_key(krope0).reshape(b * past, -1), 0,
                           P['mla_kva_g'][l], W['wk'], W['wv'], kn_g, None, False, past, past)
        kvs = [(k0.reshape(b, past, -1), v0.reshape(b, past, -1))] + kvs
    br_mla = mla_attention(q, z_mla, kvs, n, min(n, ATTN_TQ))
    k_rope = z_mla[:, MLA_NOPE:MLA_NOPE + MLA_ROPE].astype(F32)

    xbc, dtp = ssd_prep(z_ssd.reshape(b, n, -1), P['ssd_conv_w'][l], P['ssd_conv_b'][l], P['ssd_dt_bias'][l],
                        P['ssd_a_log'][l], tq)
    y_f, y_b, ssd_fin = ssd_scan(xbc, dtp, ssd0)
    br_ssd = ssd_post(flat(y_f), flat(y_b), flat(xbc), z_ssd, jnp.repeat(P['ssd_d'][l], SSD_HEAD_DIM),
                      P['ssd_norm_g'][l], tm)

    a0, u0, a1, u1 = lru_prep(z_lru.reshape(b, n, -1), P['lru_conv_w'][l], P['lru_conv_b'][l], P['lru_wa'][l],
                              P['lru_ba'][l], P['lru_wx'][l], P['lru_bx'][l], P['lru_lambda'][l], tq)
    h_f, h_b, lru_fin = lru_scan(a0, u0, a1, u1, lru0, tq)
    br_lru = lru_post(flat(h_f), flat(h_b), z_lru, tm)

    y = merge(x2d, [br_rw, br_mla, br_ssd, br_lru], logits, W['w_branch'], W['w_out'], gate, n, tm)
    states = (c_kv.reshape(b, n, -1), k_rope.reshape(b, n, -1), rw_pairs_to_state(rw_fin),
              ssd_pairs_to_state(ssd_fin), lru_fin)
    return y.reshape(b, n, d), states


def kernel(x_prompt, x_sample, cache_mla_ckv, cache_mla_krope, state_rwkv, state_ssd, state_lru, c, c_ctx, ada_w, ada_b, norm_g, w_in, rw_mu, rw_w0, rw_w2, rw_a0, rw_a2, rw_kk, rw_ka, rw_rk, rw_ln_g, rw_ln_b, mla_qa_g, mla_q_up, mla_kva_g, mla_kv_up, mla_qn_g, mla_kn_g, ssd_conv_w, ssd_conv_b, ssd_dt_bias, ssd_a_log, ssd_d, ssd_norm_g, lru_conv_w, lru_conv_b, lru_wa, lru_ba, lru_wx, lru_bx, lru_lambda, w_branch, w_out):
    P = dict(norm_g=norm_g, w_in=w_in,
             rw_mu=rw_mu, rw_w0=rw_w0, rw_w2=rw_w2, rw_a0=rw_a0, rw_a2=rw_a2, rw_kk=rw_kk,
             rw_ka=rw_ka, rw_rk=rw_rk.reshape(DEPTH, BR_WIDTH), rw_ln_g=rw_ln_g, rw_ln_b=rw_ln_b,
             mla_qa_g=mla_qa_g, mla_q_up=mla_q_up, mla_kva_g=mla_kva_g, mla_kv_up=mla_kv_up,
             mla_qn_g=mla_qn_g, mla_kn_g=mla_kn_g,
             ssd_conv_w=ssd_conv_w, ssd_conv_b=ssd_conv_b, ssd_dt_bias=ssd_dt_bias,
             ssd_a_log=ssd_a_log, ssd_d=ssd_d, ssd_norm_g=ssd_norm_g,
             lru_conv_w=lru_conv_w, lru_conv_b=lru_conv_b, lru_wa=lru_wa, lru_ba=lru_ba,
             lru_wx=lru_wx, lru_bx=lru_bx, lru_lambda=lru_lambda,
             w_branch=w_branch, w_out=w_out)
    dec_batch = c.shape[0]
    cond8 = jnp.concatenate([c_ctx[None, :], c, jnp.zeros((SUBLANE - 1 - dec_batch, D_MODEL), F32)], axis=0)
    weights, mods = [], []
    for l in range(DEPTH):
        weights.append(pack_weights(P, l))
        mod = ada_mod(cond8, ada_w[l], ada_b[l])
        mods.append(tuple(jnp.split(mod[:, None, :], 3, axis=-1)))

    y_prompt = x_prompt
    ctx_states = []
    for l in range(DEPTH):
        y_prompt, st = trunk_layer(y_prompt, tuple(m[0:1] for m in mods[l]), P, weights[l], l, None)
        ctx_states.append(st)
    new_states = tuple(jnp.stack([s[i] for s in ctx_states], axis=1) for i in range(5))

    y_sample = x_sample
    for l in range(DEPTH):
        ctx = (cache_mla_ckv[:, l], cache_mla_krope[:, l], state_rwkv[:, l], state_ssd[:, l], state_lru[:, l])
        y_sample, _ = trunk_layer(y_sample, tuple(m[1:1 + dec_batch] for m in mods[l]), P, weights[l], l, ctx)

    return (y_prompt, y_sample) + new_states
```
